```python
import math
import jax, jax.numpy as jnp
from jax import lax
import numpy as np

D_MODEL = 1024
BATCH = 16
SEQ = 2048
DEPTH = 1

HEAD_DIM = 64
ROPE_DIM = HEAD_DIM // 4
ROPE_THETA = 500000.0
Q_BLOCK = 128
NSA_HEADS = 8
NSA_GROUPS = 2
NSA_HPG = NSA_HEADS // NSA_GROUPS
CMP_LEN = 32
CMP_STRIDE = 16
CMP_HIDDEN = 256
SLC_LEN = 64
SLC_TOPK = 8
WINDOW = 512
FORCE_BONUS = 1.0e4
DIFF_HEADS = 4
DIFF_V_DIM = 2 * HEAD_DIM
FFN_HIDDEN = ((8 * D_MODEL // 3 + 255) // 256) * 256
DEEPNORM_ALPHA = (2 * DEPTH) ** 0.25
DEEPNORM_BETA = (8 * DEPTH) ** -0.25
NEG = -1.0e30
LN_EPS = 1e-5
RMS_EPS = 1e-5

NSA_Q = NSA_HEADS * HEAD_DIM
NSA_KV = NSA_GROUPS * HEAD_DIM
DIFF_QK = DIFF_HEADS * 2 * HEAD_DIM
DIFF_V = DIFF_HEADS * DIFF_V_DIM
IN_WIDTHS = (NSA_Q, NSA_KV, NSA_KV, NSA_KV, NSA_KV, NSA_KV, NSA_KV, 3 * NSA_HEADS,
             DIFF_QK, DIFF_QK, DIFF_V, 2 * D_MODEL)
IN_IS_VALUE = (False, False, True, False, True, False, True, False, False, False, True, False)

kernel_name = "hybrid_nsa_diffattn_gated_deepnorm"


def layer_norm(x, g, b):
    xf = x.astype(jnp.float32)
    mu = xf.mean(-1, keepdims=True)
    var = jnp.square(xf - mu).mean(-1, keepdims=True)
    return ((xf - mu) * lax.rsqrt(var + LN_EPS) * g + b).astype(x.dtype)


def rope_partial(t, pos):
    half = ROPE_DIM // 2
    inv_freq = ROPE_THETA ** (-jnp.arange(half, dtype=jnp.float32) * 2.0 / ROPE_DIM)
    ang = pos.astype(jnp.float32)[:, None] * inv_freq[None, :]
    shape = (1, pos.shape[0]) + (1,) * (t.ndim - 3) + (half,)
    cos = jnp.cos(ang).reshape(shape).astype(t.dtype)
    sin = jnp.sin(ang).reshape(shape).astype(t.dtype)
    r1, r2, rest = t[..., :half], t[..., half:ROPE_DIM], t[..., ROPE_DIM:]
    return jnp.concatenate([r1 * cos - r2 * sin, r1 * sin + r2 * cos, rest], axis=-1)


def compress_blocks(kv, pe, w1, b1, w2):
    B, S, G, Dh = kv.shape
    chunks = kv.reshape(B, S // CMP_STRIDE, CMP_STRIDE, G, Dh)
    blocks = jnp.concatenate([chunks[:, :-1], chunks[:, 1:]], axis=2)
    blocks = blocks + pe[None, None, :, None, :]
    nc = blocks.shape[1]
    flat = blocks.transpose(0, 1, 3, 2, 4).reshape(B, nc, G, CMP_LEN * Dh)
    return jax.nn.gelu(flat @ w1 + b1) @ w2


def compressed_attention(q, k_cmp, v_cmp):
    S, Dh = q.shape[1], q.shape[-1]
    nc = k_cmp.shape[1]
    s = jnp.einsum('bsghd,bcgd->bghsc', q, k_cmp).astype(jnp.float32) * (Dh ** -0.5)
    t = jnp.arange(S)
    c_end = jnp.arange(nc) * CMP_STRIDE + CMP_LEN - 1
    mask = c_end[None, :] <= t[:, None]
    s = jnp.where(mask, s, NEG)
    p = jax.nn.softmax(s, axis=-1)
    p = jnp.where((t >= CMP_LEN - 1)[:, None], p, 0.0)
    out = jnp.einsum('bghsc,bcgd->bsghd', p.astype(v_cmp.dtype), v_cmp)
    return out, p


def select_blocks(probs, S):
    nc = probs.shape[-1]
    nsb = S // SLC_LEN
    k = min(SLC_TOPK, nsb)
    c_start = jnp.arange(nc) * CMP_STRIDE
    s_start = jnp.arange(nsb) * SLC_LEN
    overlap = jnp.clip(jnp.minimum(c_start[:, None] + CMP_LEN, s_start[None, :] + SLC_LEN)
                       - jnp.maximum(c_start[:, None], s_start[None, :]), 0, None)
    overlap = overlap.astype(jnp.float32) / CMP_LEN
    p_slc = jnp.einsum('bghsc,cj->bgsj', probs, overlap)
    t_blk = jnp.arange(S)[:, None] // SLC_LEN
    j = jnp.arange(nsb)[None, :]
    valid = j <= t_blk
    forced = (j == 0) | (j == t_blk) | (j == t_blk - 1)
    priority = jnp.where(valid, p_slc + jnp.where(forced, FORCE_BONUS, 0.0), -1.0)
    _, idx = lax.top_k(priority, k)
    return idx


def selected_attention(q, k_slc, v_slc, idx):
    B, S, G, HPG, Dh = q.shape
    nsb = S // SLC_LEN
    kk = idx.shape[-1]
    n_blk = S // Q_BLOCK
    kb = k_slc.reshape(B, nsb, SLC_LEN, G, Dh).transpose(0, 3, 1, 2, 4)
    vb = v_slc.reshape(B, nsb, SLC_LEN, G, Dh).transpose(0, 3, 1, 2, 4)
    qb = q.reshape(B, n_blk, Q_BLOCK, G, HPG, Dh).transpose(1, 0, 3, 4, 2, 5)
    ib = idx.reshape(B, G, n_blk, Q_BLOCK, kk).transpose(2, 0, 1, 3, 4)
    gather = jax.vmap(jax.vmap(lambda blocks, ids: blocks[ids]))
    offs = jnp.arange(SLC_LEN)

    def one(args):
        qi, ii, n = args
        kg = gather(kb, ii).reshape(B, G, Q_BLOCK, kk * SLC_LEN, Dh)
        vg = gather(vb, ii).reshape(B, G, Q_BLOCK, kk * SLC_LEN, Dh)
        t = n * Q_BLOCK + jnp.arange(Q_BLOCK)
        kpos = (ii[..., None] * SLC_LEN + offs).reshape(B, G, Q_BLOCK, kk * SLC_LEN)
        mask = kpos <= t[None, None, :, None]
        s = jnp.einsum('bghqd,bgqkd->bghqk', qi, kg).astype(jnp.float32) * (Dh ** -0.5)
        s = jnp.where(mask[:, :, None], s, NEG)
        p = jax.nn.softmax(s, axis=-1).astype(vg.dtype)
        return jnp.einsum('bghqk,bgqkd->bghqd', p, vg)

    out = lax.map(one, (qb, ib, jnp.arange(n_blk)))
    return out.transpose(1, 0, 4, 2, 3, 5).reshape(B, S, G, HPG, Dh)


def window_attention(q, k, v):
    B, S, G, HPG, Dh = q.shape
    n_blk = S // Q_BLOCK
    span = WINDOW + Q_BLOCK
    kp = jnp.pad(k, ((0, 0), (WINDOW, 0), (0, 0), (0, 0)))
    vp = jnp.pad(v, ((0, 0), (WINDOW, 0), (0, 0), (0, 0)))

    def one(n):
        start = n * Q_BLOCK
        qi = lax.dynamic_slice_in_dim(q, start, Q_BLOCK, axis=1)
        ki = lax.dynamic_slice_in_dim(kp, start, span, axis=1)
        vi = lax.dynamic_slice_in_dim(vp, start, span, axis=1)
        t = start + jnp.arange(Q_BLOCK)
        s_pos = start - WINDOW + jnp.arange(span)
        dist = t[:, None] - s_pos[None, :]
        mask = (dist >= 0) & (dist < WINDOW) & (s_pos[None, :] >= 0)
        s = jnp.einsum('bqghd,bkgd->bghqk', qi, ki).astype(jnp.float32) * (Dh ** -0.5)
        s = jnp.where(mask, s, NEG)
        p = jax.nn.softmax(s, axis=-1).astype(vi.dtype)
        return jnp.einsum('bghqk,bkgd->bqghd', p, vi)

    out = lax.map(one, jnp.arange(n_blk))
    return out.transpose(1, 0, 2, 3, 4, 5).reshape(B, S, G, HPG, Dh)


def diff_attention(q, k, v, lam_params, lambda_init, norm_g):
    B, S, H, _, Dh = q.shape
    n_blk = S // Q_BLOCK
    lp = lam_params.astype(jnp.float32)
    lam = jnp.exp(jnp.sum(lp[0] * lp[1])) - jnp.exp(jnp.sum(lp[2] * lp[3])) + lambda_init
    k_pos = jnp.arange(S)

    def one(n):
        start = n * Q_BLOCK
        qi = lax.dynamic_slice_in_dim(q, start, Q_BLOCK, axis=1)
        s = jnp.einsum('bqhcd,bkhcd->bhcqk', qi, k).astype(jnp.float32) * (Dh ** -0.5)
        t = start + jnp.arange(Q_BLOCK)
        s = jnp.where(k_pos[None, :] <= t[:, None], s, NEG)
        p = jax.nn.softmax(s, axis=-1)
        a = p[:, :, 0] - lam * p[:, :, 1]
        return jnp.einsum('bhqk,bkhe->bqhe', a.astype(v.dtype), v)

    out = lax.map(one, jnp.arange(n_blk))
    out = out.transpose(1, 0, 2, 3, 4).reshape(B, S, H, 2 * Dh)
    of = out.astype(jnp.float32)
    of = of * lax.rsqrt(jnp.mean(of * of, axis=-1, keepdims=True) + RMS_EPS) * norm_g
    return (of * (1.0 - lambda_init)).astype(v.dtype)


def hybrid_mixer(x, w_in, cmp_pe, cmp_w1, cmp_b1, cmp_w2, diff_lambda, diff_norm_g,
                 w_branch_a, w_branch_b, w_o, lambda_init):
    B, S, D = x.shape
    pos = jnp.arange(S)
    proj = x @ w_in
    split_at = [int(c) for c in np.cumsum(IN_WIDTHS)[:-1]]
    (q_nsa, k_c, v_c, k_s, v_s, k_w, v_w, g_nsa,
     q_df, k_df, v_df, g_merge) = jnp.split(proj, split_at, axis=-1)

    q_nsa = q_nsa.reshape(B, S, NSA_GROUPS, NSA_HPG, HEAD_DIM)
    q_rot = rope_partial(q_nsa, pos)
    kv_shape = (B, S, NSA_GROUPS, HEAD_DIM)
    k_cmp = compress_blocks(k_c.reshape(kv_shape), cmp_pe[0], cmp_w1[0], cmp_b1[0], cmp_w2[0])
    v_cmp = compress_blocks(v_c.reshape(kv_shape), cmp_pe[1], cmp_w1[1], cmp_b1[1], cmp_w2[1])
    o_cmp, probs = compressed_attention(q_nsa, k_cmp, v_cmp)
    idx = select_blocks(probs, S)
    o_slc = selected_attention(q_rot, rope_partial(k_s.reshape(kv_shape), pos), v_s.reshape(kv_shape), idx)
    o_win = window_attention(q_rot, rope_partial(k_w.reshape(kv_shape), pos), v_w.reshape(kv_shape))
    gates = jax.nn.sigmoid(g_nsa.reshape(B, S, NSA_GROUPS, NSA_HPG, 3))
    y_a = (gates[..., 0:1] * o_cmp + gates[..., 1:2] * o_slc + gates[..., 2:3] * o_win).reshape(B, S, NSA_Q)

    q_df = rope_partial(q_df.reshape(B, S, DIFF_HEADS, 2, HEAD_DIM), pos)
    k_df = rope_partial(k_df.reshape(B, S, DIFF_HEADS, 2, HEAD_DIM), pos)
    v_df = v_df.reshape(B, S, DIFF_HEADS, DIFF_V_DIM)
    y_b = diff_attention(q_df, k_df, v_df, diff_lambda, lambda_init, diff_norm_g).reshape(B, S, DIFF_V)

    gm = jax.nn.sigmoid(g_merge).reshape(B, S, 2, D)
    merged = gm[:, :, 0] * (y_a @ w_branch_a) + gm[:, :, 1] * (y_b @ w_branch_b)
    return merged @ w_o


def swiglu(x, w_gate_up, w_down):
    g, u = jnp.split(x @ w_gate_up, 2, axis=-1)
    return (jax.nn.silu(g) * u) @ w_down


def setup_inputs(seed: int = 0) -> dict:
    key = jax.random.key(seed)
    ks = jax.random.split(key, 20)
    D = D_MODEL
    beta = DEEPNORM_BETA

    def nrm(k, shape, scale):
        return jax.random.normal(k, shape, jnp.float32) * scale

    x = nrm(ks[0], (BATCH, SEQ, D), 1.0)
    in_keys = jax.random.split(ks[1], len(IN_WIDTHS))
    w_in = jnp.concatenate(
        [nrm(kk, (DEPTH, D, w), D ** -0.5 * (beta if is_v else 1.0))
         for kk, w, is_v in zip(in_keys, IN_WIDTHS, IN_IS_VALUE)], axis=-1)
    cmp_pe = nrm(ks[2], (DEPTH, 2, CMP_LEN, HEAD_DIM), 0.1)
    cmp_w1 = nrm(ks[3], (DEPTH, 2, CMP_LEN * HEAD_DIM, CMP_HIDDEN), (CMP_LEN * HEAD_DIM) ** -0.5)
    cmp_b1 = nrm(ks[4], (DEPTH, 2, CMP_HIDDEN), 0.01)
    cmp_w2 = nrm(ks[5], (DEPTH, 2, CMP_HIDDEN, HEAD_DIM), CMP_HIDDEN ** -0.5)
    diff_lambda = nrm(ks[6], (DEPTH, 4, HEAD_DIM), 0.1)
    diff_norm_g = 1.0 + nrm(ks[7], (DEPTH, DIFF_V_DIM), 0.01)
    w_branch_a = nrm(ks[8], (DEPTH, NSA_Q, D), NSA_Q ** -0.5 * beta)
    w_branch_b = nrm(ks[9], (DEPTH, DIFF_V, D), DIFF_V ** -0.5 * beta)
    w_o = nrm(ks[10], (DEPTH, D, D), D ** -0.5 * beta)
    ln1_g = 1.0 + nrm(ks[11], (DEPTH, D), 0.01)
    ln1_b = nrm(ks[12], (DEPTH, D), 0.01)
    w_gate_up = nrm(ks[13], (DEPTH, D, 2 * FFN_HIDDEN), D ** -0.5 * beta)
    w_down = nrm(ks[14], (DEPTH, FFN_HIDDEN, D), FFN_HIDDEN ** -0.5 * beta)
    ln2_g = 1.0 + nrm(ks[15], (DEPTH, D), 0.01)
    ln2_b = nrm(ks[16], (DEPTH, D), 0.01)
    return {"x": x, "w_in": w_in, "cmp_pe": cmp_pe, "cmp_w1": cmp_w1, "cmp_b1": cmp_b1,
            "cmp_w2": cmp_w2, "diff_lambda": diff_lambda, "diff_norm_g": diff_norm_g,
            "w_branch_a": w_branch_a, "w_branch_b": w_branch_b, "w_o": w_o,
            "ln1_g": ln1_g, "ln1_b": ln1_b, "w_gate_up": w_gate_up, "w_down": w_down,
            "ln2_g": ln2_g, "ln2_b": ln2_b}


def reference(x, w_in, cmp_pe, cmp_w1, cmp_b1, cmp_w2, diff_lambda, diff_norm_g,
              w_branch_a, w_branch_b, w_o, ln1_g, ln1_b, w_gate_up, w_down, ln2_g, ln2_b):
    h = x
    for l in range(DEPTH):
        lambda_init = 0.8 - 0.6 * math.exp(-0.3 * l)
        mix = hybrid_mixer(h, w_in[l], cmp_pe[l], cmp_w1[l], cmp_b1[l], cmp_w2[l],
                           diff_lambda[l], diff_norm_g[l], w_branch_a[l], w_branch_b[l],
                           w_o[l], lambda_init)
        h = layer_norm(DEEPNORM_ALPHA * h + mix, ln1_g[l], ln1_b[l])
        h = layer_norm(DEEPNORM_ALPHA * h + swiglu(h, w_gate_up[l], w_down[l]), ln2_g[l], ln2_b[l])
    return h
```

```python
import functools
import math

import numpy as np
import jax
import jax.numpy as jnp
from jax import lax
from jax.experimental import pallas as pl
from jax.experimental.pallas import tpu as pltpu

D_MODEL = 1024
HEAD_DIM = 64
ROPE_DIM = HEAD_DIM // 4
ROPE_THETA = 500000.0
NSA_HEADS = 8
NSA_GROUPS = 2
NSA_HPG = NSA_HEADS // NSA_GROUPS
CMP_LEN = 32
CMP_STRIDE = 16
CMP_HIDDEN = 256
SLC_LEN = 64
SLC_TOPK = 8
WINDOW = 512
FORCE_BONUS = 1.0e4
DIFF_HEADS = 4
DIFF_V_DIM = 2 * HEAD_DIM
FFN_HIDDEN = ((8 * D_MODEL // 3 + 255) // 256) * 256
DEPTH = 1
DEEPNORM_ALPHA = (2 * DEPTH) ** 0.25
NEG = -1.0e30
LN_EPS = 1e-5
RMS_EPS = 1e-5
LAMBDA_INIT = 0.8 - 0.6 * math.exp(-0.3 * 0)
QK_SCALE = HEAD_DIM ** -0.5

NSA_Q = NSA_HEADS * HEAD_DIM
NSA_KV = NSA_GROUPS * HEAD_DIM
DIFF_QK = DIFF_HEADS * 2 * HEAD_DIM
DIFF_V = DIFF_HEADS * DIFF_V_DIM
IN_WIDTHS = (NSA_Q, NSA_KV, NSA_KV, NSA_KV, NSA_KV, NSA_KV, NSA_KV, 3 * NSA_HEADS,
             DIFF_QK, DIFF_QK, DIFF_V, 2 * D_MODEL)

LANE = 128
T = 256
VMEM_LIMIT = 56 * 1024 * 1024

BF16 = jnp.bfloat16
F32 = jnp.float32

_C_Q, _C_KC, _C_VC, _C_KS, _C_VS, _C_KW, _C_VW = 0, 512, 640, 768, 1024, 1152, 1408
_C_QD, _C_KD, _C_VD, _C_GM, _C_GN, _C_END = 1536, 2048, 2560, 3072, 5120, 5248


def _nt(a, b):
    return lax.dot_general(a, b, (((1,), (1,)), ((), ())), preferred_element_type=F32)


def _nn(a, b):
    return jnp.dot(a, b, preferred_element_type=F32)


def _params(n_axes, vmem=None):
    return pltpu.CompilerParams(dimension_semantics=("arbitrary",) * n_axes,
                                vmem_limit_bytes=vmem)


def _inproj_kernel(x_ref, w_ref, cos_ref, sa_ref, sb_ref, oh_ref,
                   qraw_ref, qrot_ref, kc_ref, vc_ref, ksa_ref, vst_ref, kwa_ref, vwt_ref,
                   qdf_ref, kdf_ref, vdft_ref, gm_ref, gn_ref):
    xb = x_ref[...].astype(BF16)
    cos = cos_ref[...]
    sa = sa_ref[...]
    sb = sb_ref[...]

    def mm(c0, n):
        return _nn(xb, w_ref[:, c0:c0 + n])

    def rope(t):
        return t * cos + pltpu.roll(t, LANE - 8, 1) * sa + pltpu.roll(t, 8, 1) * sb

    t = mm(_C_Q, 512)
    for j in range(4):
        tj = t[:, LANE * j:LANE * (j + 1)]
        qraw_ref[:, LANE * j:LANE * (j + 1)] = (tj * QK_SCALE).astype(BF16)
        qrot_ref[:, LANE * j:LANE * (j + 1)] = (rope(tj) * QK_SCALE).astype(BF16)

    t = mm(_C_KC, 256)
    kc_ref[...] = t[:, :LANE]
    vc_ref[...] = t[:, LANE:]

    t = mm(_C_KS, 256)
    oh = oh_ref[...]
    for g in range(2):
        ksa_ref[:, LANE * g:LANE * (g + 1)] = (rope(t[:, LANE * g:LANE * (g + 1)]) + oh).astype(BF16)

    t = mm(_C_VS, 128)
    vst_ref[...] = t.T.astype(BF16)

    t = mm(_C_KW, 256)
    for g in range(2):
        kwa_ref[:, LANE * g:LANE * (g + 1)] = rope(t[:, LANE * g:LANE * (g + 1)]).astype(BF16)

    t = mm(_C_VW, 128)
    vwt_ref[...] = t.T.astype(BF16)

    t = mm(_C_QD, 512)
    for j in range(4):
        qdf_ref[:, LANE * j:LANE * (j + 1)] = (rope(t[:, LANE * j:LANE * (j + 1)]) * QK_SCALE).astype(BF16)
    t = mm(_C_KD, 512)
    for j in range(4):
        kdf_ref[:, LANE * j:LANE * (j + 1)] = rope(t[:, LANE * j:LANE * (j + 1)]).astype(BF16)
    t = mm(_C_VD, 512)
    for j in range(4):
        vdft_ref[LANE * j:LANE * (j + 1), :] = t[:, LANE * j:LANE * (j + 1)].T.astype(BF16)

    for j in range(4):
        t = mm(_C_GM + 512 * j, 512)
        gm_ref[:, 512 * j:512 * (j + 1)] = jax.nn.sigmoid(t).astype(BF16)

    gn_ref[...] = jax.nn.sigmoid(mm(_C_GN, 128))


def _inproj(x2, w_perm, cos_t, sa_t, sb_t, oh_t, seq):
    n = x2.shape[0]
    nt = n // T
    spt = seq // T
    row = lambda w: pl.BlockSpec((T, w), lambda i: (i, 0))
    tab = pl.BlockSpec((T, LANE), lambda i: (i % spt, 0))
    tile_t = lambda r: pl.BlockSpec((None, r, T), lambda i: (i, 0, 0))
    out_shape = (
        jax.ShapeDtypeStruct((n, 512), BF16),
        jax.ShapeDtypeStruct((n, 512), BF16),
        jax.ShapeDtypeStruct((n, LANE), F32),
        jax.ShapeDtypeStruct((n, LANE), F32),
        jax.ShapeDtypeStruct((n, 256), BF16),
        jax.ShapeDtypeStruct((nt, LANE, T), BF16),
        jax.ShapeDtypeStruct((n, 256), BF16),
        jax.ShapeDtypeStruct((nt, LANE, T), BF16),
        jax.ShapeDtypeStruct((n, 512), BF16),
        jax.ShapeDtypeStruct((n, 512), BF16),
        jax.ShapeDtypeStruct((nt, 512, T), BF16),
        jax.ShapeDtypeStruct((n, 2048), BF16),
        jax.ShapeDtypeStruct((n, LANE), F32),
    )
    out_specs = (row(512), row(512), row(LANE), row(LANE), row(256), tile_t(LANE), row(256),
                 tile_t(LANE), row(512), row(512), tile_t(512), row(2048), row(LANE))
    return pl.pallas_call(
        _inproj_kernel,
        out_shape=out_shape,
        grid=(nt,),
        in_specs=[row(D_MODEL),
                  pl.BlockSpec((D_MODEL, _C_END), lambda i: (0, 0)),
                  tab, tab, tab, tab],
        out_specs=out_specs,
        compiler_params=_params(1, VMEM_LIMIT),
        name="inproj",
    )(x2, w_perm, cos_t, sa_t, sb_t, oh_t)


def _gelu_tanh(x):
    c = math.sqrt(2.0 / math.pi)
    return x * (0.5 * (1.0 + jnp.tanh(c * (x + 0.044715 * (x * x * x)))))


def _compress_kernel(kf_ref, vf_ref, pet_ref, peb_ref, wt_ref, wb_ref, b1_ref, w2k_ref, w2v_ref,
                     kca_ref, vct_ref):
    def hidden(f_ref, kv):
        f = f_ref[...]
        a = _nn((f + pet_ref[kv]).astype(BF16), wt_ref[kv])
        b = _nn((f + peb_ref[kv]).astype(BF16), wb_ref[kv])
        h = a + pltpu.roll(b, 127, 0) + b1_ref[kv]
        return _gelu_tanh(h).astype(BF16)

    hk = hidden(kf_ref, 0)
    for g in range(2):
        kca_ref[g] = _nn(hk, w2k_ref[g]).astype(BF16)
    hv = hidden(vf_ref, 1)
    vct_ref[...] = _nn(hv, w2v_ref[...]).T.astype(BF16)


def _compress(kf, vf, pet, peb, wt, wb, b1, w2k, w2v):
    nb = kf.shape[0]
    full = lambda a: pl.BlockSpec(a.shape, lambda b: (0,) * a.ndim)
    return pl.pallas_call(
        _compress_kernel,
        out_shape=(jax.ShapeDtypeStruct((nb, 2, 128, LANE), BF16),
                   jax.ShapeDtypeStruct((nb, LANE, 128), BF16)),
        grid=(nb,),
        in_specs=[pl.BlockSpec((None, 128, 2048), lambda b: (b, 0, 0)),
                  pl.BlockSpec((None, 128, 2048), lambda b: (b, 0, 0)),
                  full(pet), full(peb), full(wt), full(wb), full(b1), full(w2k), full(w2v)],
        out_specs=(pl.BlockSpec((None, 2, 128, LANE), lambda b: (b, 0, 0, 0)),
                   pl.BlockSpec((None, LANE, 128), lambda b: (b, 0, 0))),
        compiler_params=_params(1, VMEM_LIMIT),
        name="compress",
    )(kf, vf, pet, peb, wt, wb, b1, w2k, w2v)


def _cmpsel_kernel(q_ref, kca_ref, vct_ref, ocmp_ref, sel_ref):
    i = pl.program_id(1)
    t0 = i * T
    lane = lax.broadcasted_iota(jnp.int32, (T, LANE), 1)
    crow = lax.broadcasted_iota(jnp.int32, (128, T), 0)
    tcol = lax.broadcasted_iota(jnp.int32, (128, T), 1) + t0
    cmask = (crow * CMP_STRIDE + (CMP_LEN - 1)) <= tcol
    live = (lax.broadcasted_iota(jnp.int32, (1, T), 1) + t0) >= (CMP_LEN - 1)

    jj = lax.broadcasted_iota(jnp.int32, (32, 128), 0) * SLC_LEN
    cc = lax.broadcasted_iota(jnp.int32, (32, 128), 1) * CMP_STRIDE
    ov = jnp.maximum(jnp.minimum(cc + CMP_LEN, jj + SLC_LEN) - jnp.maximum(cc, jj), 0)
    ovt = (ov.astype(F32) * (1.0 / CMP_LEN)).astype(BF16)

    jrow = lax.broadcasted_iota(jnp.int32, (32, T), 0)
    tblk = (lax.broadcasted_iota(jnp.int32, (32, T), 1) + t0) // SLC_LEN
    valid = jrow <= tblk
    forced = (jrow == 0) | (jrow == tblk) | (jrow == tblk - 1)
    bonus = jnp.where(forced, FORCE_BONUS, 0.0)

    for g in range(2):
        kc = kca_ref[g]
        vt = vct_ref[64 * g:64 * (g + 1), :]
        psum = jnp.zeros((128, T), F32)
        for hp in range(2):
            qp = q_ref[:, LANE * (2 * g + hp):LANE * (2 * g + hp + 1)].astype(F32)
            qs = (jnp.where(lane < 64, qp, 0.0), jnp.where(lane < 64, pltpu.roll(qp, 64, 1), 0.0))
            outs = []
            for par in range(2):
                s = _nt(kc, qs[par].astype(BF16))
                s = jnp.where(cmask, s, NEG)
                m = jnp.max(s, axis=0, keepdims=True)
                e = jnp.exp(s - m)
                p = e / jnp.sum(e, axis=0, keepdims=True)
                p = jnp.where(live, p, 0.0)
                psum = psum + p
                outs.append(_nn(vt, p.astype(BF16)))
            ocmp_ref[:, LANE * (2 * g + hp):LANE * (2 * g + hp + 1)] = (
                jnp.concatenate(outs, axis=0).T.astype(BF16))

        p_hi = psum.astype(BF16)
        p_lo = (psum - p_hi.astype(F32)).astype(BF16)
        pslc = _nn(ovt, p_hi) + _nn(ovt, p_lo)
        pri = jnp.where(valid, pslc + bonus, -1.0)
        rank = jnp.zeros((32, T), F32)
        for r in range(32):
            row = pri[r:r + 1, :]
            rank = rank + jnp.where(row > pri, 1.0, 0.0) + jnp.where((row == pri) & (jrow > r), 1.0, 0.0)
        selneg = jnp.where(rank < float(SLC_TOPK), 0.0, NEG)
        pad = jnp.concatenate([jnp.zeros((64, T), F32), selneg, jnp.zeros((32, T), F32)], axis=0)
        sel_ref[:, LANE * g:LANE * (g + 1)] = pad.T.astype(BF16)


def _cmpsel(qraw, kca, vct, nb, seq):
    n = qraw.shape[0]
    nq = seq // T
    return pl.pallas_call(
        _cmpsel_kernel,
        out_shape=(jax.ShapeDtypeStruct((n, 512), BF16), jax.ShapeDtypeStruct((n, 256), BF16)),
        grid=(nb, nq),
        in_specs=[pl.BlockSpec((T, 512), lambda b, i: (b * nq + i, 0)),
                  pl.BlockSpec((None, 2, 128, LANE), lambda b, i: (b, 0, 0, 0)),
                  pl.BlockSpec((None, LANE, 128), lambda b, i: (b, 0, 0))],
        out_specs=(pl.BlockSpec((T, 512), lambda b, i: (b * nq + i, 0)),
                   pl.BlockSpec((T, 256), lambda b, i: (b * nq + i, 0))),
        compiler_params=_params(2, VMEM_LIMIT),
        name="cmpsel",
    )(qraw, kca, vct)


def _split_heads(q_ref, extra, qs_scr):
    lane = lax.broadcasted_iota(jnp.int32, (T, LANE), 1)
    for hp in range(2):
        qp = q_ref[:, LANE * hp:LANE * (hp + 1)].astype(F32)
        qs_scr[2 * hp] = jnp.where(lane < 64, qp, extra).astype(BF16)
        qs_scr[2 * hp + 1] = jnp.where(lane < 64, pltpu.roll(qp, 64, 1), extra).astype(BF16)


def _flash_step(s, vt, m_ref, l_ref, acc_ref):
    m_prev = m_ref[...]
    m_new = jnp.maximum(m_prev, jnp.max(s, axis=0, keepdims=True))
    alpha = jnp.exp(m_prev - m_new)
    p = jnp.exp(s - m_new)
    l_ref[...] = alpha * l_ref[...] + jnp.sum(p, axis=0, keepdims=True)
    m_ref[...] = m_new
    acc_ref[...] = acc_ref[...] * alpha + _nn(vt, p.astype(BF16))


def _slc_kernel(q_ref, sel_ref, k_ref, vt_ref, o_ref, qs_scr, m_scr, l_scr, acc_scr):
    i = pl.program_id(2)
    _split_heads(q_ref, sel_ref[...].astype(F32), qs_scr)
    m_scr[...] = jnp.full(m_scr.shape, NEG, F32)
    l_scr[...] = jnp.zeros(l_scr.shape, F32)
    acc_scr[...] = jnp.zeros(acc_scr.shape, F32)
    causal = (lax.broadcasted_iota(jnp.int32, (T, T), 0) <= lax.broadcasted_iota(jnp.int32, (T, T), 1))

    def tile(j, diag):
        kt = k_ref[j]
        vt = vt_ref[j]
        for h in range(4):
            s = _nt(kt, qs_scr[h])
            if diag:
                s = jnp.where(causal, s, NEG)
            _flash_step(s, vt, m_scr.at[h], l_scr.at[h],
                        acc_scr.at[h // 2, 64 * (h % 2):64 * (h % 2 + 1), :])

    def body(j, c):
        tile(j, False)
        return c

    lax.fori_loop(0, i, body, 0)
    tile(i, True)
    for hp in range(2):
        inv = jnp.concatenate([jnp.broadcast_to(1.0 / l_scr[2 * hp], (64, T)),
                               jnp.broadcast_to(1.0 / l_scr[2 * hp + 1], (64, T))], axis=0)
        o_ref[:, LANE * hp:LANE * (hp + 1)] = (acc_scr[hp] * inv).T.astype(BF16)


def _slc(qrot, sel, ksa3, vst, nb, seq):
    n = qrot.shape[0]
    nq = seq // T
    return pl.pallas_call(
        _slc_kernel,
        out_shape=jax.ShapeDtypeStruct((n, 512), BF16),
        grid=(nb, NSA_GROUPS, nq),
        in_specs=[pl.BlockSpec((T, 256), lambda b, g, i: (b * nq + i, g)),
                  pl.BlockSpec((T, LANE), lambda b, g, i: (b * nq + i, g)),
                  pl.BlockSpec((nq, T, LANE), lambda b, g, i: (b, 0, g)),
                  pl.BlockSpec((nq, 64, T), lambda b, g, i: (b, g, 0))],
        out_specs=pl.BlockSpec((T, 256), lambda b, g, i: (b * nq + i, g)),
        scratch_shapes=[pltpu.VMEM((4, T, LANE), BF16),
                        pltpu.VMEM((4, 1, T), F32),
                        pltpu.VMEM((4, 1, T), F32),
                        pltpu.VMEM((2, LANE, T), F32)],
        compiler_params=_params(3, VMEM_LIMIT),
        name="slc",
    )(qrot, sel, ksa3, vst)


def _win_kernel(q_ref, k_ref, vt_ref, o_ref, qs_scr):
    i = pl.program_id(2)
    _split_heads(q_ref, 0.0, qs_scr)
    krow = lax.broadcasted_iota(jnp.int32, (T, T), 0)
    qcol = lax.broadcasted_iota(jnp.int32, (T, T), 1)
    span = WINDOW // T
    masks = {0: krow <= qcol, span: krow > qcol}
    outs = []
    for h in range(4):
        q = qs_scr[h]
        ss = []
        for d in range(span + 1):
            j = i - d
            s = _nt(k_ref[jnp.maximum(j, 0)], q)
            if d in masks:
                s = jnp.where(masks[d], s, NEG)
            if d > 0:
                s = s + jnp.where(j >= 0, 0.0, NEG)
            ss.append(s)
        m = functools.reduce(jnp.maximum, [jnp.max(s, axis=0, keepdims=True) for s in ss])
        l = jnp.zeros((1, T), F32)
        acc = jnp.zeros((64, T), F32)
        for d in range(span + 1):
            p = jnp.exp(ss[d] - m)
            l = l + jnp.sum(p, axis=0, keepdims=True)
            acc = acc + _nn(vt_ref[jnp.maximum(i - d, 0)], p.astype(BF16))
        outs.append(acc * (1.0 / l))
    for hp in range(2):
        o_ref[:, LANE * hp:LANE * (hp + 1)] = (
            jnp.concatenate(outs[2 * hp:2 * hp + 2], axis=0).T.astype(BF16))


def _win(qrot, kwa3, vwt, nb, seq):
    n = qrot.shape[0]
    nq = seq // T
    return pl.pallas_call(
        _win_kernel,
        out_shape=jax.ShapeDtypeStruct((n, 512), BF16),
        grid=(nb, NSA_GROUPS, nq),
        in_specs=[pl.BlockSpec((T, 256), lambda b, g, i: (b * nq + i, g)),
                  pl.BlockSpec((nq, T, LANE), lambda b, g, i: (b, 0, g)),
                  pl.BlockSpec((nq, 64, T), lambda b, g, i: (b, g, 0))],
        out_specs=pl.BlockSpec((T, 256), lambda b, g, i: (b * nq + i, g)),
        scratch_shapes=[pltpu.VMEM((4, T, LANE), BF16)],
        compiler_params=_params(3, VMEM_LIMIT),
        name="win",
    )(qrot, kwa3, vwt)


def _diff_kernel(q_ref, k_ref, vt_ref, lam_ref, g_ref, o_ref, qs_scr, m_scr, l_scr, acc_scr):
    i = pl.program_id(2)
    lane = lax.broadcasted_iota(jnp.int32, (T, LANE), 1)
    q = q_ref[...].astype(F32)
    qs_scr[0] = jnp.where(lane < 64, q, 0.0).astype(BF16)
    qs_scr[1] = jnp.where(lane < 64, 0.0, q).astype(BF16)
    m_scr[...] = jnp.full(m_scr.shape, NEG, F32)
    l_scr[...] = jnp.zeros(l_scr.shape, F32)
    acc_scr[...] = jnp.zeros(acc_scr.shape, F32)
    causal = (lax.broadcasted_iota(jnp.int32, (T, T), 0) <= lax.broadcasted_iota(jnp.int32, (T, T), 1))

    def tile(j, diag):
        kt = k_ref[j]
        vt = vt_ref[j]
        for c in range(2):
            s = _nt(kt, qs_scr[c])
            if diag:
                s = jnp.where(causal, s, NEG)
            _flash_step(s, vt, m_scr.at[c], l_scr.at[c], acc_scr.at[c])

    def body(j, c):
        tile(j, False)
        return c

    lax.fori_loop(0, i, body, 0)
    tile(i, True)

    lp = lam_ref[...]
    lam = (jnp.exp(jnp.sum(lp[0:1] * lp[1:2], axis=1, keepdims=True))
           - jnp.exp(jnp.sum(lp[2:3] * lp[3:4], axis=1, keepdims=True)) + LAMBDA_INIT)
    o = acc_scr[0] * (1.0 / l_scr[0]) - lam * (acc_scr[1] * (1.0 / l_scr[1]))
    o = o * lax.rsqrt(jnp.mean(o * o, axis=0, keepdims=True) + RMS_EPS)
    o_ref[...] = ((o.T * g_ref[...]) * (1.0 - LAMBDA_INIT)).astype(BF16)


def _diff(qdf, kdf3, vdft, lam, norm_g, nb, seq):
    n = qdf.shape[0]
    nq = seq // T
    return pl.pallas_call(
        _diff_kernel,
        out_shape=jax.ShapeDtypeStruct((n, 512), BF16),
        grid=(nb, DIFF_HEADS, nq),
        in_specs=[pl.BlockSpec((T, LANE), lambda b, h, i: (b * nq + i, h)),
                  pl.BlockSpec((nq, T, LANE), lambda b, h, i: (b, 0, h)),
                  pl.BlockSpec((nq, LANE, T), lambda b, h, i: (b, h, 0)),
                  pl.BlockSpec((4, HEAD_DIM), lambda b, h, i: (0, 0)),
                  pl.BlockSpec((1, LANE), lambda b, h, i: (0, 0))],
        out_specs=pl.BlockSpec((T, LANE), lambda b, h, i: (b * nq + i, h)),
        scratch_shapes=[pltpu.VMEM((2, T, LANE), BF16),
                        pltpu.VMEM((2, 1, T), F32),
                        pltpu.VMEM((2, 1, T), F32),
                        pltpu.VMEM((2, LANE, T), F32)],
        compiler_params=_params(3, VMEM_LIMIT),
        name="diff",
    )(qdf, kdf3, vdft, lam, norm_g)


def _layer_norm(z, g, b):
    mu = jnp.mean(z, axis=-1, keepdims=True)
    zc = z - mu
    var = jnp.mean(zc * zc, axis=-1, keepdims=True)
    return zc * lax.rsqrt(var + LN_EPS) * g + b


def _merge_kernel(x_ref, ocmp_ref, oslc_ref, owin_ref, gn_ref, yb_ref, gm_ref, e_ref,
                  wa_ref, wb_ref, wo_ref, g_ref, b_ref, h_ref):
    gn = gn_ref[...]
    gn_hi = gn.astype(BF16)
    gn_lo = (gn - gn_hi.astype(F32)).astype(BF16)
    e = e_ref[...]
    gx = _nn(gn_hi, e) + _nn(gn_lo, e)
    ya = (gx[:, 0:512] * ocmp_ref[...].astype(F32)
          + gx[:, 512:1024] * oslc_ref[...].astype(F32)
          + gx[:, 1024:1536] * owin_ref[...].astype(F32))
    ta = _nn(ya.astype(BF16), wa_ref[...])
    tb = _nn(yb_ref[...], wb_ref[...])
    merged = gm_ref[:, 0:D_MODEL].astype(F32) * ta + gm_ref[:, D_MODEL:2 * D_MODEL].astype(F32) * tb
    mix = _nn(merged.astype(BF16), wo_ref[...])
    h_ref[...] = _layer_norm(DEEPNORM_ALPHA * x_ref[...] + mix, g_ref[...], b_ref[...])


def _merge(x2, ocmp, oslc, owin, gn, yb, gm, e, wa, wb, wo, g, b):
    n = x2.shape[0]
    row = lambda w: pl.BlockSpec((T, w), lambda i: (i, 0))
    full = lambda a: pl.BlockSpec(a.shape, lambda i: (0,) * a.ndim)
    return pl.pallas_call(
        _merge_kernel,
        out_shape=jax.ShapeDtypeStruct((n, D_MODEL), F32),
        grid=(n // T,),
        in_specs=[row(D_MODEL), row(512), row(512), row(512), row(LANE), row(512), row(2048),
                  full(e), full(wa), full(wb), full(wo), full(g), full(b)],
        out_specs=row(D_MODEL),
        compiler_params=_params(1, VMEM_LIMIT),
        name="merge",
    )(x2, ocmp, oslc, owin, gn, yb, gm, e, wa, wb, wo, g, b)


def _ffn_kernel(h_ref, wgu_ref, wd_ref, g_ref, b_ref, o_ref):
    h = h_ref[...]
    hb = h.astype(BF16)
    gate = _nn(hb, wgu_ref[:, 0:FFN_HIDDEN])
    up = _nn(hb, wgu_ref[:, FFN_HIDDEN:2 * FFN_HIDDEN])
    act = (gate * jax.nn.sigmoid(gate) * up).astype(BF16)
    y = _nn(act, wd_ref[...])
    o_ref[...] = _layer_norm(DEEPNORM_ALPHA * h + y, g_ref[...], b_ref[...])


def _ffn(h1, wgu, wd, g, b):
    n = h1.shape[0]
    row = pl.BlockSpec((T, D_MODEL), lambda i: (i, 0))
    full = lambda a: pl.BlockSpec(a.shape, lambda i: (0,) * a.ndim)
    return pl.pallas_call(
        _ffn_kernel,
        out_shape=jax.ShapeDtypeStruct((n, D_MODEL), F32),
        grid=(n // T,),
        in_specs=[row, full(wgu), full(wd), full(g), full(b)],
        out_specs=row,
        compiler_params=_params(1, VMEM_LIMIT),
        name="ffn",
    )(h1, wgu, wd, g, b)


def _rope_tables(seq):
    half = ROPE_DIM // 2
    inv_freq = ROPE_THETA ** (-jnp.arange(half, dtype=F32) * 2.0 / ROPE_DIM)
    ang = jnp.arange(seq, dtype=F32)[:, None] * inv_freq[None, :]
    cos, sin = jnp.cos(ang), jnp.sin(ang)
    ones = jnp.ones((seq, HEAD_DIM - ROPE_DIM), F32)
    zeros8 = jnp.zeros((seq, half), F32)
    zeros48 = jnp.zeros((seq, HEAD_DIM - ROPE_DIM), F32)
    c64 = jnp.concatenate([cos, cos, ones], axis=1)
    sa64 = jnp.concatenate([-sin, zeros8, zeros48], axis=1)
    sb64 = jnp.concatenate([zeros8, sin, zeros48], axis=1)
    rep = lambda a: jnp.concatenate([a, a], axis=1)
    pos_blk = jnp.arange(seq, dtype=jnp.int32)[:, None] // SLC_LEN
    lane = jnp.arange(LANE, dtype=jnp.int32)[None, :]
    onehot = ((lane >= 64) & (lane < 96) & (lane - 64 == pos_blk)).astype(F32)
    return rep(c64), rep(sa64), rep(sb64), onehot


def _prep_w_in(w):
    offs = np.cumsum((0,) + IN_WIDTHS)
    q, kc, vc, ks, vs, kw, vw, gn, qd, kd, vd, gm = [w[:, offs[i]:offs[i + 1]] for i in range(12)]
    z64 = jnp.zeros((w.shape[0], 64), w.dtype)
    ksa = jnp.concatenate([ks[:, :64], z64, ks[:, 64:], z64], axis=1)
    kwa = jnp.concatenate([kw[:, :64], z64, kw[:, 64:], z64], axis=1)
    gnp = jnp.pad(gn, ((0, 0), (0, LANE - gn.shape[1])))
    return jnp.concatenate([q, kc, vc, ksa, vs, kwa, vw, qd, kd, vd, gm, gnp], axis=1).astype(BF16)


def _prep_compress(cmp_pe, cmp_w1, cmp_b1, cmp_w2):
    half = CMP_LEN // 2
    pet, peb, wt, wb, b1 = [], [], [], [], []
    for kv in range(2):
        pe = cmp_pe[kv]
        tile2 = lambda a: jnp.concatenate([a, a], axis=1).reshape(1, half * 2 * HEAD_DIM)
        pet.append(tile2(pe[:half]))
        peb.append(tile2(pe[half:]))
        w1 = cmp_w1[kv].reshape(CMP_LEN, HEAD_DIM, CMP_HIDDEN)
        z = jnp.zeros((half, HEAD_DIM, CMP_HIDDEN), w1.dtype)

        def spread(wh):
            g0 = jnp.concatenate([wh, z], axis=2)
            g1 = jnp.concatenate([z, wh], axis=2)
            return jnp.stack([g0, g1], axis=1).reshape(half * 2 * HEAD_DIM, 2 * CMP_HIDDEN)

        wt.append(spread(w1[:half]))
        wb.append(spread(w1[half:]))
        b1.append(jnp.concatenate([cmp_b1[kv], cmp_b1[kv]])[None, :])
    w2k, w2v = cmp_w2[0], cmp_w2[1]
    zk = jnp.zeros_like(w2k)
    w2k_g = jnp.stack([
        jnp.concatenate([jnp.concatenate([w2k, zk], axis=1), jnp.zeros((CMP_HIDDEN, LANE), w2k.dtype)], axis=0),
        jnp.concatenate([jnp.zeros((CMP_HIDDEN, LANE), w2k.dtype), jnp.concatenate([w2k, zk], axis=1)], axis=0),
    ])
    zv = jnp.zeros_like(w2v)
    w2v_bd = jnp.concatenate([jnp.concatenate([w2v, zv], axis=1),
                              jnp.concatenate([zv, w2v], axis=1)], axis=0)
    st = lambda xs: jnp.stack(xs)
    return (st(pet), st(peb), st(wt).astype(BF16), st(wb).astype(BF16), st(b1),
            w2k_g.astype(BF16), w2v_bd.astype(BF16))


def _gate_expand():
    e = np.zeros((LANE, 3 * NSA_Q), np.float32)
    for h in range(NSA_HEADS):
        for k in range(3):
            e[h * 3 + k, k * NSA_Q + h * HEAD_DIM:k * NSA_Q + (h + 1) * HEAD_DIM] = 1.0
    return jnp.asarray(e, BF16)


def kernel(x, w_in, cmp_pe, cmp_w1, cmp_b1, cmp_w2, diff_lambda, diff_norm_g, w_branch_a, w_branch_b,
           w_o, ln1_g, ln1_b, w_gate_up, w_down, ln2_g, ln2_b):
    nb, seq, d = x.shape
    assert d == D_MODEL and seq % T == 0 and seq // SLC_LEN == 32 and seq // CMP_STRIDE == 128
    n = nb * seq
    x2 = x.reshape(n, d)
    cos_t, sa_t, sb_t, oh_t = _rope_tables(seq)
    (qraw, qrot, kcs, vcs, ksa, vst, kwa, vwt, qdf, kdf, vdft, gm, gn) = _inproj(
        x2, _prep_w_in(w_in[0]), cos_t, sa_t, sb_t, oh_t, seq)

    pet, peb, wt, wb, b1, w2k, w2v = _prep_compress(cmp_pe[0], cmp_w1[0], cmp_b1[0], cmp_w2[0])
    nchunk = seq // CMP_STRIDE
    kca, vct = _compress(kcs.reshape(nb, nchunk, CMP_STRIDE * LANE), vcs.reshape(nb, nchunk, CMP_STRIDE * LANE),
                         pet, peb, wt, wb, b1, w2k, w2v)

    ocmp, sel = _cmpsel(qraw, kca, vct, nb, seq)
    nt = n // T
    oslc = _slc(qrot, sel, ksa.reshape(nt, T, 256), vst, nb, seq)
    owin = _win(qrot, kwa.reshape(nt, T, 256), vwt, nb, seq)
    yb = _diff(qdf, kdf.reshape(nt, T, 512), vdft, diff_lambda[0], diff_norm_g[0][None, :], nb, seq)

    h1 = _merge(x2, ocmp, oslc, owin, gn, yb, gm, _gate_expand(),
                w_branch_a[0].astype(BF16), w_branch_b[0].astype(BF16), w_o[0].astype(BF16),
                ln1_g[0][None, :], ln1_b[0][None, :])
    out = _ffn(h1, w_gate_up[0].astype(BF16), w_down[0].astype(BF16), ln2_g[0][None, :], ln2_b[0][None, :])
    return out.reshape(nb, seq, d)
```

```python
import functools
import math

import numpy as np
import jax
import jax.numpy as jnp
from jax import lax
from jax.experimental import pallas as pl
from jax.experimental.pallas import tpu as pltpu

D_MODEL = 1024
HEAD_DIM = 64
ROPE_DIM = HEAD_DIM // 4
ROPE_THETA = 500000.0
NSA_HEADS = 8
NSA_GROUPS = 2
NSA_HPG = NSA_HEADS // NSA_GROUPS
CMP_LEN = 32
CMP_STRIDE = 16
CMP_HIDDEN = 256
SLC_LEN = 64
SLC_TOPK = 8
WINDOW = 512
FORCE_BONUS = 1.0e4
DIFF_HEADS = 4
DIFF_V_DIM = 2 * HEAD_DIM
FFN_HIDDEN = ((8 * D_MODEL // 3 + 255) // 256) * 256
DEPTH = 1
DEEPNORM_ALPHA = (2 * DEPTH) ** 0.25
NEG = -1.0e30
LN_EPS = 1e-5
RMS_EPS = 1e-5
LAMBDA_INIT = 0.8 - 0.6 * math.exp(-0.3 * 0)
QK_SCALE = HEAD_DIM ** -0.5
QK_SCALE_LOG2 = QK_SCALE * math.log2(math.e)

NSA_Q = NSA_HEADS * HEAD_DIM
NSA_KV = NSA_GROUPS * HEAD_DIM
DIFF_QK = DIFF_HEADS * 2 * HEAD_DIM
DIFF_V = DIFF_HEADS * DIFF_V_DIM
IN_WIDTHS = (NSA_Q, NSA_KV, NSA_KV, NSA_KV, NSA_KV, NSA_KV, NSA_KV, 3 * NSA_HEADS,
             DIFF_QK, DIFF_QK, DIFF_V, 2 * D_MODEL)

LANE = 128
T = 256
VMEM_LIMIT = 56 * 1024 * 1024

BF16 = jnp.bfloat16
F32 = jnp.float32

_C_Q, _C_KC, _C_VC, _C_KS, _C_VS, _C_KW, _C_VW = 0, 512, 640, 768, 1024, 1152, 1408
_C_QD, _C_KD, _C_VD, _C_GM, _C_GN, _C_END = 1536, 2048, 2560, 3072, 5120, 5248


def _nt(a, b):
    return lax.dot_general(a, b, (((1,), (1,)), ((), ())), preferred_element_type=F32)


def _nn(a, b):
    return jnp.dot(a, b, preferred_element_type=F32)


def _params(n_axes, vmem=None):
    return pltpu.CompilerParams(dimension_semantics=("arbitrary",) * n_axes,
                                vmem_limit_bytes=vmem)


def _inproj_kernel(x_ref, w_ref, cos_ref, sa_ref, sb_ref, oh_ref,
                   qraw_ref, qrot_ref, kc_ref, vc_ref, ksa_ref, vst_ref, kwa_ref, vwt_ref,
                   qdf_ref, kdf_ref, vdft_ref, gm_ref, gn_ref):
    xb = x_ref[...].astype(BF16)
    cos = cos_ref[...]
    sa = sa_ref[...]
    sb = sb_ref[...]

    def mm(c0, n):
        return _nn(xb, w_ref[:, c0:c0 + n])

    def rope(t):
        return t * cos + pltpu.roll(t, LANE - 8, 1) * sa + pltpu.roll(t, 8, 1) * sb

    t = mm(_C_Q, 512)
    for j in range(4):
        tj = t[:, LANE * j:LANE * (j + 1)]
        qraw_ref[:, LANE * j:LANE * (j + 1)] = (tj * QK_SCALE).astype(BF16)
        qrot_ref[:, LANE * j:LANE * (j + 1)] = (rope(tj) * QK_SCALE_LOG2).astype(BF16)

    t = mm(_C_KC, 256)
    kc_ref[...] = t[:, :LANE]
    vc_ref[...] = t[:, LANE:]

    t = mm(_C_KS, 256)
    oh = oh_ref[...]
    for g in range(2):
        ksa_ref[:, LANE * g:LANE * (g + 1)] = (rope(t[:, LANE * g:LANE * (g + 1)]) + oh).astype(BF16)

    t = mm(_C_VS, 128)
    vst_ref[...] = t.T.astype(BF16)

    t = mm(_C_KW, 256)
    for g in range(2):
        kwa_ref[:, LANE * g:LANE * (g + 1)] = rope(t[:, LANE * g:LANE * (g + 1)]).astype(BF16)

    t = mm(_C_VW, 128)
    vwt_ref[...] = t.T.astype(BF16)

    t = mm(_C_QD, 512)
    for j in range(4):
        qdf_ref[:, LANE * j:LANE * (j + 1)] = (rope(t[:, LANE * j:LANE * (j + 1)]) * QK_SCALE_LOG2).astype(BF16)
    t = mm(_C_KD, 512)
    for j in range(4):
        kdf_ref[:, LANE * j:LANE * (j + 1)] = rope(t[:, LANE * j:LANE * (j + 1)]).astype(BF16)
    t = mm(_C_VD, 512)
    for j in range(4):
        vdft_ref[LANE * j:LANE * (j + 1), :] = t[:, LANE * j:LANE * (j + 1)].T.astype(BF16)

    for j in range(4):
        t = mm(_C_GM + 512 * j, 512)
        gm_ref[:, 512 * j:512 * (j + 1)] = jax.nn.sigmoid(t).astype(BF16)

    gn_ref[...] = jax.nn.sigmoid(mm(_C_GN, 128))


def _inproj(x2, w_perm, cos_t, sa_t, sb_t, oh_t, seq):
    n = x2.shape[0]
    nt = n // T
    spt = seq // T
    row = lambda w: pl.BlockSpec((T, w), lambda i: (i, 0))
    tab = pl.BlockSpec((T, LANE), lambda i: (i % spt, 0))
    tile_t = lambda r: pl.BlockSpec((None, r, T), lambda i: (i, 0, 0))
    out_shape = (
        jax.ShapeDtypeStruct((n, 512), BF16),
        jax.ShapeDtypeStruct((n, 512), BF16),
        jax.ShapeDtypeStruct((n, LANE), F32),
        jax.ShapeDtypeStruct((n, LANE), F32),
        jax.ShapeDtypeStruct((n, 256), BF16),
        jax.ShapeDtypeStruct((nt, LANE, T), BF16),
        jax.ShapeDtypeStruct((n, 256), BF16),
        jax.ShapeDtypeStruct((nt, LANE, T), BF16),
        jax.ShapeDtypeStruct((n, 512), BF16),
        jax.ShapeDtypeStruct((n, 512), BF16),
        jax.ShapeDtypeStruct((nt, 512, T), BF16),
        jax.ShapeDtypeStruct((n, 2048), BF16),
        jax.ShapeDtypeStruct((n, LANE), F32),
    )
    out_specs = (row(512), row(512), row(LANE), row(LANE), row(256), tile_t(LANE), row(256),
                 tile_t(LANE), row(512), row(512), tile_t(512), row(2048), row(LANE))
    return pl.pallas_call(
        _inproj_kernel,
        out_shape=out_shape,
        grid=(nt,),
        in_specs=[row(D_MODEL),
                  pl.BlockSpec((D_MODEL, _C_END), lambda i: (0, 0)),
                  tab, tab, tab, tab],
        out_specs=out_specs,
        compiler_params=_params(1, VMEM_LIMIT),
        name="inproj",
    )(x2, w_perm, cos_t, sa_t, sb_t, oh_t)


def _gelu_tanh(x):
    c = math.sqrt(2.0 / math.pi)
    return x * (0.5 * (1.0 + jnp.tanh(c * (x + 0.044715 * (x * x * x)))))


def _compress_kernel(kf_ref, vf_ref, pet_ref, peb_ref, wt_ref, wb_ref, b1_ref, w2k_ref, w2v_ref,
                     kca_ref, vct_ref):
    def hidden(f_ref, kv):
        f = f_ref[...]
        a = _nn((f + pet_ref[kv]).astype(BF16), wt_ref[kv])
        b = _nn((f + peb_ref[kv]).astype(BF16), wb_ref[kv])
        h = a + pltpu.roll(b, 127, 0) + b1_ref[kv]
        return _gelu_tanh(h).astype(BF16)

    hk = hidden(kf_ref, 0)
    for g in range(2):
        kca_ref[g] = _nn(hk, w2k_ref[g]).astype(BF16)
    hv = hidden(vf_ref, 1)
    vct_ref[...] = _nn(hv, w2v_ref[...]).T.astype(BF16)


def _compress(kf, vf, pet, peb, wt, wb, b1, w2k, w2v):
    nb = kf.shape[0]
    full = lambda a: pl.BlockSpec(a.shape, lambda b: (0,) * a.ndim)
    return pl.pallas_call(
        _compress_kernel,
        out_shape=(jax.ShapeDtypeStruct((nb, 2, 128, LANE), BF16),
                   jax.ShapeDtypeStruct((nb, LANE, 128), BF16)),
        grid=(nb,),
        in_specs=[pl.BlockSpec((None, 128, 2048), lambda b: (b, 0, 0)),
                  pl.BlockSpec((None, 128, 2048), lambda b: (b, 0, 0)),
                  full(pet), full(peb), full(wt), full(wb), full(b1), full(w2k), full(w2v)],
        out_specs=(pl.BlockSpec((None, 2, 128, LANE), lambda b: (b, 0, 0, 0)),
                   pl.BlockSpec((None, LANE, 128), lambda b: (b, 0, 0))),
        compiler_params=_params(1, VMEM_LIMIT),
        name="compress",
    )(kf, vf, pet, peb, wt, wb, b1, w2k, w2v)


def _cmpsel_kernel(q_ref, kca_ref, vct_ref, ocmp_ref, sel_ref):
    i = pl.program_id(1)
    t0 = i * T
    lane = lax.broadcasted_iota(jnp.int32, (T, LANE), 1)
    crow = lax.broadcasted_iota(jnp.int32, (128, T), 0)
    tcol = lax.broadcasted_iota(jnp.int32, (128, T), 1) + t0
    cmask = (crow * CMP_STRIDE + (CMP_LEN - 1)) <= tcol
    live = (lax.broadcasted_iota(jnp.int32, (1, T), 1) + t0) >= (CMP_LEN - 1)

    jj = lax.broadcasted_iota(jnp.int32, (32, 128), 0) * SLC_LEN
    cc = lax.broadcasted_iota(jnp.int32, (32, 128), 1) * CMP_STRIDE
    ov = jnp.maximum(jnp.minimum(cc + CMP_LEN, jj + SLC_LEN) - jnp.maximum(cc, jj), 0)
    ovt = (ov.astype(F32) * (1.0 / CMP_LEN)).astype(BF16)

    jrow = lax.broadcasted_iota(jnp.int32, (32, T), 0)
    tblk = (lax.broadcasted_iota(jnp.int32, (32, T), 1) + t0) // SLC_LEN
    valid = jrow <= tblk
    forced = (jrow == 0) | (jrow == tblk) | (jrow == tblk - 1)
    bonus = jnp.where(forced, FORCE_BONUS, 0.0)

    for g in range(2):
        kc = kca_ref[g]
        vt = vct_ref[64 * g:64 * (g + 1), :]
        psum = jnp.zeros((128, T), F32)
        for hp in range(2):
            qp = q_ref[:, LANE * (2 * g + hp):LANE * (2 * g + hp + 1)].astype(F32)
            qs = (jnp.where(lane < 64, qp, 0.0), jnp.where(lane < 64, pltpu.roll(qp, 64, 1), 0.0))
            outs = []
            for par in range(2):
                s = _nt(kc, qs[par].astype(BF16))
                s = jnp.where(cmask, s, NEG)
                m = jnp.max(s, axis=0, keepdims=True)
                e = jnp.exp(s - m)
                p = e / jnp.sum(e, axis=0, keepdims=True)
                p = jnp.where(live, p, 0.0)
                psum = psum + p
                outs.append(_nn(vt, p.astype(BF16)))
            ocmp_ref[:, LANE * (2 * g + hp):LANE * (2 * g + hp + 1)] = (
                jnp.concatenate(outs, axis=0).T.astype(BF16))

        p_hi = psum.astype(BF16)
        p_lo = (psum - p_hi.astype(F32)).astype(BF16)
        pslc = _nn(ovt, p_hi) + _nn(ovt, p_lo)
        pri = jnp.where(valid, pslc + bonus, -1.0)
        rank = jnp.zeros((32, T), F32)
        for r in range(32):
            row = pri[r:r + 1, :]
            rank = rank + jnp.where(row > pri, 1.0, 0.0) + jnp.where((row == pri) & (jrow > r), 1.0, 0.0)
        selneg = jnp.where(rank < float(SLC_TOPK), 0.0, NEG)
        pad = jnp.concatenate([jnp.zeros((64, T), F32), selneg, jnp.zeros((32, T), F32)], axis=0)
        sel_ref[:, LANE * g:LANE * (g + 1)] = pad.T.astype(BF16)


def _cmpsel(qraw, kca, vct, nb, seq):
    n = qraw.shape[0]
    nq = seq // T
    return pl.pallas_call(
        _cmpsel_kernel,
        out_shape=(jax.ShapeDtypeStruct((n, 512), BF16), jax.ShapeDtypeStruct((n, 256), BF16)),
        grid=(nb, nq),
        in_specs=[pl.BlockSpec((T, 512), lambda b, i: (b * nq + i, 0)),
                  pl.BlockSpec((None, 2, 128, LANE), lambda b, i: (b, 0, 0, 0)),
                  pl.BlockSpec((None, LANE, 128), lambda b, i: (b, 0, 0))],
        out_specs=(pl.BlockSpec((T, 512), lambda b, i: (b * nq + i, 0)),
                   pl.BlockSpec((T, 256), lambda b, i: (b * nq + i, 0))),
        compiler_params=_params(2, VMEM_LIMIT),
        name="cmpsel",
    )(qraw, kca, vct)


def _split_heads(q_ref, extras, qs_scr):
    lane = lax.broadcasted_iota(jnp.int32, (T, LANE), 1)
    for hp in range(NSA_HEADS // 2):
        extra = extras[hp // (NSA_HPG // 2)]
        qp = q_ref[:, LANE * hp:LANE * (hp + 1)].astype(F32)
        qs_scr[2 * hp] = jnp.where(lane < 64, qp, extra).astype(BF16)
        qs_scr[2 * hp + 1] = jnp.where(lane < 64, pltpu.roll(qp, 64, 1), extra).astype(BF16)


SM_ROWS = 32


def _causal_flash(i, k_ref, vt_ref, qs_scr, s_scr, mx_scr, m_scr, l_scr, chains):
    nc = len(chains)
    m_scr[...] = jnp.full(m_scr.shape, NEG, F32)
    l_scr[...] = jnp.zeros(l_scr.shape, F32)
    for _, _, a in chains:
        a[...] = jnp.zeros(a.shape, F32)
    causal = (lax.broadcasted_iota(jnp.int32, (T, T), 0) <= lax.broadcasted_iota(jnp.int32, (T, T), 1))
    ones = jnp.ones((16, T), BF16)

    def scores(j, buf, mask=None):
        for c, (kl, _, _) in enumerate(chains):
            s = _nt(k_ref[j, :, kl:kl + LANE], qs_scr[c])
            if mask is not None:
                s = jnp.where(mask, s, NEG)
            s_scr[buf, c] = s
            mx_scr[buf, c] = jnp.max(s, axis=0, keepdims=True)

    def softmax_pv(j, buf):
        for c, (_, vr, acc) in enumerate(chains):
            dv = acc.shape[0]
            m_prev = m_scr[c]
            m_new = jnp.maximum(m_prev, mx_scr[buf, c])
            alpha = jnp.exp2(m_prev - m_new)
            parts = [jnp.exp2(s_scr[buf, c, r:r + SM_ROWS, :] - m_new).astype(BF16)
                     for r in range(0, T, SM_ROWS)]
            vt = jnp.concatenate([vt_ref[j, vr:vr + dv, :], ones], axis=0)
            pv = _nn(vt, jnp.concatenate(parts, axis=0))
            l_scr[c] = alpha * l_scr[c] + pv[dv:dv + 1]
            m_scr[c] = m_new
            acc[...] = acc[...] * alpha + pv[0:dv]

    scores(i, 0, causal)

    def pair(p, carry):
        scores(2 * p, 1)
        softmax_pv(jnp.where(p == 0, i, 2 * p - 1), 0)
        scores(2 * p + 1, 0)
        softmax_pv(2 * p, 1)
        return carry

    lax.fori_loop(0, i // 2, pair, 0)

    @pl.when(i % 2 == 1)
    def _():
        scores(i - 1, 1)
        softmax_pv(jnp.where(i == 1, i, i - 2), 0)
        softmax_pv(i - 1, 1)

    @pl.when(i % 2 == 0)
    def _():
        softmax_pv(jnp.where(i == 0, i, i - 1), 0)


def _slc_kernel(q_ref, sel_ref, k_ref, vt_ref, o_ref, qs_scr, s_scr, mx_scr, m_scr, l_scr, acc_scr):
    i = pl.program_id(1)
    sel = sel_ref[...].astype(F32)
    _split_heads(q_ref, [sel[:, 0:LANE], sel[:, LANE:2 * LANE]], qs_scr)
    chains = [(LANE * (h // NSA_HPG), 64 * (h // NSA_HPG),
               acc_scr.at[h // 2, 64 * (h % 2):64 * (h % 2 + 1), :]) for h in range(NSA_HEADS)]
    _causal_flash(i, k_ref, vt_ref, qs_scr, s_scr, mx_scr, m_scr, l_scr, chains)
    for hp in range(NSA_HEADS // 2):
        inv = jnp.concatenate([jnp.broadcast_to(1.0 / l_scr[2 * hp], (64, T)),
                               jnp.broadcast_to(1.0 / l_scr[2 * hp + 1], (64, T))], axis=0)
        o_ref[:, LANE * hp:LANE * (hp + 1)] = (acc_scr[hp] * inv).T.astype(BF16)


def _slc(qrot, sel, ksa3, vst, nb, seq):
    n = qrot.shape[0]
    nq = seq // T
    return pl.pallas_call(
        _slc_kernel,
        out_shape=jax.ShapeDtypeStruct((n, 512), BF16),
        grid=(nb, nq),
        in_specs=[pl.BlockSpec((T, 512), lambda b, i: (b * nq + i, 0)),
                  pl.BlockSpec((T, 256), lambda b, i: (b * nq + i, 0)),
                  pl.BlockSpec((nq, T, 256), lambda b, i: (b, 0, 0)),
                  pl.BlockSpec((nq, LANE, T), lambda b, i: (b, 0, 0))],
        out_specs=pl.BlockSpec((T, 512), lambda b, i: (b * nq + i, 0)),
        scratch_shapes=[pltpu.VMEM((NSA_HEADS, T, LANE), BF16),
                        pltpu.VMEM((2, NSA_HEADS, T, T), F32),
                        pltpu.VMEM((2, NSA_HEADS, 1, T), F32),
                        pltpu.VMEM((NSA_HEADS, 1, T), F32),
                        pltpu.VMEM((NSA_HEADS, 1, T), F32),
                        pltpu.VMEM((NSA_HEADS // 2, LANE, T), F32)],
        compiler_params=_params(2, VMEM_LIMIT),
        name="slc",
    )(qrot, sel, ksa3, vst)


def _win_kernel(q_ref, k_ref, vt_ref, o_ref, qs_scr, s_scr, mx_scr, acc_scr):
    i = pl.program_id(1)
    _split_heads(q_ref, [0.0, 0.0], qs_scr)
    krow = lax.broadcasted_iota(jnp.int32, (T, T), 0)
    qcol = lax.broadcasted_iota(jnp.int32, (T, T), 1)
    span = WINDOW // T
    masks = {0: krow <= qcol, span: (krow > qcol) & (i >= span)}
    ones = jnp.ones((16, T), BF16)

    def scores(hp, buf):
        for par in range(2):
            h = 2 * hp + par
            g = h // NSA_HPG
            mx = None
            for d in range(span + 1):
                j = i - d
                s = _nt(k_ref[jnp.maximum(j, 0), :, LANE * g:LANE * (g + 1)], qs_scr[h])
                if d in masks:
                    s = jnp.where(masks[d], s, NEG)
                else:
                    s = s + jnp.where(j >= 0, 0.0, NEG)
                s_scr[buf, par, d] = s
                md = jnp.max(s, axis=0, keepdims=True)
                mx = md if mx is None else jnp.maximum(mx, md)
            mx_scr[buf, par] = mx

    def softmax_pv(hp, buf):
        for par in range(2):
            g = (2 * hp + par) // NSA_HPG
            m = mx_scr[buf, par]
            acc = jnp.zeros((64 + 16, T), F32)
            for d in range(span + 1):
                parts = [jnp.exp2(s_scr[buf, par, d, r:r + SM_ROWS, :] - m).astype(BF16)
                         for r in range(0, T, SM_ROWS)]
                vt = jnp.concatenate([vt_ref[jnp.maximum(i - d, 0), 64 * g:64 * (g + 1), :], ones], axis=0)
                acc = acc + _nn(vt, jnp.concatenate(parts, axis=0))
            acc_scr[hp, 64 * par:64 * (par + 1), :] = acc[0:64] * (1.0 / acc[64:65])

    npairs = NSA_HEADS // 2
    scores(0, 0)
    for hp in range(npairs):
        if hp + 1 < npairs:
            scores(hp + 1, (hp + 1) % 2)
        softmax_pv(hp, hp % 2)
    for hp in range(npairs):
        o_ref[:, LANE * hp:LANE * (hp + 1)] = acc_scr[hp].T.astype(BF16)


def _win(qrot, kwa3, vwt, nb, seq):
    n = qrot.shape[0]
    nq = seq // T
    return pl.pallas_call(
        _win_kernel,
        out_shape=jax.ShapeDtypeStruct((n, 512), BF16),
        grid=(nb, nq),
        in_specs=[pl.BlockSpec((T, 512), lambda b, i: (b * nq + i, 0)),
                  pl.BlockSpec((nq, T, 256), lambda b, i: (b, 0, 0)),
                  pl.BlockSpec((nq, LANE, T), lambda b, i: (b, 0, 0))],
        out_specs=pl.BlockSpec((T, 512), lambda b, i: (b * nq + i, 0)),
        scratch_shapes=[pltpu.VMEM((NSA_HEADS, T, LANE), BF16),
                        pltpu.VMEM((2, 2, WINDOW // T + 1, T, T), F32),
                        pltpu.VMEM((2, 2, 1, T), F32),
                        pltpu.VMEM((NSA_HEADS // 2, LANE, T), F32)],
        compiler_params=_params(2, VMEM_LIMIT),
        name="win",
    )(qrot, kwa3, vwt)


def _diff_kernel(q_ref, k_ref, vt_ref, lam_ref, g_ref, o_ref, qs_scr, s_scr, mx_scr, m_scr, l_scr, acc_scr):
    i = pl.program_id(1)
    lane = lax.broadcasted_iota(jnp.int32, (T, LANE), 1)
    for h in range(DIFF_HEADS):
        q = q_ref[:, LANE * h:LANE * (h + 1)].astype(F32)
        qs_scr[2 * h] = jnp.where(lane < 64, q, 0.0).astype(BF16)
        qs_scr[2 * h + 1] = jnp.where(lane < 64, 0.0, q).astype(BF16)
    chains = [(LANE * (c // 2), LANE * (c // 2), acc_scr.at[c]) for c in range(2 * DIFF_HEADS)]
    _causal_flash(i, k_ref, vt_ref, qs_scr, s_scr, mx_scr, m_scr, l_scr, chains)

    lp = lam_ref[...]
    lam = (jnp.exp(jnp.sum(lp[0:1] * lp[1:2], axis=1, keepdims=True))
           - jnp.exp(jnp.sum(lp[2:3] * lp[3:4], axis=1, keepdims=True)) + LAMBDA_INIT)
    for h in range(DIFF_HEADS):
        o = (acc_scr[2 * h] * (1.0 / l_scr[2 * h])
             - lam * (acc_scr[2 * h + 1] * (1.0 / l_scr[2 * h + 1])))
        o = o * lax.rsqrt(jnp.mean(o * o, axis=0, keepdims=True) + RMS_EPS)
        o_ref[:, LANE * h:LANE * (h + 1)] = ((o.T * g_ref[...]) * (1.0 - LAMBDA_INIT)).astype(BF16)


def _diff(qdf, kdf3, vdft, lam, norm_g, nb, seq):
    n = qdf.shape[0]
    nq = seq // T
    nc = 2 * DIFF_HEADS
    return pl.pallas_call(
        _diff_kernel,
        out_shape=jax.ShapeDtypeStruct((n, 512), BF16),
        grid=(nb, nq),
        in_specs=[pl.BlockSpec((T, 512), lambda b, i: (b * nq + i, 0)),
                  pl.BlockSpec((nq, T, 512), lambda b, i: (b, 0, 0)),
                  pl.BlockSpec((nq, 512, T), lambda b, i: (b, 0, 0)),
                  pl.BlockSpec((4, HEAD_DIM), lambda b, i: (0, 0)),
                  pl.BlockSpec((1, LANE), lambda b, i: (0, 0))],
        out_specs=pl.BlockSpec((T, 512), lambda b, i: (b * nq + i, 0)),
        scratch_shapes=[pltpu.VMEM((nc, T, LANE), BF16),
                        pltpu.VMEM((2, nc, T, T), F32),
                        pltpu.VMEM((2, nc, 1, T), F32),
                        pltpu.VMEM((nc, 1, T), F32),
                        pltpu.VMEM((nc, 1, T), F32),
                        pltpu.VMEM((nc, LANE, T), F32)],
        compiler_params=_params(2, VMEM_LIMIT),
        name="diff",
    )(qdf, kdf3, vdft, lam, norm_g)


def _layer_norm(z, g, b):
    mu = jnp.mean(z, axis=-1, keepdims=True)
    zc = z - mu
    var = jnp.mean(zc * zc, axis=-1, keepdims=True)
    return zc * lax.rsqrt(var + LN_EPS) * g + b


def _merge_kernel(x_ref, ocmp_ref, oslc_ref, owin_ref, gn_ref, yb_ref, gm_ref, e_ref,
                  wa_ref, wb_ref, wo_ref, g_ref, b_ref, h_ref):
    gn = gn_ref[...]
    gn_hi = gn.astype(BF16)
    gn_lo = (gn - gn_hi.astype(F32)).astype(BF16)
    e = e_ref[...]
    gx = _nn(gn_hi, e) + _nn(gn_lo, e)
    ya = (gx[:, 0:512] * ocmp_ref[...].astype(F32)
          + gx[:, 512:1024] * oslc_ref[...].astype(F32)
          + gx[:, 1024:1536] * owin_ref[...].astype(F32))
    ta = _nn(ya.astype(BF16), wa_ref[...])
    tb = _nn(yb_ref[...], wb_ref[...])
    merged = gm_ref[:, 0:D_MODEL].astype(F32) * ta + gm_ref[:, D_MODEL:2 * D_MODEL].astype(F32) * tb
    mix = _nn(merged.astype(BF16), wo_ref[...])
    h_ref[...] = _layer_norm(DEEPNORM_ALPHA * x_ref[...] + mix, g_ref[...], b_ref[...])


def _merge(x2, ocmp, oslc, owin, gn, yb, gm, e, wa, wb, wo, g, b):
    n = x2.shape[0]
    row = lambda w: pl.BlockSpec((T, w), lambda i: (i, 0))
    full = lambda a: pl.BlockSpec(a.shape, lambda i: (0,) * a.ndim)
    return pl.pallas_call(
        _merge_kernel,
        out_shape=jax.ShapeDtypeStruct((n, D_MODEL), F32),
        grid=(n // T,),
        in_specs=[row(D_MODEL), row(512), row(512), row(512), row(LANE), row(512), row(2048),
                  full(e), full(wa), full(wb), full(wo), full(g), full(b)],
        out_specs=row(D_MODEL),
        compiler_params=_params(1, VMEM_LIMIT),
        name="merge",
    )(x2, ocmp, oslc, owin, gn, yb, gm, e, wa, wb, wo, g, b)


def _ffn_kernel(h_ref, wgu_ref, wd_ref, g_ref, b_ref, o_ref):
    h = h_ref[...]
    hb = h.astype(BF16)
    gate = _nn(hb, wgu_ref[:, 0:FFN_HIDDEN])
    up = _nn(hb, wgu_ref[:, FFN_HIDDEN:2 * FFN_HIDDEN])
    act = (gate * jax.nn.sigmoid(gate) * up).astype(BF16)
    y = _nn(act, wd_ref[...])
    o_ref[...] = _layer_norm(DEEPNORM_ALPHA * h + y, g_ref[...], b_ref[...])


def _ffn(h1, wgu, wd, g, b):
    n = h1.shape[0]
    row = pl.BlockSpec((T, D_MODEL), lambda i: (i, 0))
    full = lambda a: pl.BlockSpec(a.shape, lambda i: (0,) * a.ndim)
    return pl.pallas_call(
        _ffn_kernel,
        out_shape=jax.ShapeDtypeStruct((n, D_MODEL), F32),
        grid=(n // T,),
        in_specs=[row, full(wgu), full(wd), full(g), full(b)],
        out_specs=row,
        compiler_params=_params(1, VMEM_LIMIT),
        name="ffn",
    )(h1, wgu, wd, g, b)


def _rope_tables(seq):
    half = ROPE_DIM // 2
    inv_freq = ROPE_THETA ** (-jnp.arange(half, dtype=F32) * 2.0 / ROPE_DIM)
    ang = jnp.arange(seq, dtype=F32)[:, None] * inv_freq[None, :]
    cos, sin = jnp.cos(ang), jnp.sin(ang)
    ones = jnp.ones((seq, HEAD_DIM - ROPE_DIM), F32)
    zeros8 = jnp.zeros((seq, half), F32)
    zeros48 = jnp.zeros((seq, HEAD_DIM - ROPE_DIM), F32)
    c64 = jnp.concatenate([cos, cos, ones], axis=1)
    sa64 = jnp.concatenate([-sin, zeros8, zeros48], axis=1)
    sb64 = jnp.concatenate([zeros8, sin, zeros48], axis=1)
    rep = lambda a: jnp.concatenate([a, a], axis=1)
    pos_blk = jnp.arange(seq, dtype=jnp.int32)[:, None] // SLC_LEN
    lane = jnp.arange(LANE, dtype=jnp.int32)[None, :]
    onehot = ((lane >= 64) & (lane < 96) & (lane - 64 == pos_blk)).astype(F32)
    return rep(c64), rep(sa64), rep(sb64), onehot


def _prep_w_in(w):
    offs = np.cumsum((0,) + IN_WIDTHS)
    q, kc, vc, ks, vs, kw, vw, gn, qd, kd, vd, gm = [w[:, offs[i]:offs[i + 1]] for i in range(12)]
    z64 = jnp.zeros((w.shape[0], 64), w.dtype)
    ksa = jnp.concatenate([ks[:, :64], z64, ks[:, 64:], z64], axis=1)
    kwa = jnp.concatenate([kw[:, :64], z64, kw[:, 64:], z64], axis=1)
    gnp = jnp.pad(gn, ((0, 0), (0, LANE - gn.shape[1])))
    return jnp.concatenate([q, kc, vc, ksa, vs, kwa, vw, qd, kd, vd, gm, gnp], axis=1).astype(BF16)


def _prep_compress(cmp_pe, cmp_w1, cmp_b1, cmp_w2):
    half = CMP_LEN // 2
    pet, peb, wt, wb, b1 = [], [], [], [], []
    for kv in range(2):
        pe = cmp_pe[kv]
        tile2 = lambda a: jnp.concatenate([a, a], axis=1).reshape(1, half * 2 * HEAD_DIM)
        pet.append(tile2(pe[:half]))
        peb.append(tile2(pe[half:]))
        w1 = cmp_w1[kv].reshape(CMP_LEN, HEAD_DIM, CMP_HIDDEN)
        z = jnp.zeros((half, HEAD_DIM, CMP_HIDDEN), w1.dtype)

        def spread(wh):
            g0 = jnp.concatenate([wh, z], axis=2)
            g1 = jnp.concatenate([z, wh], axis=2)
            return jnp.stack([g0, g1], axis=1).reshape(half * 2 * HEAD_DIM, 2 * CMP_HIDDEN)

        wt.append(spread(w1[:half]))
        wb.append(spread(w1[half:]))
        b1.append(jnp.concatenate([cmp_b1[kv], cmp_b1[kv]])[None, :])
    w2k, w2v = cmp_w2[0], cmp_w2[1]
    zk = jnp.zeros_like(w2k)
    w2k_g = jnp.stack([
        jnp.concatenate([jnp.concatenate([w2k, zk], axis=1), jnp.zeros((CMP_HIDDEN, LANE), w2k.dtype)], axis=0),
        jnp.concatenate([jnp.zeros((CMP_HIDDEN, LANE), w2k.dtype), jnp.concatenate([w2k, zk], axis=1)], axis=0),
    ])
    zv = jnp.zeros_like(w2v)
    w2v_bd = jnp.concatenate([jnp.concatenate([w2v, zv], axis=1),
                              jnp.concatenate([zv, w2v], axis=1)], axis=0)
    st = lambda xs: jnp.stack(xs)
    return (st(pet), st(peb), st(wt).astype(BF16), st(wb).astype(BF16), st(b1),
            w2k_g.astype(BF16), w2v_bd.astype(BF16))


def _gate_expand():
    e = np.zeros((LANE, 3 * NSA_Q), np.float32)
    for h in range(NSA_HEADS):
        for k in range(3):
            e[h * 3 + k, k * NSA_Q + h * HEAD_DIM:k * NSA_Q + (h + 1) * HEAD_DIM] = 1.0
    return jnp.asarray(e, BF16)


def kernel(x, w_in, cmp_pe, cmp_w1, cmp_b1, cmp_w2, diff_lambda, diff_norm_g, w_branch_a, w_branch_b,
           w_o, ln1_g, ln1_b, w_gate_up, w_down, ln2_g, ln2_b):
    nb, seq, d = x.shape
    assert d == D_MODEL and seq % T == 0 and seq // SLC_LEN == 32 and seq // CMP_STRIDE == 128
    n = nb * seq
    x2 = x.reshape(n, d)
    cos_t, sa_t, sb_t, oh_t = _rope_tables(seq)
    (qraw, qrot, kcs, vcs, ksa, vst, kwa, vwt, qdf, kdf, vdft, gm, gn) = _inproj(
        x2, _prep_w_in(w_in[0]), cos_t, sa_t, sb_t, oh_t, seq)

    pet, peb, wt, wb, b1, w2k, w2v = _prep_compress(cmp_pe[0], cmp_w1[0], cmp_b1[0], cmp_w2[0])
    nchunk = seq // CMP_STRIDE
    kca, vct = _compress(kcs.reshape(nb, nchunk, CMP_STRIDE * LANE), vcs.reshape(nb, nchunk, CMP_STRIDE * LANE),
                         pet, peb, wt, wb, b1, w2k, w2v)

    ocmp, sel = _cmpsel(qraw, kca, vct, nb, seq)
    nt = n // T
    oslc = _slc(qrot, sel, ksa.reshape(nt, T, 256), vst, nb, seq)
    owin = _win(qrot, kwa.reshape(nt, T, 256), vwt, nb, seq)
    yb = _diff(qdf, kdf.reshape(nt, T, 512), vdft, diff_lambda[0], diff_norm_g[0][None, :], nb, seq)

    h1 = _merge(x2, ocmp, oslc, owin, gn, yb, gm, _gate_expand(),
                w_branch_a[0].astype(BF16), w_branch_b[0].astype(BF16), w_o[0].astype(BF16),
                ln1_g[0][None, :], ln1_b[0][None, :])
    out = _ffn(h1, w_gate_up[0].astype(BF16), w_down[0].astype(BF16), ln2_g[0][None, :], ln2_b[0][None, :])
    return out.reshape(nb, seq, d)
```

```python
import functools
import math

import numpy as np
import jax
import jax.numpy as jnp
from jax import lax
from jax.experimental import pallas as pl
from jax.experimental.pallas import tpu as pltpu

D_MODEL = 1024
HEAD_DIM = 64
ROPE_DIM = HEAD_DIM // 4
ROPE_THETA = 500000.0
NSA_HEADS = 8
NSA_GROUPS = 2
NSA_HPG = NSA_HEADS // NSA_GROUPS
CMP_LEN = 32
CMP_STRIDE = 16
CMP_HIDDEN = 256
SLC_LEN = 64
SLC_TOPK = 8
WINDOW = 512
FORCE_BONUS = 1.0e4
DIFF_HEADS = 4
DIFF_V_DIM = 2 * HEAD_DIM
FFN_HIDDEN = ((8 * D_MODEL // 3 + 255) // 256) * 256
DEPTH = 1
DEEPNORM_ALPHA = (2 * DEPTH) ** 0.25
NEG = -1.0e30
LN_EPS = 1e-5
RMS_EPS = 1e-5
LAMBDA_INIT = 0.8 - 0.6 * math.exp(-0.3 * 0)
QK_SCALE = HEAD_DIM ** -0.5
QK_SCALE_LOG2 = QK_SCALE * math.log2(math.e)

NSA_Q = NSA_HEADS * HEAD_DIM
NSA_KV = NSA_GROUPS * HEAD_DIM
DIFF_QK = DIFF_HEADS * 2 * HEAD_DIM
DIFF_V = DIFF_HEADS * DIFF_V_DIM
IN_WIDTHS = (NSA_Q, NSA_KV, NSA_KV, NSA_KV, NSA_KV, NSA_KV, NSA_KV, 3 * NSA_HEADS,
             DIFF_QK, DIFF_QK, DIFF_V, 2 * D_MODEL)

LANE = 128
T = 256
TM = 512
VMEM_LIMIT = 56 * 1024 * 1024
GATE_ROWS = 32

BF16 = jnp.bfloat16
F32 = jnp.float32

_C_Q, _C_KC, _C_VC, _C_KS, _C_VS, _C_KW, _C_VW = 0, 512, 640, 768, 1024, 1152, 1408
_C_QD, _C_KD, _C_VD, _C_GM, _C_GN, _C_END = 1536, 2048, 2560, 3072, 5120, 5248


def _nt(a, b):
    return lax.dot_general(a, b, (((1,), (1,)), ((), ())), preferred_element_type=F32)


def _nn(a, b):
    return jnp.dot(a, b, preferred_element_type=F32)


def _params(n_axes, vmem=None):
    return pltpu.CompilerParams(dimension_semantics=("arbitrary",) * n_axes,
                                vmem_limit_bytes=vmem)


def _inproj_kernel(x_ref, w_ref, cos_ref, sa_ref, sb_ref, oh_ref,
                   qraw_ref, qrot_ref, kc_ref, vc_ref, ksa_ref, vst_ref, kwa_ref, vwt_ref,
                   qdf_ref, kdf_ref, vdft_ref, gm_ref, gn_ref):
    xb = x_ref[...].astype(BF16)
    cos = cos_ref[...]
    sa = sa_ref[...]
    sb = sb_ref[...]

    def mm(c0, n):
        return _nn(xb, w_ref[:, c0:c0 + n])

    def rope(t):
        return t * cos + pltpu.roll(t, LANE - 8, 1) * sa + pltpu.roll(t, 8, 1) * sb

    t = mm(_C_Q, 512)
    for j in range(4):
        tj = t[:, LANE * j:LANE * (j + 1)]
        qraw_ref[:, LANE * j:LANE * (j + 1)] = (tj * QK_SCALE_LOG2).astype(BF16)
        qrot_ref[:, LANE * j:LANE * (j + 1)] = (rope(tj) * QK_SCALE_LOG2).astype(BF16)

    t = mm(_C_KC, 256)
    kc_ref[...] = t[:, :LANE]
    vc_ref[...] = t[:, LANE:]

    t = mm(_C_KS, 256)
    oh = oh_ref[...]
    for g in range(2):
        ksa_ref[:, LANE * g:LANE * (g + 1)] = (rope(t[:, LANE * g:LANE * (g + 1)]) + oh).astype(BF16)

    t = mm(_C_VS, 128)
    vst_ref[...] = t.T.astype(BF16)

    t = mm(_C_KW, 256)
    for g in range(2):
        kwa_ref[:, LANE * g:LANE * (g + 1)] = rope(t[:, LANE * g:LANE * (g + 1)]).astype(BF16)

    t = mm(_C_VW, 128)
    vwt_ref[...] = t.T.astype(BF16)

    t = mm(_C_QD, 512)
    for j in range(4):
        qdf_ref[:, LANE * j:LANE * (j + 1)] = (rope(t[:, LANE * j:LANE * (j + 1)]) * QK_SCALE_LOG2).astype(BF16)
    t = mm(_C_KD, 512)
    for j in range(4):
        kdf_ref[:, LANE * j:LANE * (j + 1)] = rope(t[:, LANE * j:LANE * (j + 1)]).astype(BF16)
    t = mm(_C_VD, 512)
    for j in range(4):
        vdft_ref[LANE * j:LANE * (j + 1), :] = t[:, LANE * j:LANE * (j + 1)].T.astype(BF16)

    for j in range(4):
        t = mm(_C_GM + 512 * j, 512)
        gm_ref[:, 512 * j:512 * (j + 1)] = jax.nn.sigmoid(t).astype(BF16)

    gn_ref[...] = jax.nn.sigmoid(mm(_C_GN, 128)).T[0:GATE_ROWS]


def _inproj(x2, w_perm, cos_t, sa_t, sb_t, oh_t, seq):
    n = x2.shape[0]
    nt = n // T
    spt = seq // T
    row = lambda w: pl.BlockSpec((T, w), lambda i: (i, 0))
    tab = pl.BlockSpec((T, LANE), lambda i: (i % spt, 0))
    tile_t = lambda r: pl.BlockSpec((None, r, T), lambda i: (i, 0, 0))
    out_shape = (
        jax.ShapeDtypeStruct((n, 512), BF16),
        jax.ShapeDtypeStruct((n, 512), BF16),
        jax.ShapeDtypeStruct((n, LANE), F32),
        jax.ShapeDtypeStruct((n, LANE), F32),
        jax.ShapeDtypeStruct((n, 256), BF16),
        jax.ShapeDtypeStruct((nt, LANE, T), BF16),
        jax.ShapeDtypeStruct((n, 256), BF16),
        jax.ShapeDtypeStruct((nt, LANE, T), BF16),
        jax.ShapeDtypeStruct((n, 512), BF16),
        jax.ShapeDtypeStruct((n, 512), BF16),
        jax.ShapeDtypeStruct((nt, 512, T), BF16),
        jax.ShapeDtypeStruct((n, 2048), BF16),
        jax.ShapeDtypeStruct((nt, GATE_ROWS, T), F32),
    )
    out_specs = (row(512), row(512), row(LANE), row(LANE), row(256), tile_t(LANE), row(256),
                 tile_t(LANE), row(512), row(512), tile_t(512), row(2048), tile_t(GATE_ROWS))
    return pl.pallas_call(
        _inproj_kernel,
        out_shape=out_shape,
        grid=(nt,),
        in_specs=[row(D_MODEL),
                  pl.BlockSpec((D_MODEL, _C_END), lambda i: (0, 0)),
                  tab, tab, tab, tab],
        out_specs=out_specs,
        compiler_params=_params(1, VMEM_LIMIT),
        name="inproj",
    )(x2, w_perm, cos_t, sa_t, sb_t, oh_t)


def _gelu_tanh(x):
    c = math.sqrt(2.0 / math.pi)
    return x * (0.5 * (1.0 + jnp.tanh(c * (x + 0.044715 * (x * x * x)))))


def _compress_kernel(kf_ref, vf_ref, pet_ref, peb_ref, wt_ref, wb_ref, b1_ref, w2k_ref, w2v_ref,
                     kca_ref, vct_ref):
    nchunk = kf_ref.shape[0] // CMP_STRIDE

    def hidden(x_ref, kv):
        a = jnp.zeros((nchunk, 2 * CMP_HIDDEN), F32)
        b = jnp.zeros((nchunk, 2 * CMP_HIDDEN), F32)
        for p in range(0, CMP_STRIDE, 2):
            x = jnp.concatenate([x_ref[pl.ds(p, nchunk, stride=CMP_STRIDE), :],
                                 x_ref[pl.ds(p + 1, nchunk, stride=CMP_STRIDE), :]], axis=1)
            c0, c1 = LANE * p, LANE * (p + 2)
            a = a + _nn((x + pet_ref[kv, :, c0:c1]).astype(BF16), wt_ref[kv, c0:c1, :])
            b = b + _nn((x + peb_ref[kv, :, c0:c1]).astype(BF16), wb_ref[kv, c0:c1, :])
        h = a + pltpu.roll(b, nchunk - 1, 0) + b1_ref[kv]
        return _gelu_tanh(h).astype(BF16)

    hk = hidden(kf_ref, 0)
    for g in range(2):
        kca_ref[g] = _nn(hk, w2k_ref[g]).astype(BF16)
    hv = hidden(vf_ref, 1)
    vct_ref[...] = _nn(hv, w2v_ref[...]).T.astype(BF16)


def _compress(kf, vf, pet, peb, wt, wb, b1, w2k, w2v, nb, seq):
    full = lambda a: pl.BlockSpec(a.shape, lambda b: (0,) * a.ndim)
    return pl.pallas_call(
        _compress_kernel,
        out_shape=(jax.ShapeDtypeStruct((nb, 2, 128, LANE), BF16),
                   jax.ShapeDtypeStruct((nb, LANE, 128), BF16)),
        grid=(nb,),
        in_specs=[pl.BlockSpec((seq, LANE), lambda b: (b, 0)),
                  pl.BlockSpec((seq, LANE), lambda b: (b, 0)),
                  full(pet), full(peb), full(wt), full(wb), full(b1), full(w2k), full(w2v)],
        out_specs=(pl.BlockSpec((None, 2, 128, LANE), lambda b: (b, 0, 0, 0)),
                   pl.BlockSpec((None, LANE, 128), lambda b: (b, 0, 0))),
        compiler_params=_params(1, VMEM_LIMIT),
        name="compress",
    )(kf, vf, pet, peb, wt, wb, b1, w2k, w2v)


def _cmpsel_kernel(q_ref, kca_ref, vct_ref, gnt_ref, ocmp_ref, sel_ref):
    i = pl.program_id(1)
    t0 = i * T
    lane = lax.broadcasted_iota(jnp.int32, (T, LANE), 1)
    crow = lax.broadcasted_iota(jnp.int32, (128, T), 0)
    tcol = lax.broadcasted_iota(jnp.int32, (128, T), 1) + t0
    cmask = (crow * CMP_STRIDE + (CMP_LEN - 1)) <= tcol
    live = jnp.where((lax.broadcasted_iota(jnp.int32, (1, T), 1) + t0) >= (CMP_LEN - 1), 1.0, 0.0)

    jj = lax.broadcasted_iota(jnp.int32, (32, 128), 0) * SLC_LEN
    cc = lax.broadcasted_iota(jnp.int32, (32, 128), 1) * CMP_STRIDE
    ov = jnp.maximum(jnp.minimum(cc + CMP_LEN, jj + SLC_LEN) - jnp.maximum(cc, jj), 0)
    ovt = (ov.astype(F32) * (1.0 / CMP_LEN)).astype(BF16)

    jrow = lax.broadcasted_iota(jnp.int32, (32, T), 0)
    jrow8 = lax.broadcasted_iota(jnp.int32, (8, T), 0)
    tblk = (lax.broadcasted_iota(jnp.int32, (32, T), 1) + t0) // SLC_LEN
    valid = jrow <= tblk
    forced = (jrow == 0) | (jrow == tblk) | (jrow == tblk - 1)
    bonus = jnp.where(forced, FORCE_BONUS, 0.0)

    for g in range(2):
        kc = kca_ref[g]
        vt = vct_ref[64 * g:64 * (g + 1), :]
        psum = jnp.zeros((128, T), F32)
        for hp in range(2):
            qp = q_ref[:, LANE * (2 * g + hp):LANE * (2 * g + hp + 1)].astype(F32)
            qs = (jnp.where(lane < 64, qp, 0.0), jnp.where(lane < 64, pltpu.roll(qp, 64, 1), 0.0))
            outs = []
            for par in range(2):
                h = NSA_HPG * g + 2 * hp + par
                s = _nt(kc, qs[par].astype(BF16))
                s = jnp.where(cmask, s, NEG)
                m = jnp.max(s, axis=0, keepdims=True)
                e = jnp.exp2(s - m)
                p = e * (live / jnp.sum(e, axis=0, keepdims=True))
                psum = psum + p
                outs.append(_nn(vt, p.astype(BF16)) * gnt_ref[3 * h:3 * h + 1, :])
            ocmp_ref[:, LANE * (2 * g + hp):LANE * (2 * g + hp + 1)] = (
                jnp.concatenate(outs, axis=0).T.astype(BF16))

        p_hi = psum.astype(BF16)
        p_lo = (psum - p_hi.astype(F32)).astype(BF16)
        pslc = _nn(ovt, p_hi) + _nn(ovt, p_lo)
        pri = jnp.where(valid, pslc + bonus, -1.0)
        rank = [jnp.zeros((8, T), F32) for _ in range(4)]
        for r in range(32):
            row = pri[r:r + 1, :]
            for a in range(4):
                pa = pri[8 * a:8 * (a + 1), :]
                if 8 * a > r:
                    ahead = jnp.where(row >= pa, 1.0, 0.0)
                elif 8 * a + 7 < r:
                    ahead = jnp.where(row > pa, 1.0, 0.0)
                else:
                    ahead = jnp.where(jrow8 + 8 * a > r, jnp.where(row >= pa, 1.0, 0.0),
                                      jnp.where(row > pa, 1.0, 0.0))
                rank[a] = rank[a] + ahead
        rank = jnp.concatenate(rank, axis=0)
        selneg = jnp.where(rank < float(SLC_TOPK), 0.0, NEG)
        pad = jnp.concatenate([jnp.zeros((64, T), F32), selneg, jnp.zeros((32, T), F32)], axis=0)
        sel_ref[:, LANE * g:LANE * (g + 1)] = pad.T.astype(BF16)


def _cmpsel(qraw, kca, vct, gnt, nb, seq):
    n = qraw.shape[0]
    nq = seq // T
    return pl.pallas_call(
        _cmpsel_kernel,
        out_shape=(jax.ShapeDtypeStruct((n, 512), BF16), jax.ShapeDtypeStruct((n, 256), BF16)),
        grid=(nb, nq),
        in_specs=[pl.BlockSpec((T, 512), lambda b, i: (b * nq + i, 0)),
                  pl.BlockSpec((None, 2, 128, LANE), lambda b, i: (b, 0, 0, 0)),
                  pl.BlockSpec((None, LANE, 128), lambda b, i: (b, 0, 0)),
                  pl.BlockSpec((None, GATE_ROWS, T), lambda b, i: (b * nq + i, 0, 0))],
        out_specs=(pl.BlockSpec((T, 512), lambda b, i: (b * nq + i, 0)),
                   pl.BlockSpec((T, 256), lambda b, i: (b * nq + i, 0))),
        compiler_params=_params(2, VMEM_LIMIT),
        name="cmpsel",
    )(qraw, kca, vct, gnt)


def _split_heads(q_ref, extras, qs_scr):
    lane = lax.broadcasted_iota(jnp.int32, (T, LANE), 1)
    for hp in range(NSA_HEADS // 2):
        extra = extras[hp // (NSA_HPG // 2)]
        qp = q_ref[:, LANE * hp:LANE * (hp + 1)].astype(F32)
        qs_scr[2 * hp] = jnp.where(lane < 64, qp, extra).astype(BF16)
        qs_scr[2 * hp + 1] = jnp.where(lane < 64, pltpu.roll(qp, 64, 1), extra).astype(BF16)


SM_ROWS = 32


def _causal_flash(i, k_ref, vt_ref, qs_scr, s_scr, mx_scr, m_scr, l_scr, chains):
    nc = len(chains)
    m_scr[...] = jnp.full(m_scr.shape, NEG, F32)
    l_scr[...] = jnp.zeros(l_scr.shape, F32)
    for _, _, a in chains:
        a[...] = jnp.zeros(a.shape, F32)
    causal = (lax.broadcasted_iota(jnp.int32, (T, T), 0) <= lax.broadcasted_iota(jnp.int32, (T, T), 1))
    ones = jnp.ones((16, T), BF16)

    def scores(j, buf, mask=None):
        for c, (kl, _, _) in enumerate(chains):
            s = _nt(k_ref[j, :, kl:kl + LANE], qs_scr[c])
            if mask is not None:
                s = jnp.where(mask, s, NEG)
            s_scr[buf, c] = s
            mx_scr[buf, c] = jnp.max(s, axis=0, keepdims=True)

    def softmax_pv(j, buf):
        for c, (_, vr, acc) in enumerate(chains):
            dv = acc.shape[0]
            m_prev = m_scr[c]
            m_new = jnp.maximum(m_prev, mx_scr[buf, c])
            alpha = jnp.exp2(m_prev - m_new)
            parts = [jnp.exp2(s_scr[buf, c, r:r + SM_ROWS, :] - m_new).astype(BF16)
                     for r in range(0, T, SM_ROWS)]
            vt = jnp.concatenate([vt_ref[j, vr:vr + dv, :], ones], axis=0)
            pv = _nn(vt, jnp.concatenate(parts, axis=0))
            l_scr[c] = alpha * l_scr[c] + pv[dv:dv + 1]
            m_scr[c] = m_new
            acc[...] = acc[...] * alpha + pv[0:dv]

    scores(i, 0, causal)

    def pair(p, carry):
        scores(2 * p, 1)
        softmax_pv(jnp.where(p == 0, i, 2 * p - 1), 0)
        scores(2 * p + 1, 0)
        softmax_pv(2 * p, 1)
        return carry

    lax.fori_loop(0, i // 2, pair, 0)

    @pl.when(i % 2 == 1)
    def _():
        scores(i - 1, 1)
        softmax_pv(jnp.where(i == 1, i, i - 2), 0)
        softmax_pv(i - 1, 1)

    @pl.when(i % 2 == 0)
    def _():
        softmax_pv(jnp.where(i == 0, i, i - 1), 0)


def _slc_kernel(q_ref, sel_ref, k_ref, vt_ref, gnt_ref, o_ref, qs_scr, s_scr, mx_scr, m_scr, l_scr, acc_scr):
    i = pl.program_id(1)
    sel = sel_ref[...].astype(F32)
    _split_heads(q_ref, [sel[:, 0:LANE], sel[:, LANE:2 * LANE]], qs_scr)
    chains = [(LANE * (h // NSA_HPG), 64 * (h // NSA_HPG),
               acc_scr.at[h // 2, 64 * (h % 2):64 * (h % 2 + 1), :]) for h in range(NSA_HEADS)]
    _causal_flash(i, k_ref, vt_ref, qs_scr, s_scr, mx_scr, m_scr, l_scr, chains)
    for hp in range(NSA_HEADS // 2):
        scale = [gnt_ref[3 * h + 1:3 * h + 2, :] / l_scr[h] for h in (2 * hp, 2 * hp + 1)]
        inv = jnp.concatenate([jnp.broadcast_to(scale[0], (64, T)), jnp.broadcast_to(scale[1], (64, T))], axis=0)
        o_ref[:, LANE * hp:LANE * (hp + 1)] = (acc_scr[hp] * inv).T.astype(BF16)


def _slc(qrot, sel, ksa3, vst, gnt, nb, seq):
    n = qrot.shape[0]
    nq = seq // T
    return pl.pallas_call(
        _slc_kernel,
        out_shape=jax.ShapeDtypeStruct((n, 512), BF16),
        grid=(nb, nq),
        in_specs=[pl.BlockSpec((T, 512), lambda b, i: (b * nq + i, 0)),
                  pl.BlockSpec((T, 256), lambda b, i: (b * nq + i, 0)),
                  pl.BlockSpec((nq, T, 256), lambda b, i: (b, 0, 0)),
                  pl.BlockSpec((nq, LANE, T), lambda b, i: (b, 0, 0)),
                  pl.BlockSpec((None, GATE_ROWS, T), lambda b, i: (b * nq + i, 0, 0))],
        out_specs=pl.BlockSpec((T, 512), lambda b, i: (b * nq + i, 0)),
        scratch_shapes=[pltpu.VMEM((NSA_HEADS, T, LANE), BF16),
                        pltpu.VMEM((2, NSA_HEADS, T, T), F32),
                        pltpu.VMEM((2, NSA_HEADS, 1, T), F32),
                        pltpu.VMEM((NSA_HEADS, 1, T), F32),
                        pltpu.VMEM((NSA_HEADS, 1, T), F32),
                        pltpu.VMEM((NSA_HEADS // 2, LANE, T), F32)],
        compiler_params=_params(2, VMEM_LIMIT),
        name="slc",
    )(qrot, sel, ksa3, vst, gnt)


def _win_kernel(q_ref, k_ref, vt_ref, gnt_ref, o_ref, qs_scr, s_scr, mx_scr, acc_scr):
    i = pl.program_id(1)
    _split_heads(q_ref, [0.0, 0.0], qs_scr)
    krow = lax.broadcasted_iota(jnp.int32, (T, T), 0)
    qcol = lax.broadcasted_iota(jnp.int32, (T, T), 1)
    span = WINDOW // T
    masks = {0: krow <= qcol, span: (krow > qcol) & (i >= span)}
    ones = jnp.ones((16, T), BF16)

    def scores(hp, buf):
        for par in range(2):
            h = 2 * hp + par
            g = h // NSA_HPG
            mx = None
            for d in range(span + 1):
                j = i - d
                s = _nt(k_ref[jnp.maximum(j, 0), :, LANE * g:LANE * (g + 1)], qs_scr[h])
                if d in masks:
                    s = jnp.where(masks[d], s, NEG)
                else:
                    s = s + jnp.where(j >= 0, 0.0, NEG)
                s_scr[buf, par, d] = s
                md = jnp.max(s, axis=0, keepdims=True)
                mx = md if mx is None else jnp.maximum(mx, md)
            mx_scr[buf, par] = mx

    def softmax_pv(hp, buf):
        for par in range(2):
            g = (2 * hp + par) // NSA_HPG
            m = mx_scr[buf, par]
            acc = jnp.zeros((64 + 16, T), F32)
            for d in range(span + 1):
                parts = [jnp.exp2(s_scr[buf, par, d, r:r + SM_ROWS, :] - m).astype(BF16)
                         for r in range(0, T, SM_ROWS)]
                vt = jnp.concatenate([vt_ref[jnp.maximum(i - d, 0), 64 * g:64 * (g + 1), :], ones], axis=0)
                acc = acc + _nn(vt, jnp.concatenate(parts, axis=0))
            gate = gnt_ref[3 * (2 * hp + par) + 2:3 * (2 * hp + par) + 3, :]
            acc_scr[hp, 64 * par:64 * (par + 1), :] = acc[0:64] * (gate / acc[64:65])

    npairs = NSA_HEADS // 2
    scores(0, 0)
    for hp in range(npairs):
        if hp + 1 < npairs:
            scores(hp + 1, (hp + 1) % 2)
        softmax_pv(hp, hp % 2)
    for hp in range(npairs):
        o_ref[:, LANE * hp:LANE * (hp + 1)] = acc_scr[hp].T.astype(BF16)


def _win(qrot, kwa3, vwt, gnt, nb, seq):
    n = qrot.shape[0]
    nq = seq // T
    return pl.pallas_call(
        _win_kernel,
        out_shape=jax.ShapeDtypeStruct((n, 512), BF16),
        grid=(nb, nq),
        in_specs=[pl.BlockSpec((T, 512), lambda b, i: (b * nq + i, 0)),
                  pl.BlockSpec((nq, T, 256), lambda b, i: (b, 0, 0)),
                  pl.BlockSpec((nq, LANE, T), lambda b, i: (b, 0, 0)),
                  pl.BlockSpec((None, GATE_ROWS, T), lambda b, i: (b * nq + i, 0, 0))],
        out_specs=pl.BlockSpec((T, 512), lambda b, i: (b * nq + i, 0)),
        scratch_shapes=[pltpu.VMEM((NSA_HEADS, T, LANE), BF16),
                        pltpu.VMEM((2, 2, WINDOW // T + 1, T, T), F32),
                        pltpu.VMEM((2, 2, 1, T), F32),
                        pltpu.VMEM((NSA_HEADS // 2, LANE, T), F32)],
        compiler_params=_params(2, VMEM_LIMIT),
        name="win",
    )(qrot, kwa3, vwt, gnt)


def _diff_kernel(q_ref, k_ref, vt_ref, lam_ref, g_ref, o_ref, qs_scr, s_scr, mx_scr, m_scr, l_scr, acc_scr):
    i = pl.program_id(1)
    lane = lax.broadcasted_iota(jnp.int32, (T, LANE), 1)
    for h in range(DIFF_HEADS):
        q = q_ref[:, LANE * h:LANE * (h + 1)].astype(F32)
        qs_scr[2 * h] = jnp.where(lane < 64, q, 0.0).astype(BF16)
        qs_scr[2 * h + 1] = jnp.where(lane < 64, 0.0, q).astype(BF16)
    chains = [(LANE * (c // 2), LANE * (c // 2), acc_scr.at[c]) for c in range(2 * DIFF_HEADS)]
    _causal_flash(i, k_ref, vt_ref, qs_scr, s_scr, mx_scr, m_scr, l_scr, chains)

    lp = lam_ref[...]
    lam = (jnp.exp(jnp.sum(lp[0:1] * lp[1:2], axis=1, keepdims=True))
           - jnp.exp(jnp.sum(lp[2:3] * lp[3:4], axis=1, keepdims=True)) + LAMBDA_INIT)
    for h in range(DIFF_HEADS):
        o = (acc_scr[2 * h] * (1.0 / l_scr[2 * h])
             - lam * (acc_scr[2 * h + 1] * (1.0 / l_scr[2 * h + 1])))
        o = o * lax.rsqrt(jnp.mean(o * o, axis=0, keepdims=True) + RMS_EPS)
        o_ref[:, LANE * h:LANE * (h + 1)] = ((o.T * g_ref[...]) * (1.0 - LAMBDA_INIT)).astype(BF16)


def _diff(qdf, kdf3, vdft, lam, norm_g, nb, seq):
    n = qdf.shape[0]
    nq = seq // T
    nc = 2 * DIFF_HEADS
    return pl.pallas_call(
        _diff_kernel,
        out_shape=jax.ShapeDtypeStruct((n, 512), BF16),
        grid=(nb, nq),
        in_specs=[pl.BlockSpec((T, 512), lambda b, i: (b * nq + i, 0)),
                  pl.BlockSpec((nq, T, 512), lambda b, i: (b, 0, 0)),
                  pl.BlockSpec((nq, 512, T), lambda b, i: (b, 0, 0)),
                  pl.BlockSpec((4, HEAD_DIM), lambda b, i: (0, 0)),
                  pl.BlockSpec((1, LANE), lambda b, i: (0, 0))],
        out_specs=pl.BlockSpec((T, 512), lambda b, i: (b * nq + i, 0)),
        scratch_shapes=[pltpu.VMEM((nc, T, LANE), BF16),
                        pltpu.VMEM((2, nc, T, T), F32),
                        pltpu.VMEM((2, nc, 1, T), F32),
                        pltpu.VMEM((nc, 1, T), F32),
                        pltpu.VMEM((nc, 1, T), F32),
                        pltpu.VMEM((nc, LANE, T), F32)],
        compiler_params=_params(2, VMEM_LIMIT),
        name="diff",
    )(qdf, kdf3, vdft, lam, norm_g)


def _layer_norm(z, g, b):
    mu = jnp.mean(z, axis=-1, keepdims=True)
    zc = z - mu
    var = jnp.mean(zc * zc, axis=-1, keepdims=True)
    return zc * lax.rsqrt(var + LN_EPS) * g + b


def _merge_kernel(x_ref, ocmp_ref, oslc_ref, owin_ref, yb_ref, gm_ref,
                  wa_ref, wb_ref, wo_ref, g_ref, b_ref, h_ref):
    ya = ocmp_ref[...].astype(F32) + oslc_ref[...].astype(F32) + owin_ref[...].astype(F32)
    ta = _nn(ya.astype(BF16), wa_ref[...])
    tb = _nn(yb_ref[...], wb_ref[...])
    merged = gm_ref[:, 0:D_MODEL].astype(F32) * ta + gm_ref[:, D_MODEL:2 * D_MODEL].astype(F32) * tb
    mix = _nn(merged.astype(BF16), wo_ref[...])
    h_ref[...] = _layer_norm(DEEPNORM_ALPHA * x_ref[...] + mix, g_ref[...], b_ref[...])


def _merge(x2, ocmp, oslc, owin, yb, gm, wa, wb, wo, g, b):
    n = x2.shape[0]
    row = lambda w: pl.BlockSpec((TM, w), lambda i: (i, 0))
    full = lambda a: pl.BlockSpec(a.shape, lambda i: (0,) * a.ndim)
    return pl.pallas_call(
        _merge_kernel,
        out_shape=jax.ShapeDtypeStruct((n, D_MODEL), F32),
        grid=(n // TM,),
        in_specs=[row(D_MODEL), row(512), row(512), row(512), row(512), row(2048),
                  full(wa), full(wb), full(wo), full(g), full(b)],
        out_specs=row(D_MODEL),
        compiler_params=_params(1, VMEM_LIMIT),
        name="merge",
    )(x2, ocmp, oslc, owin, yb, gm, wa, wb, wo, g, b)


def _ffn_kernel(h_ref, wgu_ref, wd_ref, g_ref, b_ref, o_ref):
    h = h_ref[...]
    hb = h.astype(BF16)
    gate = _nn(hb, wgu_ref[:, 0:FFN_HIDDEN])
    up = _nn(hb, wgu_ref[:, FFN_HIDDEN:2 * FFN_HIDDEN])
    act = (gate * jax.nn.sigmoid(gate) * up).astype(BF16)
    y = _nn(act, wd_ref[...])
    o_ref[...] = _layer_norm(DEEPNORM_ALPHA * h + y, g_ref[...], b_ref[...])


def _ffn(h1, wgu, wd, g, b):
    n = h1.shape[0]
    row = pl.BlockSpec((T, D_MODEL), lambda i: (i, 0))
    full = lambda a: pl.BlockSpec(a.shape, lambda i: (0,) * a.ndim)
    return pl.pallas_call(
        _ffn_kernel,
        out_shape=jax.ShapeDtypeStruct((n, D_MODEL), F32),
        grid=(n // T,),
        in_specs=[row, full(wgu), full(wd), full(g), full(b)],
        out_specs=row,
        compiler_params=_params(1, VMEM_LIMIT),
        name="ffn",
    )(h1, wgu, wd, g, b)


def _rope_tables(seq):
    half = ROPE_DIM // 2
    inv_freq = ROPE_THETA ** (-jnp.arange(half, dtype=F32) * 2.0 / ROPE_DIM)
    ang = jnp.arange(seq, dtype=F32)[:, None] * inv_freq[None, :]
    cos, sin = jnp.cos(ang), jnp.sin(ang)
    ones = jnp.ones((seq, HEAD_DIM - ROPE_DIM), F32)
    zeros8 = jnp.zeros((seq, half), F32)
    zeros48 = jnp.zeros((seq, HEAD_DIM - ROPE_DIM), F32)
    c64 = jnp.concatenate([cos, cos, ones], axis=1)
    sa64 = jnp.concatenate([-sin, zeros8, zeros48], axis=1)
    sb64 = jnp.concatenate([zeros8, sin, zeros48], axis=1)
    rep = lambda a: jnp.concatenate([a, a], axis=1)
    pos_blk = jnp.arange(seq, dtype=jnp.int32)[:, None] // SLC_LEN
    lane = jnp.arange(LANE, dtype=jnp.int32)[None, :]
    onehot = ((lane >= 64) & (lane < 96) & (lane - 64 == pos_blk)).astype(F32)
    return rep(c64), rep(sa64), rep(sb64), onehot


def _prep_w_in(w):
    offs = np.cumsum((0,) + IN_WIDTHS)
    q, kc, vc, ks, vs, kw, vw, gn, qd, kd, vd, gm = [w[:, offs[i]:offs[i + 1]] for i in range(12)]
    z64 = jnp.zeros((w.shape[0], 64), w.dtype)
    ksa = jnp.concatenate([ks[:, :64], z64, ks[:, 64:], z64], axis=1)
    kwa = jnp.concatenate([kw[:, :64], z64, kw[:, 64:], z64], axis=1)
    gnp = jnp.pad(gn, ((0, 0), (0, LANE - gn.shape[1])))
    return jnp.concatenate([q, kc, vc, ksa, vs, kwa, vw, qd, kd, vd, gm, gnp], axis=1).astype(BF16)


def _prep_compress(cmp_pe, cmp_w1, cmp_b1, cmp_w2):
    half = CMP_LEN // 2
    pet, peb, wt, wb, b1 = [], [], [], [], []
    for kv in range(2):
        pe = cmp_pe[kv]
        tile2 = lambda a: jnp.concatenate([a, a], axis=1).reshape(1, half * 2 * HEAD_DIM)
        pet.append(tile2(pe[:half]))
        peb.append(tile2(pe[half:]))
        w1 = cmp_w1[kv].reshape(CMP_LEN, HEAD_DIM, CMP_HIDDEN)
        z = jnp.zeros((half, HEAD_DIM, CMP_HIDDEN), w1.dtype)

        def spread(wh):
            g0 = jnp.concatenate([wh, z], axis=2)
            g1 = jnp.concatenate([z, wh], axis=2)
            return jnp.stack([g0, g1], axis=1).reshape(half * 2 * HEAD_DIM, 2 * CMP_HIDDEN)

        wt.append(spread(w1[:half]))
        wb.append(spread(w1[half:]))
        b1.append(jnp.concatenate([cmp_b1[kv], cmp_b1[kv]])[None, :])
    w2k, w2v = cmp_w2[0], cmp_w2[1]
    zk = jnp.zeros_like(w2k)
    w2k_g = jnp.stack([
        jnp.concatenate([jnp.concatenate([w2k, zk], axis=1), jnp.zeros((CMP_HIDDEN, LANE), w2k.dtype)], axis=0),
        jnp.concatenate([jnp.zeros((CMP_HIDDEN, LANE), w2k.dtype), jnp.concatenate([w2k, zk], axis=1)], axis=0),
    ])
    zv = jnp.zeros_like(w2v)
    w2v_bd = jnp.concatenate([jnp.concatenate([w2v, zv], axis=1),
                              jnp.concatenate([zv, w2v], axis=1)], axis=0)
    st = lambda xs: jnp.stack(xs)
    return (st(pet), st(peb), st(wt).astype(BF16), st(wb).astype(BF16), st(b1),
            w2k_g.astype(BF16), w2v_bd.astype(BF16))


def kernel(x, w_in, cmp_pe, cmp_w1, cmp_b1, cmp_w2, diff_lambda, diff_norm_g, w_branch_a, w_branch_b,
           w_o, ln1_g, ln1_b, w_gate_up, w_down, ln2_g, ln2_b):
    nb, seq, d = x.shape
    assert d == D_MODEL and seq % T == 0 and seq // SLC_LEN == 32 and seq // CMP_STRIDE == 128
    n = nb * seq
    x2 = x.reshape(n, d)
    cos_t, sa_t, sb_t, oh_t = _rope_tables(seq)
    (qraw, qrot, kcs, vcs, ksa, vst, kwa, vwt, qdf, kdf, vdft, gm, gnt) = _inproj(
        x2, _prep_w_in(w_in[0]), cos_t, sa_t, sb_t, oh_t, seq)

    pet, peb, wt, wb, b1, w2k, w2v = _prep_compress(cmp_pe[0], cmp_w1[0], cmp_b1[0], cmp_w2[0])
    kca, vct = _compress(kcs, vcs, pet, peb, wt, wb, b1, w2k, w2v, nb, seq)

    ocmp, sel = _cmpsel(qraw, kca, vct, gnt, nb, seq)
    nt = n // T
    oslc = _slc(qrot, sel, ksa.reshape(nt, T, 256), vst, gnt, nb, seq)
    owin = _win(qrot, kwa.reshape(nt, T, 256), vwt, gnt, nb, seq)
    yb = _diff(qdf, kdf.reshape(nt, T, 512), vdft, diff_lambda[0], diff_norm_g[0][None, :], nb, seq)

    h1 = _merge(x2, ocmp, oslc, owin, yb, gm,
                w_branch_a[0].astype(BF16), w_branch_b[0].astype(BF16), w_o[0].astype(BF16),
                ln1_g[0][None, :], ln1_b[0][None, :])
    out = _ffn(h1, w_gate_up[0].astype(BF16), w_down[0].astype(BF16), ln2_g[0][None, :], ln2_b[0][None, :])
    return out.reshape(nb, seq, d)
```

```python
import functools
import math

import numpy as np
import jax
import jax.numpy as jnp
from jax import lax
from jax.experimental import pallas as pl
from jax.experimental.pallas import tpu as pltpu

D_MODEL = 1024
HEAD_DIM = 64
ROPE_DIM = HEAD_DIM // 4
ROPE_THETA = 500000.0
NSA_HEADS = 8
NSA_GROUPS = 2
NSA_HPG = NSA_HEADS // NSA_GROUPS
CMP_LEN = 32
CMP_STRIDE = 16
CMP_HIDDEN = 256
SLC_LEN = 64
SLC_TOPK = 8
WINDOW = 512
FORCE_BONUS = 1.0e4
DIFF_HEADS = 4
DIFF_V_DIM = 2 * HEAD_DIM
FFN_HIDDEN = ((8 * D_MODEL // 3 + 255) // 256) * 256
DEPTH = 1
DEEPNORM_ALPHA = (2 * DEPTH) ** 0.25
NEG = -1.0e30
LN_EPS = 1e-5
RMS_EPS = 1e-5
LAMBDA_INIT = 0.8 - 0.6 * math.exp(-0.3 * 0)
QK_SCALE = HEAD_DIM ** -0.5
QK_SCALE_LOG2 = QK_SCALE * math.log2(math.e)

NSA_Q = NSA_HEADS * HEAD_DIM
NSA_KV = NSA_GROUPS * HEAD_DIM
DIFF_QK = DIFF_HEADS * 2 * HEAD_DIM
DIFF_V = DIFF_HEADS * DIFF_V_DIM
IN_WIDTHS = (NSA_Q, NSA_KV, NSA_KV, NSA_KV, NSA_KV, NSA_KV, NSA_KV, 3 * NSA_HEADS,
             DIFF_QK, DIFF_QK, DIFF_V, 2 * D_MODEL)

LANE = 128
T = 256
TM = 512
VMEM_LIMIT = 56 * 1024 * 1024
GATE_ROWS = 32

BF16 = jnp.bfloat16
F32 = jnp.float32

_C_Q, _C_KC, _C_VC, _C_KS, _C_VS, _C_KW, _C_VW = 0, 512, 640, 768, 1024, 1152, 1408
_C_QD, _C_KD, _C_VD, _C_GM, _C_GN, _C_END = 1536, 2048, 2560, 3072, 5120, 5248


def _nt(a, b):
    return lax.dot_general(a, b, (((1,), (1,)), ((), ())), preferred_element_type=F32)


def _nn(a, b):
    return jnp.dot(a, b, preferred_element_type=F32)


def _params(n_axes, vmem=None):
    return pltpu.CompilerParams(dimension_semantics=("arbitrary",) * n_axes,
                                vmem_limit_bytes=vmem)


def _inproj_kernel(x_ref, w_ref, cos_ref, sa_ref, sb_ref, oh_ref,
                   qraw_ref, qrot_ref, kc_ref, vc_ref, ksa_ref, vst_ref, kwa_ref, vwt_ref,
                   qdf_ref, kdf_ref, vdft_ref, gm_ref, gn_ref):
    xb = x_ref[...].astype(BF16)
    cos = cos_ref[...]
    sa = sa_ref[...]
    sb = sb_ref[...]

    def mm(c0, n):
        return _nn(xb, w_ref[:, c0:c0 + n])

    def rope(t):
        return t * cos + pltpu.roll(t, LANE - 8, 1) * sa + pltpu.roll(t, 8, 1) * sb

    t = mm(_C_Q, 512)
    for j in range(4):
        tj = t[:, LANE * j:LANE * (j + 1)]
        qraw_ref[:, LANE * j:LANE * (j + 1)] = (tj * QK_SCALE_LOG2).astype(BF16)
        qrot_ref[:, LANE * j:LANE * (j + 1)] = (rope(tj) * QK_SCALE_LOG2).astype(BF16)

    t = mm(_C_KC, 256)
    kc_ref[...] = t[:, :LANE]
    vc_ref[...] = t[:, LANE:]

    t = mm(_C_KS, 256)
    oh = oh_ref[...]
    for g in range(2):
        ksa_ref[:, LANE * g:LANE * (g + 1)] = (rope(t[:, LANE * g:LANE * (g + 1)]) + oh).astype(BF16)

    t = mm(_C_VS, 128)
    vst_ref[...] = t.T.astype(BF16)

    t = mm(_C_KW, 256)
    for g in range(2):
        kwa_ref[:, LANE * g:LANE * (g + 1)] = rope(t[:, LANE * g:LANE * (g + 1)]).astype(BF16)

    t = mm(_C_VW, 128)
    vwt_ref[...] = t.T.astype(BF16)

    t = mm(_C_QD, 512)
    for j in range(4):
        qdf_ref[:, LANE * j:LANE * (j + 1)] = (rope(t[:, LANE * j:LANE * (j + 1)]) * QK_SCALE_LOG2).astype(BF16)
    t = mm(_C_KD, 512)
    for j in range(4):
        kdf_ref[:, LANE * j:LANE * (j + 1)] = rope(t[:, LANE * j:LANE * (j + 1)]).astype(BF16)
    t = mm(_C_VD, 512)
    for j in range(4):
        vdft_ref[LANE * j:LANE * (j + 1), :] = t[:, LANE * j:LANE * (j + 1)].T.astype(BF16)

    for j in range(4):
        t = mm(_C_GM + 512 * j, 512)
        gm_ref[:, 512 * j:512 * (j + 1)] = jax.nn.sigmoid(t).astype(BF16)

    gn_ref[...] = jax.nn.sigmoid(mm(_C_GN, 128)).T[0:GATE_ROWS]


def _inproj(x2, w_perm, cos_t, sa_t, sb_t, oh_t, seq):
    n = x2.shape[0]
    nt = n // T
    spt = seq // T
    row = lambda w: pl.BlockSpec((T, w), lambda i: (i, 0))
    tab = pl.BlockSpec((T, LANE), lambda i: (i % spt, 0))
    tile_t = lambda r: pl.BlockSpec((None, r, T), lambda i: (i, 0, 0))
    out_shape = (
        jax.ShapeDtypeStruct((n, 512), BF16),
        jax.ShapeDtypeStruct((n, 512), BF16),
        jax.ShapeDtypeStruct((n, LANE), F32),
        jax.ShapeDtypeStruct((n, LANE), F32),
        jax.ShapeDtypeStruct((n, 256), BF16),
        jax.ShapeDtypeStruct((nt, LANE, T), BF16),
        jax.ShapeDtypeStruct((n, 256), BF16),
        jax.ShapeDtypeStruct((nt, LANE, T), BF16),
        jax.ShapeDtypeStruct((n, 512), BF16),
        jax.ShapeDtypeStruct((n, 512), BF16),
        jax.ShapeDtypeStruct((nt, 512, T), BF16),
        jax.ShapeDtypeStruct((n, 2048), BF16),
        jax.ShapeDtypeStruct((nt, GATE_ROWS, T), F32),
    )
    out_specs = (row(512), row(512), row(LANE), row(LANE), row(256), tile_t(LANE), row(256),
                 tile_t(LANE), row(512), row(512), tile_t(512), row(2048), tile_t(GATE_ROWS))
    return pl.pallas_call(
        _inproj_kernel,
        out_shape=out_shape,
        grid=(nt,),
        in_specs=[row(D_MODEL),
                  pl.BlockSpec((D_MODEL, _C_END), lambda i: (0, 0)),
                  tab, tab, tab, tab],
        out_specs=out_specs,
        compiler_params=_params(1, VMEM_LIMIT),
        name="inproj",
    )(x2, w_perm, cos_t, sa_t, sb_t, oh_t)


def _gelu_tanh(x):
    c = math.sqrt(2.0 / math.pi)
    return x * (0.5 * (1.0 + jnp.tanh(c * (x + 0.044715 * (x * x * x)))))


def _compress_kernel(kf_ref, vf_ref, pet_ref, peb_ref, wt_ref, wb_ref, b1_ref, w2k_ref, w2v_ref,
                     kca_ref, vct_ref):
    nchunk = kf_ref.shape[0] // CMP_STRIDE

    def hidden(x_ref, kv):
        a = jnp.zeros((nchunk, 2 * CMP_HIDDEN), F32)
        b = jnp.zeros((nchunk, 2 * CMP_HIDDEN), F32)
        for p in range(0, CMP_STRIDE, 2):
            x = jnp.concatenate([x_ref[pl.ds(p, nchunk, stride=CMP_STRIDE), :],
                                 x_ref[pl.ds(p + 1, nchunk, stride=CMP_STRIDE), :]], axis=1)
            c0, c1 = LANE * p, LANE * (p + 2)
            a = a + _nn((x + pet_ref[kv, :, c0:c1]).astype(BF16), wt_ref[kv, c0:c1, :])
            b = b + _nn((x + peb_ref[kv, :, c0:c1]).astype(BF16), wb_ref[kv, c0:c1, :])
        h = a + pltpu.roll(b, nchunk - 1, 0) + b1_ref[kv]
        return _gelu_tanh(h).astype(BF16)

    hk = hidden(kf_ref, 0)
    for g in range(2):
        kca_ref[g] = _nn(hk, w2k_ref[g]).astype(BF16)
    hv = hidden(vf_ref, 1)
    vct_ref[...] = _nn(hv, w2v_ref[...]).T.astype(BF16)


def _compress(kf, vf, pet, peb, wt, wb, b1, w2k, w2v, nb, seq):
    full = lambda a: pl.BlockSpec(a.shape, lambda b: (0,) * a.ndim)
    return pl.pallas_call(
        _compress_kernel,
        out_shape=(jax.ShapeDtypeStruct((nb, 2, 128, LANE), BF16),
                   jax.ShapeDtypeStruct((nb, LANE, 128), BF16)),
        grid=(nb,),
        in_specs=[pl.BlockSpec((seq, LANE), lambda b: (b, 0)),
                  pl.BlockSpec((seq, LANE), lambda b: (b, 0)),
                  full(pet), full(peb), full(wt), full(wb), full(b1), full(w2k), full(w2v)],
        out_specs=(pl.BlockSpec((None, 2, 128, LANE), lambda b: (b, 0, 0, 0)),
                   pl.BlockSpec((None, LANE, 128), lambda b: (b, 0, 0))),
        compiler_params=_params(1, VMEM_LIMIT),
        name="compress",
    )(kf, vf, pet, peb, wt, wb, b1, w2k, w2v)


def _cmpsel_kernel(q_ref, kca_ref, vct_ref, gnt_ref, ocmp_ref, sel_ref):
    i = pl.program_id(1)
    t0 = i * T
    lane = lax.broadcasted_iota(jnp.int32, (T, LANE), 1)
    crow = lax.broadcasted_iota(jnp.int32, (128, T), 0)
    tcol = lax.broadcasted_iota(jnp.int32, (128, T), 1) + t0
    cmask = (crow * CMP_STRIDE + (CMP_LEN - 1)) <= tcol
    live = jnp.where((lax.broadcasted_iota(jnp.int32, (1, T), 1) + t0) >= (CMP_LEN - 1), 1.0, 0.0)

    jj = lax.broadcasted_iota(jnp.int32, (32, 128), 0) * SLC_LEN
    cc = lax.broadcasted_iota(jnp.int32, (32, 128), 1) * CMP_STRIDE
    ov = jnp.maximum(jnp.minimum(cc + CMP_LEN, jj + SLC_LEN) - jnp.maximum(cc, jj), 0)
    ovt = (ov.astype(F32) * (1.0 / CMP_LEN)).astype(BF16)

    jrow = lax.broadcasted_iota(jnp.int32, (32, T), 0)
    jrow8 = lax.broadcasted_iota(jnp.int32, (8, T), 0)
    tblk = (lax.broadcasted_iota(jnp.int32, (32, T), 1) + t0) // SLC_LEN
    valid = jrow <= tblk
    forced = (jrow == 0) | (jrow == tblk) | (jrow == tblk - 1)
    bonus = jnp.where(forced, FORCE_BONUS, 0.0)

    for g in range(2):
        kc = kca_ref[g]
        vt = vct_ref[64 * g:64 * (g + 1), :]
        psum = jnp.zeros((128, T), F32)
        for hp in range(2):
            qp = q_ref[:, LANE * (2 * g + hp):LANE * (2 * g + hp + 1)].astype(F32)
            qs = (jnp.where(lane < 64, qp, 0.0), jnp.where(lane < 64, pltpu.roll(qp, 64, 1), 0.0))
            outs = []
            for par in range(2):
                h = NSA_HPG * g + 2 * hp + par
                s = _nt(kc, qs[par].astype(BF16))
                s = jnp.where(cmask, s, NEG)
                m = jnp.max(s, axis=0, keepdims=True)
                e = jnp.exp2(s - m)
                p = e * (live / jnp.sum(e, axis=0, keepdims=True))
                psum = psum + p
                outs.append(_nn(vt, p.astype(BF16)) * gnt_ref[3 * h:3 * h + 1, :])
            ocmp_ref[:, LANE * (2 * g + hp):LANE * (2 * g + hp + 1)] = (
                jnp.concatenate(outs, axis=0).T.astype(BF16))

        p_hi = psum.astype(BF16)
        p_lo = (psum - p_hi.astype(F32)).astype(BF16)
        pslc = _nn(ovt, p_hi) + _nn(ovt, p_lo)
        pri = jnp.where(valid, pslc + bonus, -1.0)
        rank = [jnp.zeros((8, T), F32) for _ in range(4)]
        for r in range(32):
            row = pri[r:r + 1, :]
            for a in range(4):
                pa = pri[8 * a:8 * (a + 1), :]
                if 8 * a > r:
                    ahead = jnp.where(row >= pa, 1.0, 0.0)
                elif 8 * a + 7 < r:
                    ahead = jnp.where(row > pa, 1.0, 0.0)
                else:
                    ahead = jnp.where(jrow8 + 8 * a > r, jnp.where(row >= pa, 1.0, 0.0),
                                      jnp.where(row > pa, 1.0, 0.0))
                rank[a] = rank[a] + ahead
        rank = jnp.concatenate(rank, axis=0)
        selneg = jnp.where(rank < float(SLC_TOPK), 0.0, NEG)
        pad = jnp.concatenate([jnp.zeros((64, T), F32), selneg, jnp.zeros((32, T), F32)], axis=0)
        sel_ref[:, LANE * g:LANE * (g + 1)] = pad.T.astype(BF16)


def _cmpsel(qraw, kca, vct, gnt, nb, seq):
    n = qraw.shape[0]
    nq = seq // T
    return pl.pallas_call(
        _cmpsel_kernel,
        out_shape=(jax.ShapeDtypeStruct((n, 512), BF16), jax.ShapeDtypeStruct((n, 256), BF16)),
        grid=(nb, nq),
        in_specs=[pl.BlockSpec((T, 512), lambda b, i: (b * nq + i, 0)),
                  pl.BlockSpec((None, 2, 128, LANE), lambda b, i: (b, 0, 0, 0)),
                  pl.BlockSpec((None, LANE, 128), lambda b, i: (b, 0, 0)),
                  pl.BlockSpec((None, GATE_ROWS, T), lambda b, i: (b * nq + i, 0, 0))],
        out_specs=(pl.BlockSpec((T, 512), lambda b, i: (b * nq + i, 0)),
                   pl.BlockSpec((T, 256), lambda b, i: (b * nq + i, 0))),
        compiler_params=_params(2, VMEM_LIMIT),
        name="cmpsel",
    )(qraw, kca, vct, gnt)


def _split_heads(q_ref, extras, qs_scr, base=0):
    lane = lax.broadcasted_iota(jnp.int32, (T, LANE), 1)
    for hp in range(NSA_HEADS // 2):
        extra = extras[hp // (NSA_HPG // 2)]
        qp = q_ref[:, LANE * hp:LANE * (hp + 1)].astype(F32)
        qs_scr[base + 2 * hp] = jnp.where(lane < 64, qp, extra).astype(BF16)
        qs_scr[base + 2 * hp + 1] = jnp.where(lane < 64, pltpu.roll(qp, 64, 1), extra).astype(BF16)


SM_ROWS = 32


def _tile_schedule(nq):
    pairs = [(i, i) for i in range(nq)] + [(i, j) for j in range(nq) for i in range(j + 1, nq)]
    assert nq % 2 == 0 and nq >= 4 and len(pairs) % 2 == 0
    return (jnp.asarray([p[0] for p in pairs], jnp.int32), jnp.asarray([p[1] for p in pairs], jnp.int32))


def _flash_scratch(nq, nc, acc_rows):
    return [pltpu.VMEM((nq * nc, T, LANE), BF16),
            pltpu.VMEM((2, nc, T, T), F32),
            pltpu.VMEM((2, nc, 1, T), F32),
            pltpu.VMEM((nq * nc, 1, T), F32),
            pltpu.VMEM((nq * nc, 1, T), F32),
            pltpu.VMEM((nq * acc_rows // LANE, LANE, T), F32)]


def _causal_flash(it_ref, jt_ref, nq, k_ref, vt_ref, qs_scr, s_scr, mx_scr, m_scr, l_scr, acc_scr, chains, acc_of):
    nc = len(chains)
    nsets = it_ref.shape[0]
    m_scr[...] = jnp.full(m_scr.shape, NEG, F32)
    l_scr[...] = jnp.zeros(l_scr.shape, F32)
    acc_scr[...] = jnp.zeros(acc_scr.shape, F32)
    causal = (lax.broadcasted_iota(jnp.int32, (T, T), 0) <= lax.broadcasted_iota(jnp.int32, (T, T), 1))
    ones = jnp.ones((16, T), BF16)

    def scores(n, buf, diag):
        i, j = it_ref[n], jt_ref[n]
        for c, (kl, _, _) in enumerate(chains):
            s = _nt(k_ref[j, :, kl:kl + LANE], qs_scr[i * nc + c])
            if diag:
                s = jnp.where(causal, s, NEG)
            s_scr[buf, c] = s
            mx_scr[buf, c] = jnp.max(s, axis=0, keepdims=True)

    def softmax_pv(n, buf):
        i, j = it_ref[n], jt_ref[n]
        for c, (_, vr, dv) in enumerate(chains):
            m_prev = m_scr[i * nc + c]
            m_new = jnp.maximum(m_prev, mx_scr[buf, c])
            alpha = jnp.exp2(m_prev - m_new)
            parts = [jnp.exp2(s_scr[buf, c, r:r + SM_ROWS, :] - m_new).astype(BF16)
                     for r in range(0, T, SM_ROWS)]
            vt = jnp.concatenate([vt_ref[j, vr:vr + dv, :], ones], axis=0)
            o = _nn(vt, jnp.concatenate(parts, axis=0))
            l_scr[i * nc + c] = alpha * l_scr[i * nc + c] + o[dv:dv + 1]
            m_scr[i * nc + c] = m_new
            acc = acc_of(i, c)
            acc[...] = acc[...] * alpha + o[0:dv]

    def sweep(lo, hi, diag):
        scores(lo, 0, diag)

        def body(u, carry):
            n = lo + 2 * u
            scores(n + 1, 1, diag)
            softmax_pv(n, 0)
            scores(n + 2, 0, diag)
            softmax_pv(n + 1, 1)
            return carry

        lax.fori_loop(0, (hi - lo - 2) // 2, body, 0)
        scores(hi - 1, 1, diag)
        softmax_pv(hi - 2, 0)
        softmax_pv(hi - 1, 1)

    sweep(0, nq, True)
    sweep(nq, nsets, False)


def _slc_kernel(it_ref, jt_ref, q_ref, sel_ref, k_ref, vt_ref, gnt_ref, o_ref,
                qs_scr, s_scr, mx_scr, m_scr, l_scr, acc_scr):
    nq = q_ref.shape[0] // T
    nh = NSA_HEADS

    def prep(i, carry):
        rows = pl.ds(pl.multiple_of(i * T, T), T)
        sel = sel_ref[rows, :].astype(F32)
        _split_heads(q_ref.at[rows, :], [sel[:, 0:LANE], sel[:, LANE:2 * LANE]], qs_scr, i * nh)
        return carry

    lax.fori_loop(0, nq, prep, 0)
    chains = [(LANE * (h // NSA_HPG), 64 * (h // NSA_HPG), 64) for h in range(nh)]
    acc_of = lambda i, h: acc_scr.at[i * (nh // 2) + h // 2, 64 * (h % 2):64 * (h % 2 + 1), :]
    _causal_flash(it_ref, jt_ref, nq, k_ref, vt_ref, qs_scr, s_scr, mx_scr, m_scr, l_scr, acc_scr, chains, acc_of)

    def finish(i, carry):
        rows = pl.ds(pl.multiple_of(i * T, T), T)
        for hp in range(nh // 2):
            scale = [gnt_ref[i, 3 * h + 1:3 * h + 2, :] / l_scr[i * nh + h] for h in (2 * hp, 2 * hp + 1)]
            inv = jnp.concatenate([jnp.broadcast_to(scale[0], (64, T)), jnp.broadcast_to(scale[1], (64, T))],
                                  axis=0)
            o_ref[rows, LANE * hp:LANE * (hp + 1)] = (acc_scr[i * (nh // 2) + hp] * inv).T.astype(BF16)
        return carry

    lax.fori_loop(0, nq, finish, 0)


def _slc(qrot, sel, ksa3, vst, gnt3, nb, seq):
    n = qrot.shape[0]
    nq = seq // T
    it, jt = _tile_schedule(nq)
    grid_spec = pltpu.PrefetchScalarGridSpec(
        num_scalar_prefetch=2,
        grid=(nb,),
        in_specs=[pl.BlockSpec((seq, 512), lambda b, it, jt: (b, 0)),
                  pl.BlockSpec((seq, 256), lambda b, it, jt: (b, 0)),
                  pl.BlockSpec((nq, T, 256), lambda b, it, jt: (b, 0, 0)),
                  pl.BlockSpec((nq, LANE, T), lambda b, it, jt: (b, 0, 0)),
                  pl.BlockSpec((nq, GATE_ROWS, T), lambda b, it, jt: (b, 0, 0))],
        out_specs=pl.BlockSpec((seq, 512), lambda b, it, jt: (b, 0)),
        scratch_shapes=_flash_scratch(nq, NSA_HEADS, NSA_Q))
    return pl.pallas_call(
        _slc_kernel,
        out_shape=jax.ShapeDtypeStruct((n, 512), BF16),
        grid_spec=grid_spec,
        compiler_params=_params(1, VMEM_LIMIT),
        name="slc",
    )(it, jt, qrot, sel, ksa3, vst, gnt3)


def _win_kernel(q_ref, k_ref, vt_ref, gnt_ref, o_ref, qs_scr, s_scr, mx_scr, acc_scr):
    i = pl.program_id(1)
    _split_heads(q_ref, [0.0, 0.0], qs_scr)
    krow = lax.broadcasted_iota(jnp.int32, (T, T), 0)
    qcol = lax.broadcasted_iota(jnp.int32, (T, T), 1)
    span = WINDOW // T
    masks = {0: krow <= qcol, span: (krow > qcol) & (i >= span)}
    ones = jnp.ones((16, T), BF16)

    def scores(hp, buf):
        for par in range(2):
            h = 2 * hp + par
            g = h // NSA_HPG
            mx = None
            for d in range(span + 1):
                j = i - d
                s = _nt(k_ref[jnp.maximum(j, 0), :, LANE * g:LANE * (g + 1)], qs_scr[h])
                if d in masks:
                    s = jnp.where(masks[d], s, NEG)
                else:
                    s = s + jnp.where(j >= 0, 0.0, NEG)
                s_scr[buf, par, d] = s
                md = jnp.max(s, axis=0, keepdims=True)
                mx = md if mx is None else jnp.maximum(mx, md)
            mx_scr[buf, par] = mx

    def softmax_pv(hp, buf):
        for par in range(2):
            g = (2 * hp + par) // NSA_HPG
            m = mx_scr[buf, par]
            acc = jnp.zeros((64 + 16, T), F32)
            for d in range(span + 1):
                parts = [jnp.exp2(s_scr[buf, par, d, r:r + SM_ROWS, :] - m).astype(BF16)
                         for r in range(0, T, SM_ROWS)]
                vt = jnp.concatenate([vt_ref[jnp.maximum(i - d, 0), 64 * g:64 * (g + 1), :], ones], axis=0)
                acc = acc + _nn(vt, jnp.concatenate(parts, axis=0))
            gate = gnt_ref[3 * (2 * hp + par) + 2:3 * (2 * hp + par) + 3, :]
            acc_scr[hp, 64 * par:64 * (par + 1), :] = acc[0:64] * (gate / acc[64:65])

    npairs = NSA_HEADS // 2
    scores(0, 0)
    for hp in range(npairs):
        if hp + 1 < npairs:
            scores(hp + 1, (hp + 1) % 2)
        softmax_pv(hp, hp % 2)
    for hp in range(npairs):
        o_ref[:, LANE * hp:LANE * (hp + 1)] = acc_scr[hp].T.astype(BF16)


def _win(qrot, kwa3, vwt, gnt, nb, seq):
    n = qrot.shape[0]
    nq = seq // T
    return pl.pallas_call(
        _win_kernel,
        out_shape=jax.ShapeDtypeStruct((n, 512), BF16),
        grid=(nb, nq),
        in_specs=[pl.BlockSpec((T, 512), lambda b, i: (b * nq + i, 0)),
                  pl.BlockSpec((nq, T, 256), lambda b, i: (b, 0, 0)),
                  pl.BlockSpec((nq, LANE, T), lambda b, i: (b, 0, 0)),
                  pl.BlockSpec((None, GATE_ROWS, T), lambda b, i: (b * nq + i, 0, 0))],
        out_specs=pl.BlockSpec((T, 512), lambda b, i: (b * nq + i, 0)),
        scratch_shapes=[pltpu.VMEM((NSA_HEADS, T, LANE), BF16),
                        pltpu.VMEM((2, 2, WINDOW // T + 1, T, T), F32),
                        pltpu.VMEM((2, 2, 1, T), F32),
                        pltpu.VMEM((NSA_HEADS // 2, LANE, T), F32)],
        compiler_params=_params(2, VMEM_LIMIT),
        name="win",
    )(qrot, kwa3, vwt, gnt)


def _diff_kernel(it_ref, jt_ref, q_ref, k_ref, vt_ref, lam_ref, g_ref, o_ref,
                 qs_scr, s_scr, mx_scr, m_scr, l_scr, acc_scr):
    nq = q_ref.shape[0] // T
    nc = 2 * DIFF_HEADS
    lane = lax.broadcasted_iota(jnp.int32, (T, LANE), 1)

    def prep(i, carry):
        rows = pl.ds(pl.multiple_of(i * T, T), T)
        for h in range(DIFF_HEADS):
            q = q_ref[rows, LANE * h:LANE * (h + 1)].astype(F32)
            qs_scr[i * nc + 2 * h] = jnp.where(lane < 64, q, 0.0).astype(BF16)
            qs_scr[i * nc + 2 * h + 1] = jnp.where(lane < 64, 0.0, q).astype(BF16)
        return carry

    lax.fori_loop(0, nq, prep, 0)
    chains = [(LANE * (c // 2), LANE * (c // 2), LANE) for c in range(nc)]
    _causal_flash(it_ref, jt_ref, nq, k_ref, vt_ref, qs_scr, s_scr, mx_scr, m_scr, l_scr, acc_scr, chains,
                  lambda i, c: acc_scr.at[i * nc + c])

    lp = lam_ref[...]
    lam = (jnp.exp(jnp.sum(lp[0:1] * lp[1:2], axis=1, keepdims=True))
           - jnp.exp(jnp.sum(lp[2:3] * lp[3:4], axis=1, keepdims=True)) + LAMBDA_INIT)

    def finish(i, carry):
        rows = pl.ds(pl.multiple_of(i * T, T), T)
        for h in range(DIFF_HEADS):
            c = i * nc + 2 * h
            o = acc_scr[c] * (1.0 / l_scr[c]) - lam * (acc_scr[c + 1] * (1.0 / l_scr[c + 1]))
            o = o * lax.rsqrt(jnp.mean(o * o, axis=0, keepdims=True) + RMS_EPS)
            o_ref[rows, LANE * h:LANE * (h + 1)] = ((o.T * g_ref[...]) * (1.0 - LAMBDA_INIT)).astype(BF16)
        return carry

    lax.fori_loop(0, nq, finish, 0)


def _diff(qdf, kdf3, vdft, lam, norm_g, nb, seq):
    n = qdf.shape[0]
    nq = seq // T
    it, jt = _tile_schedule(nq)
    grid_spec = pltpu.PrefetchScalarGridSpec(
        num_scalar_prefetch=2,
        grid=(nb,),
        in_specs=[pl.BlockSpec((seq, 512), lambda b, it, jt: (b, 0)),
                  pl.BlockSpec((nq, T, 512), lambda b, it, jt: (b, 0, 0)),
                  pl.BlockSpec((nq, 512, T), lambda b, it, jt: (b, 0, 0)),
                  pl.BlockSpec((4, HEAD_DIM), lambda b, it, jt: (0, 0)),
                  pl.BlockSpec((1, LANE), lambda b, it, jt: (0, 0))],
        out_specs=pl.BlockSpec((seq, 512), lambda b, it, jt: (b, 0)),
        scratch_shapes=_flash_scratch(nq, 2 * DIFF_HEADS, 2 * DIFF_HEADS * DIFF_V_DIM))
    return pl.pallas_call(
        _diff_kernel,
        out_shape=jax.ShapeDtypeStruct((n, 512), BF16),
        grid_spec=grid_spec,
        compiler_params=_params(1, VMEM_LIMIT),
        name="diff",
    )(it, jt, qdf, kdf3, vdft, lam, norm_g)


def _layer_norm(z, g, b):
    mu = jnp.mean(z, axis=-1, keepdims=True)
    zc = z - mu
    var = jnp.mean(zc * zc, axis=-1, keepdims=True)
    return zc * lax.rsqrt(var + LN_EPS) * g + b


def _merge_kernel(x_ref, ocmp_ref, oslc_ref, owin_ref, yb_ref, gm_ref,
                  wa_ref, wb_ref, wo_ref, g_ref, b_ref, h_ref):
    ya = ocmp_ref[...].astype(F32) + oslc_ref[...].astype(F32) + owin_ref[...].astype(F32)
    ta = _nn(ya.astype(BF16), wa_ref[...])
    tb = _nn(yb_ref[...], wb_ref[...])
    merged = gm_ref[:, 0:D_MODEL].astype(F32) * ta + gm_ref[:, D_MODEL:2 * D_MODEL].astype(F32) * tb
    mix = _nn(merged.astype(BF16), wo_ref[...])
    h_ref[...] = _layer_norm(DEEPNORM_ALPHA * x_ref[...] + mix, g_ref[...], b_ref[...])


def _merge(x2, ocmp, oslc, owin, yb, gm, wa, wb, wo, g, b):
    n = x2.shape[0]
    row = lambda w: pl.BlockSpec((TM, w), lambda i: (i, 0))
    full = lambda a: pl.BlockSpec(a.shape, lambda i: (0,) * a.ndim)
    return pl.pallas_call(
        _merge_kernel,
        out_shape=jax.ShapeDtypeStruct((n, D_MODEL), F32),
        grid=(n // TM,),
        in_specs=[row(D_MODEL), row(512), row(512), row(512), row(512), row(2048),
                  full(wa), full(wb), full(wo), full(g), full(b)],
        out_specs=row(D_MODEL),
        compiler_params=_params(1, VMEM_LIMIT),
        name="merge",
    )(x2, ocmp, oslc, owin, yb, gm, wa, wb, wo, g, b)


def _ffn_kernel(h_ref, wgu_ref, wd_ref, g_ref, b_ref, o_ref):
    h = h_ref[...]
    hb = h.astype(BF16)
    gate = _nn(hb, wgu_ref[:, 0:FFN_HIDDEN])
    up = _nn(hb, wgu_ref[:, FFN_HIDDEN:2 * FFN_HIDDEN])
    act = (gate * jax.nn.sigmoid(gate) * up).astype(BF16)
    y = _nn(act, wd_ref[...])
    o_ref[...] = _layer_norm(DEEPNORM_ALPHA * h + y, g_ref[...], b_ref[...])


def _ffn(h1, wgu, wd, g, b):
    n = h1.shape[0]
    row = pl.BlockSpec((T, D_MODEL), lambda i: (i, 0))
    full = lambda a: pl.BlockSpec(a.shape, lambda i: (0,) * a.ndim)
    return pl.pallas_call(
        _ffn_kernel,
        out_shape=jax.ShapeDtypeStruct((n, D_MODEL), F32),
        grid=(n // T,),
        in_specs=[row, full(wgu), full(wd), full(g), full(b)],
        out_specs=row,
        compiler_params=_params(1, VMEM_LIMIT),
        name="ffn",
    )(h1, wgu, wd, g, b)


def _rope_tables(seq):
    half = ROPE_DIM // 2
    inv_freq = ROPE_THETA ** (-jnp.arange(half, dtype=F32) * 2.0 / ROPE_DIM)
    ang = jnp.arange(seq, dtype=F32)[:, None] * inv_freq[None, :]
    cos, sin = jnp.cos(ang), jnp.sin(ang)
    ones = jnp.ones((seq, HEAD_DIM - ROPE_DIM), F32)
    zeros8 = jnp.zeros((seq, half), F32)
    zeros48 = jnp.zeros((seq, HEAD_DIM - ROPE_DIM), F32)
    c64 = jnp.concatenate([cos, cos, ones], axis=1)
    sa64 = jnp.concatenate([-sin, zeros8, zeros48], axis=1)
    sb64 = jnp.concatenate([zeros8, sin, zeros48], axis=1)
    rep = lambda a: jnp.concatenate([a, a], axis=1)
    pos_blk = jnp.arange(seq, dtype=jnp.int32)[:, None] // SLC_LEN
    lane = jnp.arange(LANE, dtype=jnp.int32)[None, :]
    onehot = ((lane >= 64) & (lane < 96) & (lane - 64 == pos_blk)).astype(F32)
    return rep(c64), rep(sa64), rep(sb64), onehot


def _prep_w_in(w):
    offs = np.cumsum((0,) + IN_WIDTHS)
    q, kc, vc, ks, vs, kw, vw, gn, qd, kd, vd, gm = [w[:, offs[i]:offs[i + 1]] for i in range(12)]
    z64 = jnp.zeros((w.shape[0], 64), w.dtype)
    ksa = jnp.concatenate([ks[:, :64], z64, ks[:, 64:], z64], axis=1)
    kwa = jnp.concatenate([kw[:, :64], z64, kw[:, 64:], z64], axis=1)
    gnp = jnp.pad(gn, ((0, 0), (0, LANE - gn.shape[1])))
    return jnp.concatenate([q, kc, vc, ksa, vs, kwa, vw, qd, kd, vd, gm, gnp], axis=1).astype(BF16)


def _prep_compress(cmp_pe, cmp_w1, cmp_b1, cmp_w2):
    half = CMP_LEN // 2
    pet, peb, wt, wb, b1 = [], [], [], [], []
    for kv in range(2):
        pe = cmp_pe[kv]
        tile2 = lambda a: jnp.concatenate([a, a], axis=1).reshape(1, half * 2 * HEAD_DIM)
        pet.append(tile2(pe[:half]))
        peb.append(tile2(pe[half:]))
        w1 = cmp_w1[kv].reshape(CMP_LEN, HEAD_DIM, CMP_HIDDEN)
        z = jnp.zeros((half, HEAD_DIM, CMP_HIDDEN), w1.dtype)

        def spread(wh):
            g0 = jnp.concatenate([wh, z], axis=2)
            g1 = jnp.concatenate([z, wh], axis=2)
            return jnp.stack([g0, g1], axis=1).reshape(half * 2 * HEAD_DIM, 2 * CMP_HIDDEN)

        wt.append(spread(w1[:half]))
        wb.append(spread(w1[half:]))
        b1.append(jnp.concatenate([cmp_b1[kv], cmp_b1[kv]])[None, :])
    w2k, w2v = cmp_w2[0], cmp_w2[1]
    zk = jnp.zeros_like(w2k)
    w2k_g = jnp.stack([
        jnp.concatenate([jnp.concatenate([w2k, zk], axis=1), jnp.zeros((CMP_HIDDEN, LANE), w2k.dtype)], axis=0),
        jnp.concatenate([jnp.zeros((CMP_HIDDEN, LANE), w2k.dtype), jnp.concatenate([w2k, zk], axis=1)], axis=0),
    ])
    zv = jnp.zeros_like(w2v)
    w2v_bd = jnp.concatenate([jnp.concatenate([w2v, zv], axis=1),
                              jnp.concatenate([zv, w2v], axis=1)], axis=0)
    st = lambda xs: jnp.stack(xs)
    return (st(pet), st(peb), st(wt).astype(BF16), st(wb).astype(BF16), st(b1),
            w2k_g.astype(BF16), w2v_bd.astype(BF16))


def kernel(x, w_in, cmp_pe, cmp_w1, cmp_b1, cmp_w2, diff_lambda, diff_norm_g, w_branch_a, w_branch_b,
           w_o, ln1_g, ln1_b, w_gate_up, w_down, ln2_g, ln2_b):
    nb, seq, d = x.shape
    assert d == D_MODEL and seq % T == 0 and seq // SLC_LEN == 32 and seq // CMP_STRIDE == 128
    n = nb * seq
    x2 = x.reshape(n, d)
    cos_t, sa_t, sb_t, oh_t = _rope_tables(seq)
    (qraw, qrot, kcs, vcs, ksa, vst, kwa, vwt, qdf, kdf, vdft, gm, gnt) = _inproj(
        x2, _prep_w_in(w_in[0]), cos_t, sa_t, sb_t, oh_t, seq)

    pet, peb, wt, wb, b1, w2k, w2v = _prep_compress(cmp_pe[0], cmp_w1[0], cmp_b1[0], cmp_w2[0])
    kca, vct = _compress(kcs, vcs, pet, peb, wt, wb, b1, w2k, w2v, nb, seq)

    ocmp, sel = _cmpsel(qraw, kca, vct, gnt, nb, seq)
    nt = n // T
    oslc = _slc(qrot, sel, ksa.reshape(nt, T, 256), vst, gnt, nb, seq)
    owin = _win(qrot, kwa.reshape(nt, T, 256), vwt, gnt, nb, seq)
    yb = _diff(qdf, kdf.reshape(nt, T, 512), vdft, diff_lambda[0], diff_norm_g[0][None, :], nb, seq)

    h1 = _merge(x2, ocmp, oslc, owin, yb, gm,
                w_branch_a[0].astype(BF16), w_branch_b[0].astype(BF16), w_o[0].astype(BF16),
                ln1_g[0][None, :], ln1_b[0][None, :])
    out = _ffn(h1, w_gate_up[0].astype(BF16), w_down[0].astype(BF16), ln2_g[0][None, :], ln2_b[0][None, :])
    return out.reshape(nb, seq, d)
```

```python
import functools
import math

import numpy as np
import jax
import jax.numpy as jnp
from jax import lax
from jax.experimental import pallas as pl
from jax.experimental.pallas import tpu as pltpu

D_MODEL = 1024
HEAD_DIM = 64
ROPE_DIM = HEAD_DIM // 4
ROPE_THETA = 500000.0
NSA_HEADS = 8
NSA_GROUPS = 2
NSA_HPG = NSA_HEADS // NSA_GROUPS
CMP_LEN = 32
CMP_STRIDE = 16
CMP_HIDDEN = 256
SLC_LEN = 64
SLC_TOPK = 8
WINDOW = 512
FORCE_BONUS = 1.0e4
DIFF_HEADS = 4
DIFF_V_DIM = 2 * HEAD_DIM
FFN_HIDDEN = ((8 * D_MODEL // 3 + 255) // 256) * 256
DEPTH = 1
DEEPNORM_ALPHA = (2 * DEPTH) ** 0.25
NEG = -1.0e30
LN_EPS = 1e-5
RMS_EPS = 1e-5
LAMBDA_INIT = 0.8 - 0.6 * math.exp(-0.3 * 0)
QK_SCALE = HEAD_DIM ** -0.5
QK_SCALE_LOG2 = QK_SCALE * math.log2(math.e)

NSA_Q = NSA_HEADS * HEAD_DIM
NSA_KV = NSA_GROUPS * HEAD_DIM
DIFF_QK = DIFF_HEADS * 2 * HEAD_DIM
DIFF_V = DIFF_HEADS * DIFF_V_DIM
IN_WIDTHS = (NSA_Q, NSA_KV, NSA_KV, NSA_KV, NSA_KV, NSA_KV, NSA_KV, 3 * NSA_HEADS,
             DIFF_QK, DIFF_QK, DIFF_V, 2 * D_MODEL)

LANE = 128
T = 256
TM = 512
VMEM_LIMIT = 56 * 1024 * 1024
GATE_ROWS = 32

BF16 = jnp.bfloat16
F32 = jnp.float32

_C_Q, _C_KC, _C_VC, _C_KS, _C_VS, _C_KW, _C_VW = 0, 512, 640, 768, 1024, 1152, 1408
_C_QD, _C_KD, _C_VD, _C_GM, _C_GN, _C_END = 1536, 2048, 2560, 3072, 5120, 5248


def _nt(a, b):
    return lax.dot_general(a, b, (((1,), (1,)), ((), ())), preferred_element_type=F32)


def _nn(a, b):
    return jnp.dot(a, b, preferred_element_type=F32)


def _params(n_axes, vmem=None):
    return pltpu.CompilerParams(dimension_semantics=("arbitrary",) * n_axes,
                                vmem_limit_bytes=vmem)


def _inproj_kernel(x_ref, w_ref, cos_ref, sa_ref, sb_ref, oh_ref,
                   qraw_ref, qrot_ref, kc_ref, vc_ref, ksa_ref, vst_ref, kwa_ref, vwt_ref,
                   qdf_ref, kdf_ref, vdft_ref, gm_ref, gn_ref):
    xb = x_ref[...].astype(BF16)
    cos = cos_ref[...]
    sa = sa_ref[...]
    sb = sb_ref[...]

    def mm(c0, n):
        return _nn(xb, w_ref[:, c0:c0 + n])

    def rope(t):
        return t * cos + pltpu.roll(t, LANE - 8, 1) * sa + pltpu.roll(t, 8, 1) * sb

    t = mm(_C_Q, 512)
    for j in range(4):
        tj = t[:, LANE * j:LANE * (j + 1)]
        qraw_ref[:, LANE * j:LANE * (j + 1)] = (tj * QK_SCALE_LOG2).astype(BF16)
        qrot_ref[:, LANE * j:LANE * (j + 1)] = (rope(tj) * QK_SCALE_LOG2).astype(BF16)

    t = mm(_C_KC, 256)
    kc_ref[...] = t[:, :LANE]
    vc_ref[...] = t[:, LANE:]

    t = mm(_C_KS, 256)
    oh = oh_ref[...]
    for g in range(2):
        ksa_ref[:, LANE * g:LANE * (g + 1)] = (rope(t[:, LANE * g:LANE * (g + 1)]) + oh).astype(BF16)

    t = mm(_C_VS, 128)
    vst_ref[...] = t.T.astype(BF16)

    t = mm(_C_KW, 256)
    for g in range(2):
        kwa_ref[:, LANE * g:LANE * (g + 1)] = rope(t[:, LANE * g:LANE * (g + 1)]).astype(BF16)

    t = mm(_C_VW, 128)
    vwt_ref[...] = t.T.astype(BF16)

    t = mm(_C_QD, 512)
    for j in range(4):
        qdf_ref[:, LANE * j:LANE * (j + 1)] = (rope(t[:, LANE * j:LANE * (j + 1)]) * QK_SCALE_LOG2).astype(BF16)
    t = mm(_C_KD, 512)
    for j in range(4):
        kdf_ref[:, LANE * j:LANE * (j + 1)] = rope(t[:, LANE * j:LANE * (j + 1)]).astype(BF16)
    t = mm(_C_VD, 512)
    for j in range(4):
        vdft_ref[LANE * j:LANE * (j + 1), :] = t[:, LANE * j:LANE * (j + 1)].T.astype(BF16)

    for j in range(4):
        t = mm(_C_GM + 512 * j, 512)
        gm_ref[:, 512 * j:512 * (j + 1)] = jax.nn.sigmoid(t).astype(BF16)

    gn_ref[...] = jax.nn.sigmoid(mm(_C_GN, 128)).T[0:GATE_ROWS]


def _inproj(x2, w_perm, cos_t, sa_t, sb_t, oh_t, seq):
    n = x2.shape[0]
    nt = n // T
    spt = seq // T
    row = lambda w: pl.BlockSpec((T, w), lambda i: (i, 0))
    tab = pl.BlockSpec((T, LANE), lambda i: (i % spt, 0))
    tile_t = lambda r: pl.BlockSpec((None, r, T), lambda i: (i, 0, 0))
    out_shape = (
        jax.ShapeDtypeStruct((n, 512), BF16),
        jax.ShapeDtypeStruct((n, 512), BF16),
        jax.ShapeDtypeStruct((n, LANE), F32),
        jax.ShapeDtypeStruct((n, LANE), F32),
        jax.ShapeDtypeStruct((n, 256), BF16),
        jax.ShapeDtypeStruct((nt, LANE, T), BF16),
        jax.ShapeDtypeStruct((n, 256), BF16),
        jax.ShapeDtypeStruct((nt, LANE, T), BF16),
        jax.ShapeDtypeStruct((n, 512), BF16),
        jax.ShapeDtypeStruct((n, 512), BF16),
        jax.ShapeDtypeStruct((nt, 512, T), BF16),
        jax.ShapeDtypeStruct((n, 2048), BF16),
        jax.ShapeDtypeStruct((nt, GATE_ROWS, T), F32),
    )
    out_specs = (row(512), row(512), row(LANE), row(LANE), row(256), tile_t(LANE), row(256),
                 tile_t(LANE), row(512), row(512), tile_t(512), row(2048), tile_t(GATE_ROWS))
    return pl.pallas_call(
        _inproj_kernel,
        out_shape=out_shape,
        grid=(nt,),
        in_specs=[row(D_MODEL),
                  pl.BlockSpec((D_MODEL, _C_END), lambda i: (0, 0)),
                  tab, tab, tab, tab],
        out_specs=out_specs,
        compiler_params=_params(1, VMEM_LIMIT),
        name="inproj",
    )(x2, w_perm, cos_t, sa_t, sb_t, oh_t)


def _gelu_tanh(x):
    c = math.sqrt(2.0 / math.pi)
    return x * (0.5 * (1.0 + jnp.tanh(c * (x + 0.044715 * (x * x * x)))))


def _compress_kernel(kf_ref, vf_ref, pet_ref, peb_ref, wt_ref, wb_ref, b1_ref, w2k_ref, w2v_ref,
                     kca_ref, vct_ref):
    nchunk = kf_ref.shape[0] // CMP_STRIDE

    def hidden(x_ref, kv):
        a = jnp.zeros((nchunk, 2 * CMP_HIDDEN), F32)
        b = jnp.zeros((nchunk, 2 * CMP_HIDDEN), F32)
        for p in range(0, CMP_STRIDE, 2):
            x = jnp.concatenate([x_ref[pl.ds(p, nchunk, stride=CMP_STRIDE), :],
                                 x_ref[pl.ds(p + 1, nchunk, stride=CMP_STRIDE), :]], axis=1)
            c0, c1 = LANE * p, LANE * (p + 2)
            a = a + _nn((x + pet_ref[kv, :, c0:c1]).astype(BF16), wt_ref[kv, c0:c1, :])
            b = b + _nn((x + peb_ref[kv, :, c0:c1]).astype(BF16), wb_ref[kv, c0:c1, :])
        h = a + pltpu.roll(b, nchunk - 1, 0) + b1_ref[kv]
        return _gelu_tanh(h).astype(BF16)

    hk = hidden(kf_ref, 0)
    for g in range(2):
        kca_ref[g] = _nn(hk, w2k_ref[g]).astype(BF16)
    hv = hidden(vf_ref, 1)
    vct_ref[...] = _nn(hv, w2v_ref[...]).T.astype(BF16)


def _compress(kf, vf, pet, peb, wt, wb, b1, w2k, w2v, nb, seq):
    full = lambda a: pl.BlockSpec(a.shape, lambda b: (0,) * a.ndim)
    return pl.pallas_call(
        _compress_kernel,
        out_shape=(jax.ShapeDtypeStruct((nb, 2, 128, LANE), BF16),
                   jax.ShapeDtypeStruct((nb, LANE, 128), BF16)),
        grid=(nb,),
        in_specs=[pl.BlockSpec((seq, LANE), lambda b: (b, 0)),
                  pl.BlockSpec((seq, LANE), lambda b: (b, 0)),
                  full(pet), full(peb), full(wt), full(wb), full(b1), full(w2k), full(w2v)],
        out_specs=(pl.BlockSpec((None, 2, 128, LANE), lambda b: (b, 0, 0, 0)),
                   pl.BlockSpec((None, LANE, 128), lambda b: (b, 0, 0))),
        compiler_params=_params(1, VMEM_LIMIT),
        name="compress",
    )(kf, vf, pet, peb, wt, wb, b1, w2k, w2v)


def _cmpsel_kernel(q_ref, kca_ref, vct_ref, gnt_ref, ocmp_ref, sel_ref, s_scr):
    nq = q_ref.shape[0] // T
    lane = lax.broadcasted_iota(jnp.int32, (T, LANE), 1)
    crow = lax.broadcasted_iota(jnp.int32, (128, T), 0)
    tcol0 = lax.broadcasted_iota(jnp.int32, (128, T), 1)
    tlane = lax.broadcasted_iota(jnp.int32, (1, T), 1)

    jj = lax.broadcasted_iota(jnp.int32, (32, 128), 0) * SLC_LEN
    cc = lax.broadcasted_iota(jnp.int32, (32, 128), 1) * CMP_STRIDE
    ov = jnp.maximum(jnp.minimum(cc + CMP_LEN, jj + SLC_LEN) - jnp.maximum(cc, jj), 0)
    ovt = (ov.astype(F32) * (1.0 / CMP_LEN)).astype(BF16)

    jrow = lax.broadcasted_iota(jnp.int32, (32, T), 0)
    jrow8 = lax.broadcasted_iota(jnp.int32, (8, T), 0)
    tblk0 = lax.broadcasted_iota(jnp.int32, (32, T), 1)

    def scores(i, buf):
        rows = pl.ds(pl.multiple_of(i * T, T), T)
        for hp in range(NSA_HEADS // 2):
            g = hp // (NSA_HPG // 2)
            qp = q_ref[rows, LANE * hp:LANE * (hp + 1)].astype(F32)
            qs = (jnp.where(lane < 64, qp, 0.0), jnp.where(lane < 64, pltpu.roll(qp, 64, 1), 0.0))
            for par in range(2):
                s_scr[buf, 2 * hp + par] = _nt(kca_ref[g], qs[par].astype(BF16))

    def attend_select(i, buf):
        t0 = i * T
        rows = pl.ds(pl.multiple_of(t0, T), T)
        cmask = (crow * CMP_STRIDE + (CMP_LEN - 1)) <= tcol0 + t0
        live = jnp.where(tlane + t0 >= (CMP_LEN - 1), 1.0, 0.0)
        tblk = (tblk0 + t0) // SLC_LEN
        valid = jrow <= tblk
        forced = (jrow == 0) | (jrow == tblk) | (jrow == tblk - 1)
        bonus = jnp.where(forced, FORCE_BONUS, 0.0)
        for g in range(NSA_GROUPS):
            vt = vct_ref[64 * g:64 * (g + 1), :]
            psum = jnp.zeros((128, T), F32)
            for hp in range(NSA_HPG // 2):
                outs = []
                for par in range(2):
                    h = NSA_HPG * g + 2 * hp + par
                    s = jnp.where(cmask, s_scr[buf, h], NEG)
                    m = jnp.max(s, axis=0, keepdims=True)
                    e = jnp.exp2(s - m)
                    p = e * (live / jnp.sum(e, axis=0, keepdims=True))
                    psum = psum + p
                    outs.append(_nn(vt, p.astype(BF16)) * gnt_ref[i, 3 * h:3 * h + 1, :])
                ocmp_ref[rows, LANE * (2 * g + hp):LANE * (2 * g + hp + 1)] = (
                    jnp.concatenate(outs, axis=0).T.astype(BF16))

            p_hi = psum.astype(BF16)
            p_lo = (psum - p_hi.astype(F32)).astype(BF16)
            pslc = _nn(ovt, p_hi) + _nn(ovt, p_lo)
            pri = jnp.where(valid, pslc + bonus, -1.0)
            rank = [jnp.zeros((8, T), F32) for _ in range(4)]
            for r in range(32):
                row = pri[r:r + 1, :]
                for a in range(4):
                    pa = pri[8 * a:8 * (a + 1), :]
                    if 8 * a > r:
                        ahead = jnp.where(row >= pa, 1.0, 0.0)
                    elif 8 * a + 7 < r:
                        ahead = jnp.where(row > pa, 1.0, 0.0)
                    else:
                        ahead = jnp.where(jrow8 + 8 * a > r, jnp.where(row >= pa, 1.0, 0.0),
                                          jnp.where(row > pa, 1.0, 0.0))
                    rank[a] = rank[a] + ahead
            rank = jnp.concatenate(rank, axis=0)
            selneg = jnp.where(rank < float(SLC_TOPK), 0.0, NEG)
            pad = jnp.concatenate([jnp.zeros((64, T), F32), selneg, jnp.zeros((32, T), F32)], axis=0)
            sel_ref[rows, LANE * g:LANE * (g + 1)] = pad.T.astype(BF16)

    scores(0, 0)

    def body(u, carry):
        scores(2 * u + 1, 1)
        attend_select(2 * u, 0)
        scores(jnp.minimum(2 * u + 2, nq - 1), 0)
        attend_select(2 * u + 1, 1)
        return carry

    lax.fori_loop(0, nq // 2, body, 0)


def _cmpsel(qraw, kca, vct, gnt, nb, seq):
    n = qraw.shape[0]
    nq = seq // T
    assert nq % 2 == 0
    return pl.pallas_call(
        _cmpsel_kernel,
        out_shape=(jax.ShapeDtypeStruct((n, 512), BF16), jax.ShapeDtypeStruct((n, 256), BF16)),
        grid=(nb,),
        in_specs=[pl.BlockSpec((seq, 512), lambda b: (b, 0)),
                  pl.BlockSpec((None, 2, 128, LANE), lambda b: (b, 0, 0, 0)),
                  pl.BlockSpec((None, LANE, 128), lambda b: (b, 0, 0)),
                  pl.BlockSpec((nq, GATE_ROWS, T), lambda b: (b, 0, 0))],
        out_specs=(pl.BlockSpec((seq, 512), lambda b: (b, 0)),
                   pl.BlockSpec((seq, 256), lambda b: (b, 0))),
        scratch_shapes=[pltpu.VMEM((2, NSA_HEADS, 128, T), F32)],
        compiler_params=_params(1, VMEM_LIMIT),
        name="cmpsel",
    )(qraw, kca, vct, gnt)


def _split_heads(q_ref, extras, qs_scr, base=0):
    lane = lax.broadcasted_iota(jnp.int32, (T, LANE), 1)
    for hp in range(NSA_HEADS // 2):
        extra = extras[hp // (NSA_HPG // 2)]
        qp = q_ref[:, LANE * hp:LANE * (hp + 1)].astype(F32)
        qs_scr[base + 2 * hp] = jnp.where(lane < 64, qp, extra).astype(BF16)
        qs_scr[base + 2 * hp + 1] = jnp.where(lane < 64, pltpu.roll(qp, 64, 1), extra).astype(BF16)


SM_ROWS = 32


RING = 4


def _tile_schedule(nq):
    below = [(i, j) for j in range(nq) for i in range(j + 1, nq)]
    assert len(below) == (RING - 1) * nq + RING
    pairs = []
    for i in range(nq):
        pairs += [(i, i)] + below[(RING - 1) * i:(RING - 1) * (i + 1)]
    pairs += below[(RING - 1) * nq:]
    return (jnp.asarray([p[0] for p in pairs], jnp.int32), jnp.asarray([p[1] for p in pairs], jnp.int32))


def _flash_scratch(nq, nc, acc_rows):
    return [pltpu.VMEM((nq * nc, T, LANE), BF16),
            pltpu.VMEM((RING, nc, T, T), F32),
            pltpu.VMEM((RING, nc, 1, T), F32),
            pltpu.VMEM((nq * nc, 1, T), F32),
            pltpu.VMEM((nq * nc, 1, T), F32),
            pltpu.VMEM((nq * acc_rows // LANE, LANE, T), F32)]


def _causal_flash(it_ref, jt_ref, nq, k_ref, vt_ref, qs_scr, s_scr, mx_scr, m_scr, l_scr, acc_scr, chains, acc_of):
    nc = len(chains)
    nsets = it_ref.shape[0]
    assert nsets % RING == 0 and nsets >= 2 * RING
    m_scr[...] = jnp.full(m_scr.shape, NEG, F32)
    l_scr[...] = jnp.zeros(l_scr.shape, F32)
    acc_scr[...] = jnp.zeros(acc_scr.shape, F32)
    krow = lax.broadcasted_iota(jnp.int32, (T, T), 0)
    qcol = lax.broadcasted_iota(jnp.int32, (T, T), 1)
    ones = jnp.ones((16, T), BF16)

    def scores(n, buf):
        i, j = it_ref[n], jt_ref[n]
        if buf == 0:
            mask = krow <= qcol + jnp.where(i == j, 0, T)
        for c, (kl, _, _) in enumerate(chains):
            s = _nt(k_ref[j, :, kl:kl + LANE], qs_scr[i * nc + c])
            if buf == 0:
                s = jnp.where(mask, s, NEG)
            s_scr[buf, c] = s
            mx_scr[buf, c] = jnp.max(s, axis=0, keepdims=True)

    def softmax_pv(n, buf):
        i, j = it_ref[n], jt_ref[n]
        for c, (_, vr, dv) in enumerate(chains):
            m_prev = m_scr[i * nc + c]
            m_new = jnp.maximum(m_prev, mx_scr[buf, c])
            alpha = jnp.exp2(m_prev - m_new)
            parts = [jnp.exp2(s_scr[buf, c, r:r + SM_ROWS, :] - m_new).astype(BF16)
                     for r in range(0, T, SM_ROWS)]
            vt = jnp.concatenate([vt_ref[j, vr:vr + dv, :], ones], axis=0)
            o = _nn(vt, jnp.concatenate(parts, axis=0))
            l_scr[i * nc + c] = alpha * l_scr[i * nc + c] + o[dv:dv + 1]
            m_scr[i * nc + c] = m_new
            acc = acc_of(i, c)
            acc[...] = acc[...] * alpha + o[0:dv]

    scores(0, 0)

    def body(u, carry):
        n = RING * u
        for r in range(RING):
            scores(n + r + 1, (r + 1) % RING)
            softmax_pv(n + r, r)
        return carry

    lax.fori_loop(0, nsets // RING - 1, body, 0)
    n = nsets - RING
    for r in range(RING):
        if r + 1 < RING:
            scores(n + r + 1, r + 1)
        softmax_pv(n + r, r)


def _slc_kernel(it_ref, jt_ref, q_ref, sel_ref, k_ref, vt_ref, gnt_ref, o_ref,
                qs_scr, s_scr, mx_scr, m_scr, l_scr, acc_scr):
    nq = q_ref.shape[0] // T
    nh = NSA_HEADS

    def prep(i, carry):
        rows = pl.ds(pl.multiple_of(i * T, T), T)
        sel = sel_ref[rows, :].astype(F32)
        _split_heads(q_ref.at[rows, :], [sel[:, 0:LANE], sel[:, LANE:2 * LANE]], qs_scr, i * nh)
        return carry

    lax.fori_loop(0, nq, prep, 0)
    chains = [(LANE * (h // NSA_HPG), 64 * (h // NSA_HPG), 64) for h in range(nh)]
    acc_of = lambda i, h: acc_scr.at[i * (nh // 2) + h // 2, 64 * (h % 2):64 * (h % 2 + 1), :]
    _causal_flash(it_ref, jt_ref, nq, k_ref, vt_ref, qs_scr, s_scr, mx_scr, m_scr, l_scr, acc_scr, chains, acc_of)

    def finish(i, carry):
        rows = pl.ds(pl.multiple_of(i * T, T), T)
        for hp in range(nh // 2):
            scale = [gnt_ref[i, 3 * h + 1:3 * h + 2, :] / l_scr[i * nh + h] for h in (2 * hp, 2 * hp + 1)]
            inv = jnp.concatenate([jnp.broadcast_to(scale[0], (64, T)), jnp.broadcast_to(scale[1], (64, T))],
                                  axis=0)
            o_ref[rows, LANE * hp:LANE * (hp + 1)] = (acc_scr[i * (nh // 2) + hp] * inv).T.astype(BF16)
        return carry

    lax.fori_loop(0, nq, finish, 0)


def _slc(qrot, sel, ksa3, vst, gnt3, nb, seq):
    n = qrot.shape[0]
    nq = seq // T
    it, jt = _tile_schedule(nq)
    grid_spec = pltpu.PrefetchScalarGridSpec(
        num_scalar_prefetch=2,
        grid=(nb,),
        in_specs=[pl.BlockSpec((seq, 512), lambda b, it, jt: (b, 0)),
                  pl.BlockSpec((seq, 256), lambda b, it, jt: (b, 0)),
                  pl.BlockSpec((nq, T, 256), lambda b, it, jt: (b, 0, 0)),
                  pl.BlockSpec((nq, LANE, T), lambda b, it, jt: (b, 0, 0)),
                  pl.BlockSpec((nq, GATE_ROWS, T), lambda b, it, jt: (b, 0, 0))],
        out_specs=pl.BlockSpec((seq, 512), lambda b, it, jt: (b, 0)),
        scratch_shapes=_flash_scratch(nq, NSA_HEADS, NSA_Q))
    return pl.pallas_call(
        _slc_kernel,
        out_shape=jax.ShapeDtypeStruct((n, 512), BF16),
        grid_spec=grid_spec,
        compiler_params=_params(1, VMEM_LIMIT),
        name="slc",
    )(it, jt, qrot, sel, ksa3, vst, gnt3)


def _win_kernel(q_ref, k_ref, vt_ref, gnt_ref, o_ref, qs_scr, s_scr, mx_scr, acc_scr):
    i = pl.program_id(1)
    _split_heads(q_ref, [0.0, 0.0], qs_scr)
    krow = lax.broadcasted_iota(jnp.int32, (T, T), 0)
    qcol = lax.broadcasted_iota(jnp.int32, (T, T), 1)
    span = WINDOW // T
    masks = {0: krow <= qcol, span: (krow > qcol) & (i >= span)}
    ones = jnp.ones((16, T), BF16)

    def scores(hp, buf):
        for par in range(2):
            h = 2 * hp + par
            g = h // NSA_HPG
            mx = None
            for d in range(span + 1):
                j = i - d
                s = _nt(k_ref[jnp.maximum(j, 0), :, LANE * g:LANE * (g + 1)], qs_scr[h])
                if d in masks:
                    s = jnp.where(masks[d], s, NEG)
                else:
                    s = s + jnp.where(j >= 0, 0.0, NEG)
                s_scr[buf, par, d] = s
                md = jnp.max(s, axis=0, keepdims=True)
                mx = md if mx is None else jnp.maximum(mx, md)
            mx_scr[buf, par] = mx

    def softmax_pv(hp, buf):
        for par in range(2):
            g = (2 * hp + par) // NSA_HPG
            m = mx_scr[buf, par]
            acc = jnp.zeros((64 + 16, T), F32)
            for d in range(span + 1):
                parts = [jnp.exp2(s_scr[buf, par, d, r:r + SM_ROWS, :] - m).astype(BF16)
                         for r in range(0, T, SM_ROWS)]
                vt = jnp.concatenate([vt_ref[jnp.maximum(i - d, 0), 64 * g:64 * (g + 1), :], ones], axis=0)
                acc = acc + _nn(vt, jnp.concatenate(parts, axis=0))
            gate = gnt_ref[3 * (2 * hp + par) + 2:3 * (2 * hp + par) + 3, :]
            acc_scr[hp, 64 * par:64 * (par + 1), :] = acc[0:64] * (gate / acc[64:65])

    npairs = NSA_HEADS // 2
    scores(0, 0)
    for hp in range(npairs):
        if hp + 1 < npairs:
            scores(hp + 1, (hp + 1) % 2)
        softmax_pv(hp, hp % 2)
    for hp in range(npairs):
        o_ref[:, LANE * hp:LANE * (hp + 1)] = acc_scr[hp].T.astype(BF16)


def _win(qrot, kwa3, vwt, gnt, nb, seq):
    n = qrot.shape[0]
    nq = seq // T
    return pl.pallas_call(
        _win_kernel,
        out_shape=jax.ShapeDtypeStruct((n, 512), BF16),
        grid=(nb, nq),
        in_specs=[pl.BlockSpec((T, 512), lambda b, i: (b * nq + i, 0)),
                  pl.BlockSpec((nq, T, 256), lambda b, i: (b, 0, 0)),
                  pl.BlockSpec((nq, LANE, T), lambda b, i: (b, 0, 0)),
                  pl.BlockSpec((None, GATE_ROWS, T), lambda b, i: (b * nq + i, 0, 0))],
        out_specs=pl.BlockSpec((T, 512), lambda b, i: (b * nq + i, 0)),
        scratch_shapes=[pltpu.VMEM((NSA_HEADS, T, LANE), BF16),
                        pltpu.VMEM((2, 2, WINDOW // T + 1, T, T), F32),
                        pltpu.VMEM((2, 2, 1, T), F32),
                        pltpu.VMEM((NSA_HEADS // 2, LANE, T), F32)],
        compiler_params=_params(2, VMEM_LIMIT),
        name="win",
    )(qrot, kwa3, vwt, gnt)


def _diff_kernel(it_ref, jt_ref, q_ref, k_ref, vt_ref, lam_ref, g_ref, o_ref,
                 qs_scr, s_scr, mx_scr, m_scr, l_scr, acc_scr):
    nq = q_ref.shape[0] // T
    nc = 2 * DIFF_HEADS
    lane = lax.broadcasted_iota(jnp.int32, (T, LANE), 1)

    def prep(i, carry):
        rows = pl.ds(pl.multiple_of(i * T, T), T)
        for h in range(DIFF_HEADS):
            q = q_ref[rows, LANE * h:LANE * (h + 1)].astype(F32)
            qs_scr[i * nc + 2 * h] = jnp.where(lane < 64, q, 0.0).astype(BF16)
            qs_scr[i * nc + 2 * h + 1] = jnp.where(lane < 64, 0.0, q).astype(BF16)
        return carry

    lax.fori_loop(0, nq, prep, 0)
    chains = [(LANE * (c // 2), LANE * (c // 2), LANE) for c in range(nc)]
    _causal_flash(it_ref, jt_ref, nq, k_ref, vt_ref, qs_scr, s_scr, mx_scr, m_scr, l_scr, acc_scr, chains,
                  lambda i, c: acc_scr.at[i * nc + c])

    lp = lam_ref[...]
    lam = (jnp.exp(jnp.sum(lp[0:1] * lp[1:2], axis=1, keepdims=True))
           - jnp.exp(jnp.sum(lp[2:3] * lp[3:4], axis=1, keepdims=True)) + LAMBDA_INIT)

    def finish(i, carry):
        rows = pl.ds(pl.multiple_of(i * T, T), T)
        for h in range(DIFF_HEADS):
            c = i * nc + 2 * h
            o = acc_scr[c] * (1.0 / l_scr[c]) - lam * (acc_scr[c + 1] * (1.0 / l_scr[c + 1]))
            o = o * lax.rsqrt(jnp.mean(o * o, axis=0, keepdims=True) + RMS_EPS)
            o_ref[rows, LANE * h:LANE * (h + 1)] = ((o.T * g_ref[...]) * (1.0 - LAMBDA_INIT)).astype(BF16)
        return carry

    lax.fori_loop(0, nq, finish, 0)


def _diff(qdf, kdf3, vdft, lam, norm_g, nb, seq):
    n = qdf.shape[0]
    nq = seq // T
    it, jt = _tile_schedule(nq)
    grid_spec = pltpu.PrefetchScalarGridSpec(
        num_scalar_prefetch=2,
        grid=(nb,),
        in_specs=[pl.BlockSpec((seq, 512), lambda b, it, jt: (b, 0)),
                  pl.BlockSpec((nq, T, 512), lambda b, it, jt: (b, 0, 0)),
                  pl.BlockSpec((nq, 512, T), lambda b, it, jt: (b, 0, 0)),
                  pl.BlockSpec((4, HEAD_DIM), lambda b, it, jt: (0, 0)),
                  pl.BlockSpec((1, LANE), lambda b, it, jt: (0, 0))],
        out_specs=pl.BlockSpec((seq, 512), lambda b, it, jt: (b, 0)),
        scratch_shapes=_flash_scratch(nq, 2 * DIFF_HEADS, 2 * DIFF_HEADS * DIFF_V_DIM))
    return pl.pallas_call(
        _diff_kernel,
        out_shape=jax.ShapeDtypeStruct((n, 512), BF16),
        grid_spec=grid_spec,
        compiler_params=_params(1, VMEM_LIMIT),
        name="diff",
    )(it, jt, qdf, kdf3, vdft, lam, norm_g)


def _layer_norm(z, g, b):
    mu = jnp.mean(z, axis=-1, keepdims=True)
    zc = z - mu
    var = jnp.mean(zc * zc, axis=-1, keepdims=True)
    return zc * lax.rsqrt(var + LN_EPS) * g + b


def _merge_kernel(x_ref, ocmp_ref, oslc_ref, owin_ref, yb_ref, gm_ref,
                  wa_ref, wb_ref, wo_ref, g_ref, b_ref, h_ref):
    ya = ocmp_ref[...].astype(F32) + oslc_ref[...].astype(F32) + owin_ref[...].astype(F32)
    ta = _nn(ya.astype(BF16), wa_ref[...])
    tb = _nn(yb_ref[...], wb_ref[...])
    merged = gm_ref[:, 0:D_MODEL].astype(F32) * ta + gm_ref[:, D_MODEL:2 * D_MODEL].astype(F32) * tb
    mix = _nn(merged.astype(BF16), wo_ref[...])
    h_ref[...] = _layer_norm(DEEPNORM_ALPHA * x_ref[...] + mix, g_ref[...], b_ref[...])


def _merge(x2, ocmp, oslc, owin, yb, gm, wa, wb, wo, g, b):
    n = x2.shape[0]
    row = lambda w: pl.BlockSpec((TM, w), lambda i: (i, 0))
    full = lambda a: pl.BlockSpec(a.shape, lambda i: (0,) * a.ndim)
    return pl.pallas_call(
        _merge_kernel,
        out_shape=jax.ShapeDtypeStruct((n, D_MODEL), F32),
        grid=(n // TM,),
        in_specs=[row(D_MODEL), row(512), row(512), row(512), row(512), row(2048),
                  full(wa), full(wb), full(wo), full(g), full(b)],
        out_specs=row(D_MODEL),
        compiler_params=_params(1, VMEM_LIMIT),
        name="merge",
    )(x2, ocmp, oslc, owin, yb, gm, wa, wb, wo, g, b)


def _ffn_kernel(h_ref, wgu_ref, wd_ref, g_ref, b_ref, o_ref):
    h = h_ref[...]
    hb = h.astype(BF16)
    gate = _nn(hb, wgu_ref[:, 0:FFN_HIDDEN])
    up = _nn(hb, wgu_ref[:, FFN_HIDDEN:2 * FFN_HIDDEN])
    act = (gate * jax.nn.sigmoid(gate) * up).astype(BF16)
    y = _nn(act, wd_ref[...])
    o_ref[...] = _layer_norm(DEEPNORM_ALPHA * h + y, g_ref[...], b_ref[...])


def _ffn(h1, wgu, wd, g, b):
    n = h1.shape[0]
    row = pl.BlockSpec((TM, D_MODEL), lambda i: (i, 0))
    full = lambda a: pl.BlockSpec(a.shape, lambda i: (0,) * a.ndim, pipeline_mode=pl.Buffered(1))
    return pl.pallas_call(
        _ffn_kernel,
        out_shape=jax.ShapeDtypeStruct((n, D_MODEL), F32),
        grid=(n // TM,),
        in_specs=[row, full(wgu), full(wd), full(g), full(b)],
        out_specs=row,
        compiler_params=_params(1, VMEM_LIMIT),
        name="ffn",
    )(h1, wgu, wd, g, b)


def _rope_tables(seq):
    half = ROPE_DIM // 2
    inv_freq = ROPE_THETA ** (-jnp.arange(half, dtype=F32) * 2.0 / ROPE_DIM)
    ang = jnp.arange(seq, dtype=F32)[:, None] * inv_freq[None, :]
    cos, sin = jnp.cos(ang), jnp.sin(ang)
    ones = jnp.ones((seq, HEAD_DIM - ROPE_DIM), F32)
    zeros8 = jnp.zeros((seq, half), F32)
    zeros48 = jnp.zeros((seq, HEAD_DIM - ROPE_DIM), F32)
    c64 = jnp.concatenate([cos, cos, ones], axis=1)
    sa64 = jnp.concatenate([-sin, zeros8, zeros48], axis=1)
    sb64 = jnp.concatenate([zeros8, sin, zeros48], axis=1)
    rep = lambda a: jnp.concatenate([a, a], axis=1)
    pos_blk = jnp.arange(seq, dtype=jnp.int32)[:, None] // SLC_LEN
    lane = jnp.arange(LANE, dtype=jnp.int32)[None, :]
    onehot = ((lane >= 64) & (lane < 96) & (lane - 64 == pos_blk)).astype(F32)
    return rep(c64), rep(sa64), rep(sb64), onehot


def _prep_w_in(w):
    offs = np.cumsum((0,) + IN_WIDTHS)
    q, kc, vc, ks, vs, kw, vw, gn, qd, kd, vd, gm = [w[:, offs[i]:offs[i + 1]] for i in range(12)]
    z64 = jnp.zeros((w.shape[0], 64), w.dtype)
    ksa = jnp.concatenate([ks[:, :64], z64, ks[:, 64:], z64], axis=1)
    kwa = jnp.concatenate([kw[:, :64], z64, kw[:, 64:], z64], axis=1)
    gnp = jnp.pad(gn, ((0, 0), (0, LANE - gn.shape[1])))
    return jnp.concatenate([q, kc, vc, ksa, vs, kwa, vw, qd, kd, vd, gm, gnp], axis=1).astype(BF16)


def _prep_compress(cmp_pe, cmp_w1, cmp_b1, cmp_w2):
    half = CMP_LEN // 2
    pet, peb, wt, wb, b1 = [], [], [], [], []
    for kv in range(2):
        pe = cmp_pe[kv]
        tile2 = lambda a: jnp.concatenate([a, a], axis=1).reshape(1, half * 2 * HEAD_DIM)
        pet.append(tile2(pe[:half]))
        peb.append(tile2(pe[half:]))
        w1 = cmp_w1[kv].reshape(CMP_LEN, HEAD_DIM, CMP_HIDDEN)
        z = jnp.zeros((half, HEAD_DIM, CMP_HIDDEN), w1.dtype)

        def spread(wh):
            g0 = jnp.concatenate([wh, z], axis=2)
            g1 = jnp.concatenate([z, wh], axis=2)
            return jnp.stack([g0, g1], axis=1).reshape(half * 2 * HEAD_DIM, 2 * CMP_HIDDEN)

        wt.append(spread(w1[:half]))
        wb.append(spread(w1[half:]))
        b1.append(jnp.concatenate([cmp_b1[kv], cmp_b1[kv]])[None, :])
    w2k, w2v = cmp_w2[0], cmp_w2[1]
    zk = jnp.zeros_like(w2k)
    w2k_g = jnp.stack([
        jnp.concatenate([jnp.concatenate([w2k, zk], axis=1), jnp.zeros((CMP_HIDDEN, LANE), w2k.dtype)], axis=0),
        jnp.concatenate([jnp.zeros((CMP_HIDDEN, LANE), w2k.dtype), jnp.concatenate([w2k, zk], axis=1)], axis=0),
    ])
    zv = jnp.zeros_like(w2v)
    w2v_bd = jnp.concatenate([jnp.concatenate([w2v, zv], axis=1),
                              jnp.concatenate([zv, w2v], axis=1)], axis=0)
    st = lambda xs: jnp.stack(xs)
    return (st(pet), st(peb), st(wt).astype(BF16), st(wb).astype(BF16), st(b1),
            w2k_g.astype(BF16), w2v_bd.astype(BF16))


def kernel(x, w_in, cmp_pe, cmp_w1, cmp_b1, cmp_w2, diff_lambda, diff_norm_g, w_branch_a, w_branch_b,
           w_o, ln1_g, ln1_b, w_gate_up, w_down, ln2_g, ln2_b):
    nb, seq, d = x.shape
    assert d == D_MODEL and seq % T == 0 and seq // SLC_LEN == 32 and seq // CMP_STRIDE == 128
    n = nb * seq
    x2 = x.reshape(n, d)
    cos_t, sa_t, sb_t, oh_t = _rope_tables(seq)
    (qraw, qrot, kcs, vcs, ksa, vst, kwa, vwt, qdf, kdf, vdft, gm, gnt) = _inproj(
        x2, _prep_w_in(w_in[0]), cos_t, sa_t, sb_t, oh_t, seq)

    pet, peb, wt, wb, b1, w2k, w2v = _prep_compress(cmp_pe[0], cmp_w1[0], cmp_b1[0], cmp_w2[0])
    kca, vct = _compress(kcs, vcs, pet, peb, wt, wb, b1, w2k, w2v, nb, seq)

    ocmp, sel = _cmpsel(qraw, kca, vct, gnt, nb, seq)
    nt = n // T
    oslc = _slc(qrot, sel, ksa.reshape(nt, T, 256), vst, gnt, nb, seq)
    owin = _win(qrot, kwa.reshape(nt, T, 256), vwt, gnt, nb, seq)
    yb = _diff(qdf, kdf.reshape(nt, T, 512), vdft, diff_lambda[0], diff_norm_g[0][None, :], nb, seq)

    h1 = _merge(x2, ocmp, oslc, owin, yb, gm,
                w_branch_a[0].astype(BF16), w_branch_b[0].astype(BF16), w_o[0].astype(BF16),
                ln1_g[0][None, :], ln1_b[0][None, :])
    out = _ffn(h1, w_gate_up[0].astype(BF16), w_down[0].astype(BF16), ln2_g[0][None, :], ln2_b[0][None, :])
    return out.reshape(nb, seq, d)
```

```python
import functools
import math

import numpy as np
import jax
import jax.numpy as jnp
from jax import lax
from jax.experimental import pallas as pl
from jax.experimental.pallas import tpu as pltpu

D_MODEL = 1024
HEAD_DIM = 64
ROPE_DIM = HEAD_DIM // 4
ROPE_THETA = 500000.0
NSA_HEADS = 8
NSA_GROUPS = 2
NSA_HPG = NSA_HEADS // NSA_GROUPS
CMP_LEN = 32
CMP_STRIDE = 16
CMP_HIDDEN = 256
SLC_LEN = 64
SLC_TOPK = 8
WINDOW = 512
FORCE_BONUS = 1.0e4
DIFF_HEADS = 4
DIFF_V_DIM = 2 * HEAD_DIM
FFN_HIDDEN = ((8 * D_MODEL // 3 + 255) // 256) * 256
DEPTH = 1
DEEPNORM_ALPHA = (2 * DEPTH) ** 0.25
NEG = -1.0e30
LN_EPS = 1e-5
RMS_EPS = 1e-5
LAMBDA_INIT = 0.8 - 0.6 * math.exp(-0.3 * 0)
QK_SCALE = HEAD_DIM ** -0.5
QK_SCALE_LOG2 = QK_SCALE * math.log2(math.e)

NSA_Q = NSA_HEADS * HEAD_DIM
NSA_KV = NSA_GROUPS * HEAD_DIM
DIFF_QK = DIFF_HEADS * 2 * HEAD_DIM
DIFF_V = DIFF_HEADS * DIFF_V_DIM
IN_WIDTHS = (NSA_Q, NSA_KV, NSA_KV, NSA_KV, NSA_KV, NSA_KV, NSA_KV, 3 * NSA_HEADS,
             DIFF_QK, DIFF_QK, DIFF_V, 2 * D_MODEL)

LANE = 128
T = 256
TM = 512
VMEM_LIMIT = 56 * 1024 * 1024
GATE_ROWS = 32

BF16 = jnp.bfloat16
F32 = jnp.float32

_C_Q, _C_KC, _C_VC, _C_KS, _C_VS, _C_KW, _C_VW = 0, 512, 640, 768, 896, 1024, 1152
_C_QD, _C_KD, _C_VD, _C_GM, _C_GN, _C_END = 1280, 1792, 2304, 2816, 4864, 4992


def _nt(a, b):
    return lax.dot_general(a, b, (((1,), (1,)), ((), ())), preferred_element_type=F32)


def _nn(a, b):
    return jnp.dot(a, b, preferred_element_type=F32)


def _params(n_axes, vmem=None):
    return pltpu.CompilerParams(dimension_semantics=("arbitrary",) * n_axes,
                                vmem_limit_bytes=vmem)


def _inproj_kernel(x_ref, wlo_ref, whi_ref, wgn_ref, cos_ref, sa_ref, sb_ref, oh_ref,
                   qraw_ref, qrot_ref, kc_ref, vc_ref, ksa_ref, vst_ref, kwa_ref, vwt_ref,
                   qdf_ref, kdf_ref, vdft_ref, gm_ref, gn_ref):
    xb = x_ref[...].astype(BF16)
    cos = cos_ref[...]
    sa = sa_ref[...]
    sb = sb_ref[...]

    def mm(c0, n):
        for w_ref, base in ((wlo_ref, 0), (whi_ref, _C_QD), (wgn_ref, _C_GN)):
            if base <= c0 and c0 + n <= base + w_ref.shape[1]:
                return _nn(xb, w_ref[:, c0 - base:c0 - base + n])
        raise ValueError((c0, n))

    def rope(t):
        return t * cos + pltpu.roll(t, LANE - 8, 1) * sa + pltpu.roll(t, 8, 1) * sb

    t = mm(_C_Q, 512)
    for j in range(4):
        tj = t[:, LANE * j:LANE * (j + 1)]
        qraw_ref[:, LANE * j:LANE * (j + 1)] = (tj * QK_SCALE_LOG2).astype(BF16)
        qrot_ref[:, LANE * j:LANE * (j + 1)] = (rope(tj) * QK_SCALE_LOG2).astype(BF16)

    t = mm(_C_KC, 256)
    kc_ref[...] = t[:, :LANE]
    vc_ref[...] = t[:, LANE:]

    lane = lax.broadcasted_iota(jnp.int32, (T, LANE), 1)

    def per_group(k, extra, out_ref):
        out_ref[:, 0:LANE] = jnp.where(lane < 64, k, extra).astype(BF16)
        out_ref[:, LANE:2 * LANE] = jnp.where(lane < 64, pltpu.roll(k, 64, 1), extra).astype(BF16)

    t = mm(_C_KS, 256)
    per_group(rope(t[:, :LANE]), oh_ref[...], ksa_ref)
    vst_ref[...] = t[:, LANE:].T.astype(BF16)

    t = mm(_C_KW, 256)
    per_group(rope(t[:, :LANE]), 0.0, kwa_ref)
    vwt_ref[...] = t[:, LANE:].T.astype(BF16)

    t = mm(_C_QD, 512)
    for j in range(4):
        qdf_ref[:, LANE * j:LANE * (j + 1)] = (rope(t[:, LANE * j:LANE * (j + 1)]) * QK_SCALE_LOG2).astype(BF16)
    t = mm(_C_KD, 512)
    for j in range(4):
        kdf_ref[:, LANE * j:LANE * (j + 1)] = rope(t[:, LANE * j:LANE * (j + 1)]).astype(BF16)
    t = mm(_C_VD, 512)
    for j in range(4):
        vdft_ref[LANE * j:LANE * (j + 1), :] = t[:, LANE * j:LANE * (j + 1)].T.astype(BF16)

    for j in range(4):
        t = mm(_C_GM + 512 * j, 512)
        gm_ref[:, 512 * j:512 * (j + 1)] = jax.nn.sigmoid(t).astype(BF16)

    gn_ref[...] = jax.nn.sigmoid(mm(_C_GN, 128)).T[0:GATE_ROWS]


def _inproj(x2, w_lo, w_hi, w_gn, cos_t, sa_t, sb_t, oh_t, seq):
    n = x2.shape[0]
    nt = n // T
    spt = seq // T
    assert w_lo.shape[1] == _C_QD and w_hi.shape[1] == _C_GN - _C_QD and w_gn.shape[1] == _C_END - _C_GN
    row = lambda w: pl.BlockSpec((T, w), lambda i: (i, 0))
    whole = lambda a: pl.BlockSpec(a.shape, lambda i: (0, 0), pipeline_mode=pl.Buffered(1))
    tab = pl.BlockSpec((T, LANE), lambda i: (i % spt, 0))
    tile_t = lambda r: pl.BlockSpec((None, r, T), lambda i: (i, 0, 0))
    out_shape = (
        jax.ShapeDtypeStruct((n, 512), BF16),
        jax.ShapeDtypeStruct((n, 512), BF16),
        jax.ShapeDtypeStruct((n, LANE), F32),
        jax.ShapeDtypeStruct((n, LANE), F32),
        jax.ShapeDtypeStruct((n, 256), BF16),
        jax.ShapeDtypeStruct((nt, LANE, T), BF16),
        jax.ShapeDtypeStruct((n, 256), BF16),
        jax.ShapeDtypeStruct((nt, LANE, T), BF16),
        jax.ShapeDtypeStruct((n, 512), BF16),
        jax.ShapeDtypeStruct((n, 512), BF16),
        jax.ShapeDtypeStruct((nt, 512, T), BF16),
        jax.ShapeDtypeStruct((n, 2048), BF16),
        jax.ShapeDtypeStruct((nt, GATE_ROWS, T), F32),
    )
    out_specs = (row(512), row(512), row(LANE), row(LANE), row(256), tile_t(LANE), row(256),
                 tile_t(LANE), row(512), row(512), tile_t(512), row(2048), tile_t(GATE_ROWS))
    return pl.pallas_call(
        _inproj_kernel,
        out_shape=out_shape,
        grid=(nt,),
        in_specs=[row(D_MODEL), whole(w_lo), whole(w_hi), whole(w_gn), tab, tab, tab, tab],
        out_specs=out_specs,
        compiler_params=_params(1, VMEM_LIMIT),
        name="inproj",
    )(x2, w_lo, w_hi, w_gn, cos_t, sa_t, sb_t, oh_t)


def _gelu_tanh(x):
    c = math.sqrt(2.0 / math.pi)
    return x * (0.5 * (1.0 + jnp.tanh(c * (x + 0.044715 * (x * x * x)))))


def _compress_kernel(kf_ref, vf_ref, pet_ref, peb_ref, wt_ref, wb_ref, b1_ref, w2k_ref, w2v_ref,
                     kca_ref, vct_ref):
    nchunk = kf_ref.shape[0] // CMP_STRIDE

    def hidden(x_ref, kv):
        a = jnp.zeros((nchunk, 2 * CMP_HIDDEN), F32)
        b = jnp.zeros((nchunk, 2 * CMP_HIDDEN), F32)
        for p in range(0, CMP_STRIDE, 2):
            x = jnp.concatenate([x_ref[pl.ds(p, nchunk, stride=CMP_STRIDE), :],
                                 x_ref[pl.ds(p + 1, nchunk, stride=CMP_STRIDE), :]], axis=1)
            c0, c1 = LANE * p, LANE * (p + 2)
            a = a + _nn((x + pet_ref[kv, :, c0:c1]).astype(BF16), wt_ref[kv, c0:c1, :])
            b = b + _nn((x + peb_ref[kv, :, c0:c1]).astype(BF16), wb_ref[kv, c0:c1, :])
        h = a + pltpu.roll(b, nchunk - 1, 0) + b1_ref[kv]
        return _gelu_tanh(h).astype(BF16)

    hk = hidden(kf_ref, 0)
    for g in range(2):
        kca_ref[g] = _nn(hk, w2k_ref[g]).astype(BF16)
    hv = hidden(vf_ref, 1)
    vct_ref[...] = _nn(hv, w2v_ref[...]).T.astype(BF16)


def _compress(kf, vf, pet, peb, wt, wb, b1, w2k, w2v, nb, seq):
    full = lambda a: pl.BlockSpec(a.shape, lambda b: (0,) * a.ndim)
    return pl.pallas_call(
        _compress_kernel,
        out_shape=(jax.ShapeDtypeStruct((nb, 2, 128, LANE), BF16),
                   jax.ShapeDtypeStruct((nb, LANE, 128), BF16)),
        grid=(nb,),
        in_specs=[pl.BlockSpec((seq, LANE), lambda b: (b, 0)),
                  pl.BlockSpec((seq, LANE), lambda b: (b, 0)),
                  full(pet), full(peb), full(wt), full(wb), full(b1), full(w2k), full(w2v)],
        out_specs=(pl.BlockSpec((None, 2, 128, LANE), lambda b: (b, 0, 0, 0)),
                   pl.BlockSpec((None, LANE, 128), lambda b: (b, 0, 0))),
        compiler_params=_params(1, VMEM_LIMIT),
        name="compress",
    )(kf, vf, pet, peb, wt, wb, b1, w2k, w2v)


def _cmpsel_kernel(q_ref, kca_ref, vct_ref, gnt_ref, ocmp_ref, sel_ref, s_scr):
    nq = q_ref.shape[0] // T
    lane = lax.broadcasted_iota(jnp.int32, (T, LANE), 1)
    crow = lax.broadcasted_iota(jnp.int32, (128, T), 0)
    tcol0 = lax.broadcasted_iota(jnp.int32, (128, T), 1)
    tlane = lax.broadcasted_iota(jnp.int32, (1, T), 1)

    jj = lax.broadcasted_iota(jnp.int32, (32, 128), 0) * SLC_LEN
    cc = lax.broadcasted_iota(jnp.int32, (32, 128), 1) * CMP_STRIDE
    ov = jnp.maximum(jnp.minimum(cc + CMP_LEN, jj + SLC_LEN) - jnp.maximum(cc, jj), 0)
    ovt = (ov.astype(F32) * (1.0 / CMP_LEN)).astype(BF16)

    jrow = lax.broadcasted_iota(jnp.int32, (32, T), 0)
    jrow8 = lax.broadcasted_iota(jnp.int32, (8, T), 0)
    tblk0 = lax.broadcasted_iota(jnp.int32, (32, T), 1)

    def scores(i, buf):
        rows = pl.ds(pl.multiple_of(i * T, T), T)
        for hp in range(NSA_HEADS // 2):
            g = hp // (NSA_HPG // 2)
            qp = q_ref[rows, LANE * hp:LANE * (hp + 1)].astype(F32)
            qs = (jnp.where(lane < 64, qp, 0.0), jnp.where(lane < 64, pltpu.roll(qp, 64, 1), 0.0))
            for par in range(2):
                s_scr[buf, 2 * hp + par] = _nt(kca_ref[g], qs[par].astype(BF16))

    def attend_select(i, buf):
        t0 = i * T
        rows = pl.ds(pl.multiple_of(t0, T), T)
        cmask = (crow * CMP_STRIDE + (CMP_LEN - 1)) <= tcol0 + t0
        live = jnp.where(tlane + t0 >= (CMP_LEN - 1), 1.0, 0.0)
        tblk = (tblk0 + t0) // SLC_LEN
        valid = jrow <= tblk
        forced = (jrow == 0) | (jrow == tblk) | (jrow == tblk - 1)
        bonus = jnp.where(forced, FORCE_BONUS, 0.0)
        for g in range(NSA_GROUPS):
            vt = vct_ref[64 * g:64 * (g + 1), :]
            psum = jnp.zeros((128, T), F32)
            for hp in range(NSA_HPG // 2):
                outs = []
                for par in range(2):
                    h = NSA_HPG * g + 2 * hp + par
                    s = jnp.where(cmask, s_scr[buf, h], NEG)
                    m = jnp.max(s, axis=0, keepdims=True)
                    e = jnp.exp2(s - m)
                    p = e * (live / jnp.sum(e, axis=0, keepdims=True))
                    psum = psum + p
                    outs.append(_nn(vt, p.astype(BF16)) * gnt_ref[i, 3 * h:3 * h + 1, :])
                ocmp_ref[rows, LANE * (2 * g + hp):LANE * (2 * g + hp + 1)] = (
                    jnp.concatenate(outs, axis=0).T.astype(BF16))

            p_hi = psum.astype(BF16)
            p_lo = (psum - p_hi.astype(F32)).astype(BF16)
            pslc = _nn(ovt, p_hi) + _nn(ovt, p_lo)
            pri = jnp.where(valid, pslc + bonus, -1.0)
            rank = [jnp.zeros((8, T), F32) for _ in range(4)]
            for r in range(32):
                row = pri[r:r + 1, :]
                for a in range(4):
                    pa = pri[8 * a:8 * (a + 1), :]
                    if 8 * a > r:
                        ahead = jnp.where(row >= pa, 1.0, 0.0)
                    elif 8 * a + 7 < r:
                        ahead = jnp.where(row > pa, 1.0, 0.0)
                    else:
                        ahead = jnp.where(jrow8 + 8 * a > r, jnp.where(row >= pa, 1.0, 0.0),
                                          jnp.where(row > pa, 1.0, 0.0))
                    rank[a] = rank[a] + ahead
            rank = jnp.concatenate(rank, axis=0)
            selneg = jnp.where(rank < float(SLC_TOPK), 0.0, NEG)
            pad = jnp.concatenate([jnp.zeros((64, T), F32), selneg, jnp.zeros((32, T), F32)], axis=0)
            sel_ref[rows, LANE * g:LANE * (g + 1)] = pad.T.astype(BF16)

    scores(0, 0)

    def body(u, carry):
        scores(2 * u + 1, 1)
        attend_select(2 * u, 0)
        scores(jnp.minimum(2 * u + 2, nq - 1), 0)
        attend_select(2 * u + 1, 1)
        return carry

    lax.fori_loop(0, nq // 2, body, 0)


def _cmpsel(qraw, kca, vct, gnt, nb, seq):
    n = qraw.shape[0]
    nq = seq // T
    assert nq % 2 == 0
    return pl.pallas_call(
        _cmpsel_kernel,
        out_shape=(jax.ShapeDtypeStruct((n, 512), BF16), jax.ShapeDtypeStruct((n, 256), BF16)),
        grid=(nb,),
        in_specs=[pl.BlockSpec((seq, 512), lambda b: (b, 0)),
                  pl.BlockSpec((None, 2, 128, LANE), lambda b: (b, 0, 0, 0)),
                  pl.BlockSpec((None, LANE, 128), lambda b: (b, 0, 0)),
                  pl.BlockSpec((nq, GATE_ROWS, T), lambda b: (b, 0, 0))],
        out_specs=(pl.BlockSpec((seq, 512), lambda b: (b, 0)),
                   pl.BlockSpec((seq, 256), lambda b: (b, 0))),
        scratch_shapes=[pltpu.VMEM((2, NSA_HEADS, 128, T), F32)],
        compiler_params=_params(1, VMEM_LIMIT),
        name="cmpsel",
    )(qraw, kca, vct, gnt)


def _split_heads(q_ref, extras, qs_scr, base=0):
    lane = lax.broadcasted_iota(jnp.int32, (T, LANE), 1)
    for hp in range(NSA_HEADS // 2):
        extra = extras[hp // (NSA_HPG // 2)]
        qp = q_ref[:, LANE * hp:LANE * (hp + 1)].astype(F32)
        qs_scr[base + 2 * hp] = jnp.where(lane < 64, qp, extra).astype(BF16)
        qs_scr[base + 2 * hp + 1] = jnp.where(lane < 64, pltpu.roll(qp, 64, 1), extra).astype(BF16)


SM_ROWS = 32


RING = 4


def _tile_schedule(nq):
    below = [(i, j) for j in range(nq) for i in range(j + 1, nq)]
    assert len(below) == (RING - 1) * nq + RING
    pairs = []
    for i in range(nq):
        pairs += [(i, i)] + below[(RING - 1) * i:(RING - 1) * (i + 1)]
    pairs += below[(RING - 1) * nq:]
    return (jnp.asarray([p[0] for p in pairs], jnp.int32), jnp.asarray([p[1] for p in pairs], jnp.int32))


def _flash_scratch(nq, nc, acc_rows):
    return [pltpu.VMEM((nq * nc, T, LANE), BF16),
            pltpu.VMEM((RING, nc, T, T), F32),
            pltpu.VMEM((RING, nc, 1, T), F32),
            pltpu.VMEM((nq * nc, 1, T), F32),
            pltpu.VMEM((nq * nc, 1, T), F32),
            pltpu.VMEM((nq * acc_rows // LANE, LANE, T), F32)]


def _causal_flash(it_ref, jt_ref, nq, k_ref, vt_ref, qs_scr, s_scr, mx_scr, m_scr, l_scr, acc_scr, chains, acc_of):
    nc = len(chains)
    nsets = it_ref.shape[0]
    assert nsets % RING == 0 and nsets >= 2 * RING
    m_scr[...] = jnp.full(m_scr.shape, NEG, F32)
    l_scr[...] = jnp.zeros(l_scr.shape, F32)
    acc_scr[...] = jnp.zeros(acc_scr.shape, F32)
    krow = lax.broadcasted_iota(jnp.int32, (T, T), 0)
    qcol = lax.broadcasted_iota(jnp.int32, (T, T), 1)
    ones = jnp.ones((16, T), BF16)

    def scores(n, buf):
        i, j = it_ref[n], jt_ref[n]
        if buf == 0:
            mask = krow <= qcol + jnp.where(i == j, 0, T)
        for c, (kl, _, _) in enumerate(chains):
            s = _nt(k_ref[j, :, kl:kl + LANE], qs_scr[i * nc + c])
            if buf == 0:
                s = jnp.where(mask, s, NEG)
            s_scr[buf, c] = s
            mx_scr[buf, c] = jnp.max(s, axis=0, keepdims=True)

    def softmax_pv(n, buf):
        i, j = it_ref[n], jt_ref[n]
        for c, (_, vr, dv) in enumerate(chains):
            m_prev = m_scr[i * nc + c]
            m_new = jnp.maximum(m_prev, mx_scr[buf, c])
            alpha = jnp.exp2(m_prev - m_new)
            parts = [jnp.exp2(s_scr[buf, c, r:r + SM_ROWS, :] - m_new).astype(BF16)
                     for r in range(0, T, SM_ROWS)]
            vt = jnp.concatenate([vt_ref[j, vr:vr + dv, :], ones], axis=0)
            o = _nn(vt, jnp.concatenate(parts, axis=0))
            l_scr[i * nc + c] = alpha * l_scr[i * nc + c] + o[dv:dv + 1]
            m_scr[i * nc + c] = m_new
            acc = acc_of(i, c)
            acc[...] = acc[...] * alpha + o[0:dv]

    scores(0, 0)

    def body(u, carry):
        n = RING * u
        for r in range(RING):
            scores(n + r + 1, (r + 1) % RING)
            softmax_pv(n + r, r)
        return carry

    lax.fori_loop(0, nsets // RING - 1, body, 0)
    n = nsets - RING
    for r in range(RING):
        if r + 1 < RING:
            scores(n + r + 1, r + 1)
        softmax_pv(n + r, r)


def _slc_kernel(it_ref, jt_ref, q_ref, sel_ref, k_ref, vt_ref, gnt_ref, o_ref,
                qs_scr, s_scr, mx_scr, m_scr, l_scr, acc_scr):
    nq = q_ref.shape[0] // T
    nh = NSA_HEADS

    def prep(i, carry):
        rows = pl.ds(pl.multiple_of(i * T, T), T)
        sel = sel_ref[rows, :].astype(F32)
        _split_heads(q_ref.at[rows, :], [sel[:, 0:LANE], sel[:, LANE:2 * LANE]], qs_scr, i * nh)
        return carry

    lax.fori_loop(0, nq, prep, 0)
    chains = [(LANE * (h // NSA_HPG), 64 * (h // NSA_HPG), 64) for h in range(nh)]
    acc_of = lambda i, h: acc_scr.at[i * (nh // 2) + h // 2, 64 * (h % 2):64 * (h % 2 + 1), :]
    _causal_flash(it_ref, jt_ref, nq, k_ref, vt_ref, qs_scr, s_scr, mx_scr, m_scr, l_scr, acc_scr, chains, acc_of)

    def finish(i, carry):
        rows = pl.ds(pl.multiple_of(i * T, T), T)
        for hp in range(nh // 2):
            scale = [gnt_ref[i, 3 * h + 1:3 * h + 2, :] / l_scr[i * nh + h] for h in (2 * hp, 2 * hp + 1)]
            inv = jnp.concatenate([jnp.broadcast_to(scale[0], (64, T)), jnp.broadcast_to(scale[1], (64, T))],
                                  axis=0)
            o_ref[rows, LANE * hp:LANE * (hp + 1)] = (acc_scr[i * (nh // 2) + hp] * inv).T.astype(BF16)
        return carry

    lax.fori_loop(0, nq, finish, 0)


def _slc(qrot, sel, ksa3, vst, gnt3, nb, seq):
    n = qrot.shape[0]
    nq = seq // T
    it, jt = _tile_schedule(nq)
    grid_spec = pltpu.PrefetchScalarGridSpec(
        num_scalar_prefetch=2,
        grid=(nb,),
        in_specs=[pl.BlockSpec((seq, 512), lambda b, it, jt: (b, 0)),
                  pl.BlockSpec((seq, 256), lambda b, it, jt: (b, 0)),
                  pl.BlockSpec((nq, T, 256), lambda b, it, jt: (b, 0, 0)),
                  pl.BlockSpec((nq, LANE, T), lambda b, it, jt: (b, 0, 0)),
                  pl.BlockSpec((nq, GATE_ROWS, T), lambda b, it, jt: (b, 0, 0))],
        out_specs=pl.BlockSpec((seq, 512), lambda b, it, jt: (b, 0)),
        scratch_shapes=_flash_scratch(nq, NSA_HEADS, NSA_Q))
    return pl.pallas_call(
        _slc_kernel,
        out_shape=jax.ShapeDtypeStruct((n, 512), BF16),
        grid_spec=grid_spec,
        compiler_params=_params(1, VMEM_LIMIT),
        name="slc",
    )(it, jt, qrot, sel, ksa3, vst, gnt3)


def _win_kernel(q_ref, k_ref, vt_ref, gnt_ref, o_ref, qs_scr, s_scr, mx_scr):
    nq = q_ref.shape[0] // T
    nh = NSA_HEADS
    npairs = nh // 2
    assert npairs % 2 == 0

    def prep(i, carry):
        rows = pl.ds(pl.multiple_of(i * T, T), T)
        _split_heads(q_ref.at[rows, :], [0.0, 0.0], qs_scr, i * nh)
        return carry

    lax.fori_loop(0, nq, prep, 0)
    krow = lax.broadcasted_iota(jnp.int32, (T, T), 0)
    qcol = lax.broadcasted_iota(jnp.int32, (T, T), 1)
    span = WINDOW // T
    ones = jnp.ones((16, T), BF16)

    def scores(i, hp, buf):
        masks = {0: krow <= qcol, span: krow > qcol + jnp.where(i >= span, 0, T)}
        for par in range(2):
            h = 2 * hp + par
            g = h // NSA_HPG
            mx = None
            for d in range(span + 1):
                j = i - d
                s = _nt(k_ref[jnp.maximum(j, 0), :, LANE * g:LANE * (g + 1)], qs_scr[i * nh + h])
                if d in masks:
                    s = jnp.where(masks[d], s, NEG)
                else:
                    s = s + jnp.where(j >= 0, 0.0, NEG)
                s_scr[buf, par, d] = s
                md = jnp.max(s, axis=0, keepdims=True)
                mx = md if mx is None else jnp.maximum(mx, md)
            mx_scr[buf, par] = mx

    def softmax_pv(i, hp, buf):
        outs = []
        for par in range(2):
            h = 2 * hp + par
            g = h // NSA_HPG
            m = mx_scr[buf, par]
            acc = jnp.zeros((64 + 16, T), F32)
            for d in range(span + 1):
                parts = [jnp.exp2(s_scr[buf, par, d, r:r + SM_ROWS, :] - m).astype(BF16)
                         for r in range(0, T, SM_ROWS)]
                vt = jnp.concatenate([vt_ref[jnp.maximum(i - d, 0), 64 * g:64 * (g + 1), :], ones], axis=0)
                acc = acc + _nn(vt, jnp.concatenate(parts, axis=0))
            outs.append(acc[0:64] * (gnt_ref[i, 3 * h + 2:3 * h + 3, :] / acc[64:65]))
        rows = pl.ds(pl.multiple_of(i * T, T), T)
        o_ref[rows, LANE * hp:LANE * (hp + 1)] = jnp.concatenate(outs, axis=0).T.astype(BF16)

    scores(0, 0, 0)

    def body(i, carry):
        for hp in range(npairs):
            if hp + 1 < npairs:
                scores(i, hp + 1, (hp + 1) % 2)
            else:
                scores(jnp.minimum(i + 1, nq - 1), 0, 0)
            softmax_pv(i, hp, hp % 2)
        return carry

    lax.fori_loop(0, nq, body, 0)


def _win(qrot, kwa3, vwt, gnt, nb, seq):
    n = qrot.shape[0]
    nq = seq // T
    return pl.pallas_call(
        _win_kernel,
        out_shape=jax.ShapeDtypeStruct((n, 512), BF16),
        grid=(nb,),
        in_specs=[pl.BlockSpec((seq, 512), lambda b: (b, 0)),
                  pl.BlockSpec((nq, T, 256), lambda b: (b, 0, 0)),
                  pl.BlockSpec((nq, LANE, T), lambda b: (b, 0, 0)),
                  pl.BlockSpec((nq, GATE_ROWS, T), lambda b: (b, 0, 0))],
        out_specs=pl.BlockSpec((seq, 512), lambda b: (b, 0)),
        scratch_shapes=[pltpu.VMEM((nq * NSA_HEADS, T, LANE), BF16),
                        pltpu.VMEM((2, 2, WINDOW // T + 1, T, T), F32),
                        pltpu.VMEM((2, 2, 1, T), F32)],
        compiler_params=_params(1, VMEM_LIMIT),
        name="win",
    )(qrot, kwa3, vwt, gnt)


def _diff_kernel(it_ref, jt_ref, q_ref, k_ref, vt_ref, lam_ref, g_ref, o_ref,
                 qs_scr, s_scr, mx_scr, m_scr, l_scr, acc_scr):
    nq = q_ref.shape[0] // T
    nc = 2 * DIFF_HEADS
    lane = lax.broadcasted_iota(jnp.int32, (T, LANE), 1)

    def prep(i, carry):
        rows = pl.ds(pl.multiple_of(i * T, T), T)
        for h in range(DIFF_HEADS):
            q = q_ref[rows, LANE * h:LANE * (h + 1)].astype(F32)
            qs_scr[i * nc + 2 * h] = jnp.where(lane < 64, q, 0.0).astype(BF16)
            qs_scr[i * nc + 2 * h + 1] = jnp.where(lane < 64, 0.0, q).astype(BF16)
        return carry

    lax.fori_loop(0, nq, prep, 0)
    chains = [(LANE * (c // 2), LANE * (c // 2), LANE) for c in range(nc)]
    _causal_flash(it_ref, jt_ref, nq, k_ref, vt_ref, qs_scr, s_scr, mx_scr, m_scr, l_scr, acc_scr, chains,
                  lambda i, c: acc_scr.at[i * nc + c])

    lp = lam_ref[...]
    lam = (jnp.exp(jnp.sum(lp[0:1] * lp[1:2], axis=1, keepdims=True))
           - jnp.exp(jnp.sum(lp[2:3] * lp[3:4], axis=1, keepdims=True)) + LAMBDA_INIT)

    def finish(i, carry):
        rows = pl.ds(pl.multiple_of(i * T, T), T)
        for h in range(DIFF_HEADS):
            c = i * nc + 2 * h
            o = acc_scr[c] * (1.0 / l_scr[c]) - lam * (acc_scr[c + 1] * (1.0 / l_scr[c + 1]))
            o = o * lax.rsqrt(jnp.mean(o * o, axis=0, keepdims=True) + RMS_EPS)
            o_ref[rows, LANE * h:LANE * (h + 1)] = ((o.T * g_ref[...]) * (1.0 - LAMBDA_INIT)).astype(BF16)
        return carry

    lax.fori_loop(0, nq, finish, 0)


def _diff(qdf, kdf3, vdft, lam, norm_g, nb, seq):
    n = qdf.shape[0]
    nq = seq // T
    it, jt = _tile_schedule(nq)
    grid_spec = pltpu.PrefetchScalarGridSpec(
        num_scalar_prefetch=2,
        grid=(nb,),
        in_specs=[pl.BlockSpec((seq, 512), lambda b, it, jt: (b, 0)),
                  pl.BlockSpec((nq, T, 512), lambda b, it, jt: (b, 0, 0)),
                  pl.BlockSpec((nq, 512, T), lambda b, it, jt: (b, 0, 0)),
                  pl.BlockSpec((4, HEAD_DIM), lambda b, it, jt: (0, 0)),
                  pl.BlockSpec((1, LANE), lambda b, it, jt: (0, 0))],
        out_specs=pl.BlockSpec((seq, 512), lambda b, it, jt: (b, 0)),
        scratch_shapes=_flash_scratch(nq, 2 * DIFF_HEADS, 2 * DIFF_HEADS * DIFF_V_DIM))
    return pl.pallas_call(
        _diff_kernel,
        out_shape=jax.ShapeDtypeStruct((n, 512), BF16),
        grid_spec=grid_spec,
        compiler_params=_params(1, VMEM_LIMIT),
        name="diff",
    )(it, jt, qdf, kdf3, vdft, lam, norm_g)


def _layer_norm(z, g, b):
    mu = jnp.mean(z, axis=-1, keepdims=True)
    zc = z - mu
    var = jnp.mean(zc * zc, axis=-1, keepdims=True)
    return zc * lax.rsqrt(var + LN_EPS) * g + b


def _merge_kernel(x_ref, ocmp_ref, oslc_ref, owin_ref, yb_ref, gm_ref,
                  wa_ref, wb_ref, wo_ref, g_ref, b_ref, h_ref):
    ya = ocmp_ref[...].astype(F32) + oslc_ref[...].astype(F32) + owin_ref[...].astype(F32)
    ta = _nn(ya.astype(BF16), wa_ref[...])
    tb = _nn(yb_ref[...], wb_ref[...])
    merged = gm_ref[:, 0:D_MODEL].astype(F32) * ta + gm_ref[:, D_MODEL:2 * D_MODEL].astype(F32) * tb
    mix = _nn(merged.astype(BF16), wo_ref[...])
    h_ref[...] = _layer_norm(DEEPNORM_ALPHA * x_ref[...] + mix, g_ref[...], b_ref[...])


def _merge(x2, ocmp, oslc, owin, yb, gm, wa, wb, wo, g, b):
    n = x2.shape[0]
    row = lambda w: pl.BlockSpec((TM, w), lambda i: (i, 0))
    full = lambda a: pl.BlockSpec(a.shape, lambda i: (0,) * a.ndim)
    return pl.pallas_call(
        _merge_kernel,
        out_shape=jax.ShapeDtypeStruct((n, D_MODEL), F32),
        grid=(n // TM,),
        in_specs=[row(D_MODEL), row(512), row(512), row(512), row(512), row(2048),
                  full(wa), full(wb), full(wo), full(g), full(b)],
        out_specs=row(D_MODEL),
        compiler_params=_params(1, VMEM_LIMIT),
        name="merge",
    )(x2, ocmp, oslc, owin, yb, gm, wa, wb, wo, g, b)


def _ffn_kernel(h_ref, wgu_ref, wd_ref, g_ref, b_ref, o_ref):
    h = h_ref[...]
    hb = h.astype(BF16)
    gate = _nn(hb, wgu_ref[:, 0:FFN_HIDDEN])
    up = _nn(hb, wgu_ref[:, FFN_HIDDEN:2 * FFN_HIDDEN])
    act = (gate * jax.nn.sigmoid(gate) * up).astype(BF16)
    y = _nn(act, wd_ref[...])
    o_ref[...] = _layer_norm(DEEPNORM_ALPHA * h + y, g_ref[...], b_ref[...])


def _ffn(h1, wgu, wd, g, b):
    n = h1.shape[0]
    row = pl.BlockSpec((TM, D_MODEL), lambda i: (i, 0))
    full = lambda a: pl.BlockSpec(a.shape, lambda i: (0,) * a.ndim, pipeline_mode=pl.Buffered(1))
    return pl.pallas_call(
        _ffn_kernel,
        out_shape=jax.ShapeDtypeStruct((n, D_MODEL), F32),
        grid=(n // TM,),
        in_specs=[row, full(wgu), full(wd), full(g), full(b)],
        out_specs=row,
        compiler_params=_params(1, VMEM_LIMIT),
        name="ffn",
    )(h1, wgu, wd, g, b)


def _rope_tables(seq):
    half = ROPE_DIM // 2
    inv_freq = ROPE_THETA ** (-jnp.arange(half, dtype=F32) * 2.0 / ROPE_DIM)
    ang = jnp.arange(seq, dtype=F32)[:, None] * inv_freq[None, :]
    cos, sin = jnp.cos(ang), jnp.sin(ang)
    ones = jnp.ones((seq, HEAD_DIM - ROPE_DIM), F32)
    zeros8 = jnp.zeros((seq, half), F32)
    zeros48 = jnp.zeros((seq, HEAD_DIM - ROPE_DIM), F32)
    c64 = jnp.concatenate([cos, cos, ones], axis=1)
    sa64 = jnp.concatenate([-sin, zeros8, zeros48], axis=1)
    sb64 = jnp.concatenate([zeros8, sin, zeros48], axis=1)
    rep = lambda a: jnp.concatenate([a, a], axis=1)
    pos_blk = jnp.arange(seq, dtype=jnp.int32)[:, None] // SLC_LEN
    lane = jnp.arange(LANE, dtype=jnp.int32)[None, :]
    onehot = ((lane >= 64) & (lane < 96) & (lane - 64 == pos_blk)).astype(F32)
    return rep(c64), rep(sa64), rep(sb64), onehot


def _prep_w_in(w):
    offs = np.cumsum((0,) + IN_WIDTHS)
    lo, gn, hi = w[:, :offs[7]], w[:, offs[7]:offs[8]], w[:, offs[8]:]
    gnp = jnp.pad(gn, ((0, 0), (0, LANE - gn.shape[1])))
    return lo.astype(BF16), hi.astype(BF16), gnp.astype(BF16)


def _prep_compress(cmp_pe, cmp_w1, cmp_b1, cmp_w2):
    half = CMP_LEN // 2
    pet, peb, wt, wb, b1 = [], [], [], [], []
    for kv in range(2):
        pe = cmp_pe[kv]
        tile2 = lambda a: jnp.concatenate([a, a], axis=1).reshape(1, half * 2 * HEAD_DIM)
        pet.append(tile2(pe[:half]))
        peb.append(tile2(pe[half:]))
        w1 = cmp_w1[kv].reshape(CMP_LEN, HEAD_DIM, CMP_HIDDEN)
        z = jnp.zeros((half, HEAD_DIM, CMP_HIDDEN), w1.dtype)

        def spread(wh):
            g0 = jnp.concatenate([wh, z], axis=2)
            g1 = jnp.concatenate([z, wh], axis=2)
            return jnp.stack([g0, g1], axis=1).reshape(half * 2 * HEAD_DIM, 2 * CMP_HIDDEN)

        wt.append(spread(w1[:half]))
        wb.append(spread(w1[half:]))
        b1.append(jnp.concatenate([cmp_b1[kv], cmp_b1[kv]])[None, :])
    w2k, w2v = cmp_w2[0], cmp_w2[1]
    zk = jnp.zeros_like(w2k)
    w2k_g = jnp.stack([
        jnp.concatenate([jnp.concatenate([w2k, zk], axis=1), jnp.zeros((CMP_HIDDEN, LANE), w2k.dtype)], axis=0),
        jnp.concatenate([jnp.zeros((CMP_HIDDEN, LANE), w2k.dtype), jnp.concatenate([w2k, zk], axis=1)], axis=0),
    ])
    zv = jnp.zeros_like(w2v)
    w2v_bd = jnp.concatenate([jnp.concatenate([w2v, zv], axis=1),
                              jnp.concatenate([zv, w2v], axis=1)], axis=0)
    st = lambda xs: jnp.stack(xs)
    return (st(pet), st(peb), st(wt).astype(BF16), st(wb).astype(BF16), st(b1),
            w2k_g.astype(BF16), w2v_bd.astype(BF16))


def kernel(x, w_in, cmp_pe, cmp_w1, cmp_b1, cmp_w2, diff_lambda, diff_norm_g, w_branch_a, w_branch_b,
           w_o, ln1_g, ln1_b, w_gate_up, w_down, ln2_g, ln2_b):
    nb, seq, d = x.shape
    assert d == D_MODEL and seq % T == 0 and seq // SLC_LEN == 32 and seq // CMP_STRIDE == 128
    n = nb * seq
    x2 = x.reshape(n, d)
    cos_t, sa_t, sb_t, oh_t = _rope_tables(seq)
    (qraw, qrot, kcs, vcs, ksa, vst, kwa, vwt, qdf, kdf, vdft, gm, gnt) = _inproj(
        x2, *_prep_w_in(w_in[0]), cos_t, sa_t, sb_t, oh_t, seq)

    pet, peb, wt, wb, b1, w2k, w2v = _prep_compress(cmp_pe[0], cmp_w1[0], cmp_b1[0], cmp_w2[0])
    kca, vct = _compress(kcs, vcs, pet, peb, wt, wb, b1, w2k, w2v, nb, seq)

    ocmp, sel = _cmpsel(qraw, kca, vct, gnt, nb, seq)
    nt = n // T
    oslc = _slc(qrot, sel, ksa.reshape(nt, T, 256), vst, gnt, nb, seq)
    owin = _win(qrot, kwa.reshape(nt, T, 256), vwt, gnt, nb, seq)
    yb = _diff(qdf, kdf.reshape(nt, T, 512), vdft, diff_lambda[0], diff_norm_g[0][None, :], nb, seq)

    h1 = _merge(x2, ocmp, oslc, owin, yb, gm,
                w_branch_a[0].astype(BF16), w_branch_b[0].astype(BF16), w_o[0].astype(BF16),
                ln1_g[0][None, :], ln1_b[0][None, :])
    out = _ffn(h1, w_gate_up[0].astype(BF16), w_down[0].astype(BF16), ln2_g[0][None, :], ln2_b[0][None, :])
    return out.reshape(nb, seq, d)
```

```python
import functools
import math

import numpy as np
import jax
import jax.numpy as jnp
from jax import lax
from jax.experimental import pallas as pl
from jax.experimental.pallas import tpu as pltpu

D_MODEL = 1024
HEAD_DIM = 64
ROPE_DIM = HEAD_DIM // 4
ROPE_THETA = 500000.0
NSA_HEADS = 8
NSA_GROUPS = 2
NSA_HPG = NSA_HEADS // NSA_GROUPS
CMP_LEN = 32
CMP_STRIDE = 16
CMP_HIDDEN = 256
SLC_LEN = 64
SLC_TOPK = 8
WINDOW = 512
FORCE_BONUS = 1.0e4
DIFF_HEADS = 4
DIFF_V_DIM = 2 * HEAD_DIM
FFN_HIDDEN = ((8 * D_MODEL // 3 + 255) // 256) * 256
DEPTH = 1
DEEPNORM_ALPHA = (2 * DEPTH) ** 0.25
NEG = -1.0e30
LN_EPS = 1e-5
RMS_EPS = 1e-5
LAMBDA_INIT = 0.8 - 0.6 * math.exp(-0.3 * 0)
QK_SCALE = HEAD_DIM ** -0.5
QK_SCALE_LOG2 = QK_SCALE * math.log2(math.e)

NSA_Q = NSA_HEADS * HEAD_DIM
NSA_KV = NSA_GROUPS * HEAD_DIM
DIFF_QK = DIFF_HEADS * 2 * HEAD_DIM
DIFF_V = DIFF_HEADS * DIFF_V_DIM
IN_WIDTHS = (NSA_Q, NSA_KV, NSA_KV, NSA_KV, NSA_KV, NSA_KV, NSA_KV, 3 * NSA_HEADS,
             DIFF_QK, DIFF_QK, DIFF_V, 2 * D_MODEL)

LANE = 128
T = 256
TM = 512
VMEM_LIMIT = 56 * 1024 * 1024
GATE_ROWS = 32

BF16 = jnp.bfloat16
F32 = jnp.float32

_C_Q, _C_KC, _C_VC, _C_KS, _C_VS, _C_KW, _C_VW = 0, 512, 640, 768, 896, 1024, 1152
_C_QD, _C_KD, _C_VD, _C_GM, _C_GN, _C_END = 1280, 1792, 2304, 2816, 4864, 4992


def _nt(a, b):
    return lax.dot_general(a, b, (((1,), (1,)), ((), ())), preferred_element_type=F32)


def _nn(a, b):
    return jnp.dot(a, b, preferred_element_type=F32)


def _params(n_axes, vmem=None):
    return pltpu.CompilerParams(dimension_semantics=("arbitrary",) * n_axes,
                                vmem_limit_bytes=vmem)


def _inproj_kernel(x_ref, w_ref, cos_ref, sa_ref, sb_ref, oh_ref,
                   qraw_ref, qrot_ref, kc_ref, vc_ref, ksa_ref, vst_ref, kwa_ref, vwt_ref,
                   qdf_ref, kdf_ref, vdft_ref, gm_ref, gn_ref, w_scr):
    @pl.when(pl.program_id(0) == 0)
    def _():
        gn0 = _C_QD
        gn1 = gn0 + 3 * NSA_HEADS
        rows = 128

        def regroup(r, carry):
            rs = pl.ds(pl.multiple_of(r * rows, rows), rows)
            w_scr[rs, 0:_C_QD] = w_ref[rs, 0:gn0].astype(BF16)
            w_scr[rs, _C_QD:_C_GN] = w_ref[rs, gn1:gn1 + _C_GN - _C_QD].astype(BF16)
            gates = jnp.concatenate([w_ref[rs, gn0:gn1], jnp.zeros((rows, LANE - (gn1 - gn0)), F32)], axis=1)
            w_scr[rs, _C_GN:_C_END] = gates.astype(BF16)
            return carry

        lax.fori_loop(0, w_ref.shape[0] // rows, regroup, 0)

    xb = x_ref[...].astype(BF16)
    cos = cos_ref[...]
    sa = sa_ref[...]
    sb = sb_ref[...]

    def mm(c0, n):
        return _nn(xb, w_scr[:, c0:c0 + n])

    def rope(t):
        return t * cos + pltpu.roll(t, LANE - 8, 1) * sa + pltpu.roll(t, 8, 1) * sb

    t = mm(_C_Q, 512)
    for j in range(4):
        tj = t[:, LANE * j:LANE * (j + 1)]
        qraw_ref[:, LANE * j:LANE * (j + 1)] = (tj * QK_SCALE_LOG2).astype(BF16)
        qrot_ref[:, LANE * j:LANE * (j + 1)] = (rope(tj) * QK_SCALE_LOG2).astype(BF16)

    t = mm(_C_KC, 256)
    kc_ref[...] = t[:, :LANE]
    vc_ref[...] = t[:, LANE:]

    lane = lax.broadcasted_iota(jnp.int32, (T, LANE), 1)

    def per_group(k, extra, out_ref):
        out_ref[:, 0:LANE] = jnp.where(lane < 64, k, extra).astype(BF16)
        out_ref[:, LANE:2 * LANE] = jnp.where(lane < 64, pltpu.roll(k, 64, 1), extra).astype(BF16)

    t = mm(_C_KS, 256)
    per_group(rope(t[:, :LANE]), oh_ref[...], ksa_ref)
    vst_ref[...] = t[:, LANE:].T.astype(BF16)

    t = mm(_C_KW, 256)
    per_group(rope(t[:, :LANE]), 0.0, kwa_ref)
    vwt_ref[...] = t[:, LANE:].T.astype(BF16)

    t = mm(_C_QD, 512)
    for j in range(4):
        qdf_ref[:, LANE * j:LANE * (j + 1)] = (rope(t[:, LANE * j:LANE * (j + 1)]) * QK_SCALE_LOG2).astype(BF16)
    t = mm(_C_KD, 512)
    for j in range(4):
        kdf_ref[:, LANE * j:LANE * (j + 1)] = rope(t[:, LANE * j:LANE * (j + 1)]).astype(BF16)
    t = mm(_C_VD, 512)
    for j in range(4):
        vdft_ref[LANE * j:LANE * (j + 1), :] = t[:, LANE * j:LANE * (j + 1)].T.astype(BF16)

    for j in range(4):
        t = mm(_C_GM + 512 * j, 512)
        gm_ref[:, 512 * j:512 * (j + 1)] = jax.nn.sigmoid(t).astype(BF16)

    gn_ref[...] = jax.nn.sigmoid(mm(_C_GN, 128)).T[0:GATE_ROWS]


def _inproj(x2, w, cos_t, sa_t, sb_t, oh_t, seq):
    n = x2.shape[0]
    nt = n // T
    spt = seq // T
    assert w.shape == (D_MODEL, sum(IN_WIDTHS)) and sum(IN_WIDTHS[:7]) == _C_QD
    assert sum(IN_WIDTHS) - 3 * NSA_HEADS == _C_GN
    row = lambda w: pl.BlockSpec((T, w), lambda i: (i, 0))
    whole = lambda a: pl.BlockSpec(a.shape, lambda i: (0, 0), pipeline_mode=pl.Buffered(1))
    tab = pl.BlockSpec((T, LANE), lambda i: (i % spt, 0))
    tile_t = lambda r: pl.BlockSpec((None, r, T), lambda i: (i, 0, 0))
    out_shape = (
        jax.ShapeDtypeStruct((n, 512), BF16),
        jax.ShapeDtypeStruct((n, 512), BF16),
        jax.ShapeDtypeStruct((n, LANE), F32),
        jax.ShapeDtypeStruct((n, LANE), F32),
        jax.ShapeDtypeStruct((n, 256), BF16),
        jax.ShapeDtypeStruct((nt, LANE, T), BF16),
        jax.ShapeDtypeStruct((n, 256), BF16),
        jax.ShapeDtypeStruct((nt, LANE, T), BF16),
        jax.ShapeDtypeStruct((n, 512), BF16),
        jax.ShapeDtypeStruct((n, 512), BF16),
        jax.ShapeDtypeStruct((nt, 512, T), BF16),
        jax.ShapeDtypeStruct((n, 2048), BF16),
        jax.ShapeDtypeStruct((nt, GATE_ROWS, T), F32),
    )
    out_specs = (row(512), row(512), row(LANE), row(LANE), row(256), tile_t(LANE), row(256),
                 tile_t(LANE), row(512), row(512), tile_t(512), row(2048), tile_t(GATE_ROWS))
    return pl.pallas_call(
        _inproj_kernel,
        out_shape=out_shape,
        grid=(nt,),
        in_specs=[row(D_MODEL), whole(w), tab, tab, tab, tab],
        out_specs=out_specs,
        scratch_shapes=[pltpu.VMEM((D_MODEL, _C_END), BF16)],
        compiler_params=_params(1, VMEM_LIMIT),
        name="inproj",
    )(x2, w, cos_t, sa_t, sb_t, oh_t)


def _gelu_tanh(x):
    c = math.sqrt(2.0 / math.pi)
    return x * (0.5 * (1.0 + jnp.tanh(c * (x + 0.044715 * (x * x * x)))))


def _compress_kernel(kf_ref, vf_ref, pet_ref, peb_ref, wt_ref, wb_ref, b1_ref, w2k_ref, w2v_ref,
                     kca_ref, vct_ref):
    nchunk = kf_ref.shape[0] // CMP_STRIDE

    def hidden(x_ref, kv):
        a = jnp.zeros((nchunk, 2 * CMP_HIDDEN), F32)
        b = jnp.zeros((nchunk, 2 * CMP_HIDDEN), F32)
        for p in range(0, CMP_STRIDE, 2):
            x = jnp.concatenate([x_ref[pl.ds(p, nchunk, stride=CMP_STRIDE), :],
                                 x_ref[pl.ds(p + 1, nchunk, stride=CMP_STRIDE), :]], axis=1)
            c0, c1 = LANE * p, LANE * (p + 2)
            a = a + _nn((x + pet_ref[kv, :, c0:c1]).astype(BF16), wt_ref[kv, c0:c1, :])
            b = b + _nn((x + peb_ref[kv, :, c0:c1]).astype(BF16), wb_ref[kv, c0:c1, :])
        h = a + pltpu.roll(b, nchunk - 1, 0) + b1_ref[kv]
        return _gelu_tanh(h).astype(BF16)

    hk = hidden(kf_ref, 0)
    for g in range(2):
        kca_ref[g] = _nn(hk, w2k_ref[g]).astype(BF16)
    hv = hidden(vf_ref, 1)
    vct_ref[...] = _nn(hv, w2v_ref[...]).T.astype(BF16)


def _compress(kf, vf, pet, peb, wt, wb, b1, w2k, w2v, nb, seq):
    full = lambda a: pl.BlockSpec(a.shape, lambda b: (0,) * a.ndim)
    return pl.pallas_call(
        _compress_kernel,
        out_shape=(jax.ShapeDtypeStruct((nb, 2, 128, LANE), BF16),
                   jax.ShapeDtypeStruct((nb, LANE, 128), BF16)),
        grid=(nb,),
        in_specs=[pl.BlockSpec((seq, LANE), lambda b: (b, 0)),
                  pl.BlockSpec((seq, LANE), lambda b: (b, 0)),
                  full(pet), full(peb), full(wt), full(wb), full(b1), full(w2k), full(w2v)],
        out_specs=(pl.BlockSpec((None, 2, 128, LANE), lambda b: (b, 0, 0, 0)),
                   pl.BlockSpec((None, LANE, 128), lambda b: (b, 0, 0))),
        compiler_params=_params(1, VMEM_LIMIT),
        name="compress",
    )(kf, vf, pet, peb, wt, wb, b1, w2k, w2v)


def _cmpsel_kernel(q_ref, kca_ref, vct_ref, gnt_ref, ocmp_ref, sel_ref, s_scr):
    nq = q_ref.shape[0] // T
    lane = lax.broadcasted_iota(jnp.int32, (T, LANE), 1)
    crow = lax.broadcasted_iota(jnp.int32, (128, T), 0)
    tcol0 = lax.broadcasted_iota(jnp.int32, (128, T), 1)
    tlane = lax.broadcasted_iota(jnp.int32, (1, T), 1)

    jj = lax.broadcasted_iota(jnp.int32, (32, 128), 0) * SLC_LEN
    cc = lax.broadcasted_iota(jnp.int32, (32, 128), 1) * CMP_STRIDE
    ov = jnp.maximum(jnp.minimum(cc + CMP_LEN, jj + SLC_LEN) - jnp.maximum(cc, jj), 0)
    ovt = (ov.astype(F32) * (1.0 / CMP_LEN)).astype(BF16)

    jrow = lax.broadcasted_iota(jnp.int32, (32, T), 0)
    jrow8 = lax.broadcasted_iota(jnp.int32, (8, T), 0)
    tblk0 = lax.broadcasted_iota(jnp.int32, (32, T), 1)

    def scores(i, buf):
        rows = pl.ds(pl.multiple_of(i * T, T), T)
        for hp in range(NSA_HEADS // 2):
            g = hp // (NSA_HPG // 2)
            qp = q_ref[rows, LANE * hp:LANE * (hp + 1)].astype(F32)
            qs = (jnp.where(lane < 64, qp, 0.0), jnp.where(lane < 64, pltpu.roll(qp, 64, 1), 0.0))
            for par in range(2):
                s_scr[buf, 2 * hp + par] = _nt(kca_ref[g], qs[par].astype(BF16))

    def attend_select(i, buf):
        t0 = i * T
        rows = pl.ds(pl.multiple_of(t0, T), T)
        cmask = (crow * CMP_STRIDE + (CMP_LEN - 1)) <= tcol0 + t0
        live = jnp.where(tlane + t0 >= (CMP_LEN - 1), 1.0, 0.0)
        tblk = (tblk0 + t0) // SLC_LEN
        valid = jrow <= tblk
        forced = (jrow == 0) | (jrow == tblk) | (jrow == tblk - 1)
        bonus = jnp.where(forced, FORCE_BONUS, 0.0)
        for g in range(NSA_GROUPS):
            vt = vct_ref[64 * g:64 * (g + 1), :]
            psum = jnp.zeros((128, T), F32)
            for hp in range(NSA_HPG // 2):
                outs = []
                for par in range(2):
                    h = NSA_HPG * g + 2 * hp + par
                    s = jnp.where(cmask, s_scr[buf, h], NEG)
                    m = jnp.max(s, axis=0, keepdims=True)
                    e = jnp.exp2(s - m)
                    p = e * (live / jnp.sum(e, axis=0, keepdims=True))
                    psum = psum + p
                    outs.append(_nn(vt, p.astype(BF16)) * gnt_ref[i, 3 * h:3 * h + 1, :])
                ocmp_ref[rows, LANE * (2 * g + hp):LANE * (2 * g + hp + 1)] = (
                    jnp.concatenate(outs, axis=0).T.astype(BF16))

            p_hi = psum.astype(BF16)
            p_lo = (psum - p_hi.astype(F32)).astype(BF16)
            pslc = _nn(ovt, p_hi) + _nn(ovt, p_lo)
            pri = jnp.where(valid, pslc + bonus, -1.0)
            rank = [jnp.zeros((8, T), F32) for _ in range(4)]
            for r in range(32):
                row = pri[r:r + 1, :]
                for a in range(4):
                    pa = pri[8 * a:8 * (a + 1), :]
                    if 8 * a > r:
                        ahead = jnp.where(row >= pa, 1.0, 0.0)
                    elif 8 * a + 7 < r:
                        ahead = jnp.where(row > pa, 1.0, 0.0)
                    else:
                        ahead = jnp.where(jrow8 + 8 * a > r, jnp.where(row >= pa, 1.0, 0.0),
                                          jnp.where(row > pa, 1.0, 0.0))
                    rank[a] = rank[a] + ahead
            rank = jnp.concatenate(rank, axis=0)
            selneg = jnp.where(rank < float(SLC_TOPK), 0.0, NEG)
            pad = jnp.concatenate([jnp.zeros((64, T), F32), selneg, jnp.zeros((32, T), F32)], axis=0)
            sel_ref[rows, LANE * g:LANE * (g + 1)] = pad.T.astype(BF16)

    scores(0, 0)

    def body(u, carry):
        scores(2 * u + 1, 1)
        attend_select(2 * u, 0)
        scores(jnp.minimum(2 * u + 2, nq - 1), 0)
        attend_select(2 * u + 1, 1)
        return carry

    lax.fori_loop(0, nq // 2, body, 0)


def _cmpsel(qraw, kca, vct, gnt, nb, seq):
    n = qraw.shape[0]
    nq = seq // T
    assert nq % 2 == 0
    return pl.pallas_call(
        _cmpsel_kernel,
        out_shape=(jax.ShapeDtypeStruct((n, 512), BF16), jax.ShapeDtypeStruct((n, 256), BF16)),
        grid=(nb,),
        in_specs=[pl.BlockSpec((seq, 512), lambda b: (b, 0)),
                  pl.BlockSpec((None, 2, 128, LANE), lambda b: (b, 0, 0, 0)),
                  pl.BlockSpec((None, LANE, 128), lambda b: (b, 0, 0)),
                  pl.BlockSpec((nq, GATE_ROWS, T), lambda b: (b, 0, 0))],
        out_specs=(pl.BlockSpec((seq, 512), lambda b: (b, 0)),
                   pl.BlockSpec((seq, 256), lambda b: (b, 0))),
        scratch_shapes=[pltpu.VMEM((2, NSA_HEADS, 128, T), F32)],
        compiler_params=_params(1, VMEM_LIMIT),
        name="cmpsel",
    )(qraw, kca, vct, gnt)


def _split_heads(q_ref, extras, qs_scr, base=0):
    lane = lax.broadcasted_iota(jnp.int32, (T, LANE), 1)
    for hp in range(NSA_HEADS // 2):
        extra = extras[hp // (NSA_HPG // 2)]
        qp = q_ref[:, LANE * hp:LANE * (hp + 1)].astype(F32)
        qs_scr[base + 2 * hp] = jnp.where(lane < 64, qp, extra).astype(BF16)
        qs_scr[base + 2 * hp + 1] = jnp.where(lane < 64, pltpu.roll(qp, 64, 1), extra).astype(BF16)


SM_ROWS = 32


RING = 4


def _tile_schedule(nq):
    below = [(i, j) for j in range(nq) for i in range(j + 1, nq)]
    assert len(below) == (RING - 1) * nq + RING
    pairs = []
    for i in range(nq):
        pairs += [(i, i)] + below[(RING - 1) * i:(RING - 1) * (i + 1)]
    pairs += below[(RING - 1) * nq:]
    return (jnp.asarray([p[0] for p in pairs], jnp.int32), jnp.asarray([p[1] for p in pairs], jnp.int32))


def _flash_scratch(nq, nc, acc_rows):
    return [pltpu.VMEM((nq * nc, T, LANE), BF16),
            pltpu.VMEM((RING, nc, T, T), F32),
            pltpu.VMEM((RING, nc, 1, T), F32),
            pltpu.VMEM((nq * nc, 1, T), F32),
            pltpu.VMEM((nq * nc, 1, T), F32),
            pltpu.VMEM((nq * acc_rows // LANE, LANE, T), F32)]


def _causal_flash(it_ref, jt_ref, nq, k_ref, vt_ref, qs_scr, s_scr, mx_scr, m_scr, l_scr, acc_scr, chains, acc_of):
    nc = len(chains)
    nsets = it_ref.shape[0]
    assert nsets % RING == 0 and nsets >= 2 * RING
    m_scr[...] = jnp.full(m_scr.shape, NEG, F32)
    l_scr[...] = jnp.zeros(l_scr.shape, F32)
    acc_scr[...] = jnp.zeros(acc_scr.shape, F32)
    krow = lax.broadcasted_iota(jnp.int32, (T, T), 0)
    qcol = lax.broadcasted_iota(jnp.int32, (T, T), 1)
    ones = jnp.ones((16, T), BF16)

    def scores(n, buf):
        i, j = it_ref[n], jt_ref[n]
        if buf == 0:
            mask = krow <= qcol + jnp.where(i == j, 0, T)
        for c, (kl, _, _) in enumerate(chains):
            s = _nt(k_ref[j, :, kl:kl + LANE], qs_scr[i * nc + c])
            if buf == 0:
                s = jnp.where(mask, s, NEG)
            s_scr[buf, c] = s
            mx_scr[buf, c] = jnp.max(s, axis=0, keepdims=True)

    def softmax_pv(n, buf):
        i, j = it_ref[n], jt_ref[n]
        for c, (_, vr, dv) in enumerate(chains):
            m_prev = m_scr[i * nc + c]
            m_new = jnp.maximum(m_prev, mx_scr[buf, c])
            alpha = jnp.exp2(m_prev - m_new)
            parts = [jnp.exp2(s_scr[buf, c, r:r + SM_ROWS, :] - m_new).astype(BF16)
                     for r in range(0, T, SM_ROWS)]
            vt = jnp.concatenate([vt_ref[j, vr:vr + dv, :], ones], axis=0)
            o = _nn(vt, jnp.concatenate(parts, axis=0))
            l_scr[i * nc + c] = alpha * l_scr[i * nc + c] + o[dv:dv + 1]
            m_scr[i * nc + c] = m_new
            acc = acc_of(i, c)
            acc[...] = acc[...] * alpha + o[0:dv]

    scores(0, 0)

    def body(u, carry):
        n = RING * u
        for r in range(RING):
            scores(n + r + 1, (r + 1) % RING)
            softmax_pv(n + r, r)
        return carry

    lax.fori_loop(0, nsets // RING - 1, body, 0)
    n = nsets - RING
    for r in range(RING):
        if r + 1 < RING:
            scores(n + r + 1, r + 1)
        softmax_pv(n + r, r)


def _slc_kernel(it_ref, jt_ref, q_ref, sel_ref, k_ref, vt_ref, gnt_ref, o_ref,
                qs_scr, s_scr, mx_scr, m_scr, l_scr, acc_scr):
    nq = q_ref.shape[0] // T
    nh = NSA_HEADS

    def prep(i, carry):
        rows = pl.ds(pl.multiple_of(i * T, T), T)
        sel = sel_ref[rows, :].astype(F32)
        _split_heads(q_ref.at[rows, :], [sel[:, 0:LANE], sel[:, LANE:2 * LANE]], qs_scr, i * nh)
        return carry

    lax.fori_loop(0, nq, prep, 0)
    chains = [(LANE * (h // NSA_HPG), 64 * (h // NSA_HPG), 64) for h in range(nh)]
    acc_of = lambda i, h: acc_scr.at[i * (nh // 2) + h // 2, 64 * (h % 2):64 * (h % 2 + 1), :]
    _causal_flash(it_ref, jt_ref, nq, k_ref, vt_ref, qs_scr, s_scr, mx_scr, m_scr, l_scr, acc_scr, chains, acc_of)

    def finish(i, carry):
        rows = pl.ds(pl.multiple_of(i * T, T), T)
        for hp in range(nh // 2):
            scale = [gnt_ref[i, 3 * h + 1:3 * h + 2, :] / l_scr[i * nh + h] for h in (2 * hp, 2 * hp + 1)]
            inv = jnp.concatenate([jnp.broadcast_to(scale[0], (64, T)), jnp.broadcast_to(scale[1], (64, T))],
                                  axis=0)
            o_ref[rows, LANE * hp:LANE * (hp + 1)] = (acc_scr[i * (nh // 2) + hp] * inv).T.astype(BF16)
        return carry

    lax.fori_loop(0, nq, finish, 0)


def _slc(qrot, sel, ksa3, vst, gnt3, nb, seq):
    n = qrot.shape[0]
    nq = seq // T
    it, jt = _tile_schedule(nq)
    grid_spec = pltpu.PrefetchScalarGridSpec(
        num_scalar_prefetch=2,
        grid=(nb,),
        in_specs=[pl.BlockSpec((seq, 512), lambda b, it, jt: (b, 0)),
                  pl.BlockSpec((seq, 256), lambda b, it, jt: (b, 0)),
                  pl.BlockSpec((nq, T, 256), lambda b, it, jt: (b, 0, 0)),
                  pl.BlockSpec((nq, LANE, T), lambda b, it, jt: (b, 0, 0)),
                  pl.BlockSpec((nq, GATE_ROWS, T), lambda b, it, jt: (b, 0, 0))],
        out_specs=pl.BlockSpec((seq, 512), lambda b, it, jt: (b, 0)),
        scratch_shapes=_flash_scratch(nq, NSA_HEADS, NSA_Q))
    return pl.pallas_call(
        _slc_kernel,
        out_shape=jax.ShapeDtypeStruct((n, 512), BF16),
        grid_spec=grid_spec,
        compiler_params=_params(1, VMEM_LIMIT),
        name="slc",
    )(it, jt, qrot, sel, ksa3, vst, gnt3)


def _win_kernel(q_ref, k_ref, vt_ref, gnt_ref, o_ref, qs_scr, s_scr, mx_scr, acc_scr):
    i = pl.program_id(1)
    _split_heads(q_ref, [0.0, 0.0], qs_scr)
    krow = lax.broadcasted_iota(jnp.int32, (T, T), 0)
    qcol = lax.broadcasted_iota(jnp.int32, (T, T), 1)
    span = WINDOW // T
    masks = {0: krow <= qcol, span: (krow > qcol) & (i >= span)}
    ones = jnp.ones((16, T), BF16)

    def scores(hp, buf):
        for par in range(2):
            h = 2 * hp + par
            g = h // NSA_HPG
            mx = None
            for d in range(span + 1):
                j = i - d
                s = _nt(k_ref[jnp.maximum(j, 0), :, LANE * g:LANE * (g + 1)], qs_scr[h])
                if d in masks:
                    s = jnp.where(masks[d], s, NEG)
                else:
                    s = s + jnp.where(j >= 0, 0.0, NEG)
                s_scr[buf, par, d] = s
                md = jnp.max(s, axis=0, keepdims=True)
                mx = md if mx is None else jnp.maximum(mx, md)
            mx_scr[buf, par] = mx

    def softmax_pv(hp, buf):
        for par in range(2):
            g = (2 * hp + par) // NSA_HPG
            m = mx_scr[buf, par]
            acc = jnp.zeros((64 + 16, T), F32)
            for d in range(span + 1):
                parts = [jnp.exp2(s_scr[buf, par, d, r:r + SM_ROWS, :] - m).astype(BF16)
                         for r in range(0, T, SM_ROWS)]
                vt = jnp.concatenate([vt_ref[jnp.maximum(i - d, 0), 64 * g:64 * (g + 1), :], ones], axis=0)
                acc = acc + _nn(vt, jnp.concatenate(parts, axis=0))
            gate = gnt_ref[3 * (2 * hp + par) + 2:3 * (2 * hp + par) + 3, :]
            acc_scr[hp, 64 * par:64 * (par + 1), :] = acc[0:64] * (gate / acc[64:65])

    npairs = NSA_HEADS // 2
    scores(0, 0)
    for hp in range(npairs):
        if hp + 1 < npairs:
            scores(hp + 1, (hp + 1) % 2)
        softmax_pv(hp, hp % 2)
    for hp in range(npairs):
        o_ref[:, LANE * hp:LANE * (hp + 1)] = acc_scr[hp].T.astype(BF16)


def _win(qrot, kwa3, vwt, gnt, nb, seq):
    n = qrot.shape[0]
    nq = seq // T
    return pl.pallas_call(
        _win_kernel,
        out_shape=jax.ShapeDtypeStruct((n, 512), BF16),
        grid=(nb, nq),
        in_specs=[pl.BlockSpec((T, 512), lambda b, i: (b * nq + i, 0)),
                  pl.BlockSpec((nq, T, 256), lambda b, i: (b, 0, 0)),
                  pl.BlockSpec((nq, LANE, T), lambda b, i: (b, 0, 0)),
                  pl.BlockSpec((None, GATE_ROWS, T), lambda b, i: (b * nq + i, 0, 0))],
        out_specs=pl.BlockSpec((T, 512), lambda b, i: (b * nq + i, 0)),
        scratch_shapes=[pltpu.VMEM((NSA_HEADS, T, LANE), BF16),
                        pltpu.VMEM((2, 2, WINDOW // T + 1, T, T), F32),
                        pltpu.VMEM((2, 2, 1, T), F32),
                        pltpu.VMEM((NSA_HEADS // 2, LANE, T), F32)],
        compiler_params=_params(2, VMEM_LIMIT),
        name="win",
    )(qrot, kwa3, vwt, gnt)


def _diff_kernel(it_ref, jt_ref, q_ref, k_ref, vt_ref, lam_ref, g_ref, o_ref,
                 qs_scr, s_scr, mx_scr, m_scr, l_scr, acc_scr):
    nq = q_ref.shape[0] // T
    nc = 2 * DIFF_HEADS
    lane = lax.broadcasted_iota(jnp.int32, (T, LANE), 1)

    def prep(i, carry):
        rows = pl.ds(pl.multiple_of(i * T, T), T)
        for h in range(DIFF_HEADS):
            q = q_ref[rows, LANE * h:LANE * (h + 1)].astype(F32)
            qs_scr[i * nc + 2 * h] = jnp.where(lane < 64, q, 0.0).astype(BF16)
            qs_scr[i * nc + 2 * h + 1] = jnp.where(lane < 64, 0.0, q).astype(BF16)
        return carry

    lax.fori_loop(0, nq, prep, 0)
    chains = [(LANE * (c // 2), LANE * (c // 2), LANE) for c in range(nc)]
    _causal_flash(it_ref, jt_ref, nq, k_ref, vt_ref, qs_scr, s_scr, mx_scr, m_scr, l_scr, acc_scr, chains,
                  lambda i, c: acc_scr.at[i * nc + c])

    lp = lam_ref[...]
    lam = (jnp.exp(jnp.sum(lp[0:1] * lp[1:2], axis=1, keepdims=True))
           - jnp.exp(jnp.sum(lp[2:3] * lp[3:4], axis=1, keepdims=True)) + LAMBDA_INIT)

    def finish(i, carry):
        rows = pl.ds(pl.multiple_of(i * T, T), T)
        for h in range(DIFF_HEADS):
            c = i * nc + 2 * h
            o = acc_scr[c] * (1.0 / l_scr[c]) - lam * (acc_scr[c + 1] * (1.0 / l_scr[c + 1]))
            o = o * lax.rsqrt(jnp.mean(o * o, axis=0, keepdims=True) + RMS_EPS)
            o_ref[rows, LANE * h:LANE * (h + 1)] = ((o.T * g_ref[...]) * (1.0 - LAMBDA_INIT)).astype(BF16)
        return carry

    lax.fori_loop(0, nq, finish, 0)


def _diff(qdf, kdf3, vdft, lam, norm_g, nb, seq):
    n = qdf.shape[0]
    nq = seq // T
    it, jt = _tile_schedule(nq)
    grid_spec = pltpu.PrefetchScalarGridSpec(
        num_scalar_prefetch=2,
        grid=(nb,),
        in_specs=[pl.BlockSpec((seq, 512), lambda b, it, jt: (b, 0)),
                  pl.BlockSpec((nq, T, 512), lambda b, it, jt: (b, 0, 0)),
                  pl.BlockSpec((nq, 512, T), lambda b, it, jt: (b, 0, 0)),
                  pl.BlockSpec((4, HEAD_DIM), lambda b, it, jt: (0, 0)),
                  pl.BlockSpec((1, LANE), lambda b, it, jt: (0, 0))],
        out_specs=pl.BlockSpec((seq, 512), lambda b, it, jt: (b, 0)),
        scratch_shapes=_flash_scratch(nq, 2 * DIFF_HEADS, 2 * DIFF_HEADS * DIFF_V_DIM))
    return pl.pallas_call(
        _diff_kernel,
        out_shape=jax.ShapeDtypeStruct((n, 512), BF16),
        grid_spec=grid_spec,
        compiler_params=_params(1, VMEM_LIMIT),
        name="diff",
    )(it, jt, qdf, kdf3, vdft, lam, norm_g)


def _layer_norm(z, g, b):
    mu = jnp.mean(z, axis=-1, keepdims=True)
    zc = z - mu
    var = jnp.mean(zc * zc, axis=-1, keepdims=True)
    return zc * lax.rsqrt(var + LN_EPS) * g + b


def _merge_kernel(x_ref, ocmp_ref, oslc_ref, owin_ref, yb_ref, gm_ref,
                  wa_ref, wb_ref, wo_ref, g_ref, b_ref, h_ref):
    ya = ocmp_ref[...].astype(F32) + oslc_ref[...].astype(F32) + owin_ref[...].astype(F32)
    ta = _nn(ya.astype(BF16), wa_ref[...])
    tb = _nn(yb_ref[...], wb_ref[...])
    merged = gm_ref[:, 0:D_MODEL].astype(F32) * ta + gm_ref[:, D_MODEL:2 * D_MODEL].astype(F32) * tb
    mix = _nn(merged.astype(BF16), wo_ref[...])
    h_ref[...] = _layer_norm(DEEPNORM_ALPHA * x_ref[...] + mix, g_ref[...], b_ref[...])


def _merge(x2, ocmp, oslc, owin, yb, gm, wa, wb, wo, g, b):
    n = x2.shape[0]
    row = lambda w: pl.BlockSpec((TM, w), lambda i: (i, 0))
    full = lambda a: pl.BlockSpec(a.shape, lambda i: (0,) * a.ndim)
    return pl.pallas_call(
        _merge_kernel,
        out_shape=jax.ShapeDtypeStruct((n, D_MODEL), F32),
        grid=(n // TM,),
        in_specs=[row(D_MODEL), row(512), row(512), row(512), row(512), row(2048),
                  full(wa), full(wb), full(wo), full(g), full(b)],
        out_specs=row(D_MODEL),
        compiler_params=_params(1, VMEM_LIMIT),
        name="merge",
    )(x2, ocmp, oslc, owin, yb, gm, wa, wb, wo, g, b)


def _ffn_kernel(h_ref, wgu_ref, wd_ref, g_ref, b_ref, o_ref):
    h = h_ref[...]
    hb = h.astype(BF16)
    gate = _nn(hb, wgu_ref[:, 0:FFN_HIDDEN])
    up = _nn(hb, wgu_ref[:, FFN_HIDDEN:2 * FFN_HIDDEN])
    act = (gate * jax.nn.sigmoid(gate) * up).astype(BF16)
    y = _nn(act, wd_ref[...])
    o_ref[...] = _layer_norm(DEEPNORM_ALPHA * h + y, g_ref[...], b_ref[...])


def _ffn(h1, wgu, wd, g, b):
    n = h1.shape[0]
    row = pl.BlockSpec((TM, D_MODEL), lambda i: (i, 0))
    full = lambda a: pl.BlockSpec(a.shape, lambda i: (0,) * a.ndim, pipeline_mode=pl.Buffered(1))
    return pl.pallas_call(
        _ffn_kernel,
        out_shape=jax.ShapeDtypeStruct((n, D_MODEL), F32),
        grid=(n // TM,),
        in_specs=[row, full(wgu), full(wd), full(g), full(b)],
        out_specs=row,
        compiler_params=_params(1, VMEM_LIMIT),
        name="ffn",
    )(h1, wgu, wd, g, b)


def _rope_tables(seq):
    half = ROPE_DIM // 2
    inv_freq = ROPE_THETA ** (-jnp.arange(half, dtype=F32) * 2.0 / ROPE_DIM)
    ang = jnp.arange(seq, dtype=F32)[:, None] * inv_freq[None, :]
    cos, sin = jnp.cos(ang), jnp.sin(ang)
    ones = jnp.ones((seq, HEAD_DIM - ROPE_DIM), F32)
    zeros8 = jnp.zeros((seq, half), F32)
    zeros48 = jnp.zeros((seq, HEAD_DIM - ROPE_DIM), F32)
    c64 = jnp.concatenate([cos, cos, ones], axis=1)
    sa64 = jnp.concatenate([-sin, zeros8, zeros48], axis=1)
    sb64 = jnp.concatenate([zeros8, sin, zeros48], axis=1)
    rep = lambda a: jnp.concatenate([a, a], axis=1)
    pos_blk = jnp.arange(seq, dtype=jnp.int32)[:, None] // SLC_LEN
    lane = jnp.arange(LANE, dtype=jnp.int32)[None, :]
    onehot = ((lane >= 64) & (lane < 96) & (lane - 64 == pos_blk)).astype(F32)
    return rep(c64), rep(sa64), rep(sb64), onehot


def _prep_compress(cmp_pe, cmp_w1, cmp_b1, cmp_w2):
    half = CMP_LEN // 2
    pet, peb, wt, wb, b1 = [], [], [], [], []
    for kv in range(2):
        pe = cmp_pe[kv]
        tile2 = lambda a: jnp.concatenate([a, a], axis=1).reshape(1, half * 2 * HEAD_DIM)
        pet.append(tile2(pe[:half]))
        peb.append(tile2(pe[half:]))
        w1 = cmp_w1[kv].reshape(CMP_LEN, HEAD_DIM, CMP_HIDDEN)
        z = jnp.zeros((half, HEAD_DIM, CMP_HIDDEN), w1.dtype)

        def spread(wh):
            g0 = jnp.concatenate([wh, z], axis=2)
            g1 = jnp.concatenate([z, wh], axis=2)
            return jnp.stack([g0, g1], axis=1).reshape(half * 2 * HEAD_DIM, 2 * CMP_HIDDEN)

        wt.append(spread(w1[:half]))
        wb.append(spread(w1[half:]))
        b1.append(jnp.concatenate([cmp_b1[kv], cmp_b1[kv]])[None, :])
    w2k, w2v = cmp_w2[0], cmp_w2[1]
    zk = jnp.zeros_like(w2k)
    w2k_g = jnp.stack([
        jnp.concatenate([jnp.concatenate([w2k, zk], axis=1), jnp.zeros((CMP_HIDDEN, LANE), w2k.dtype)], axis=0),
        jnp.concatenate([jnp.zeros((CMP_HIDDEN, LANE), w2k.dtype), jnp.concatenate([w2k, zk], axis=1)], axis=0),
    ])
    zv = jnp.zeros_like(w2v)
    w2v_bd = jnp.concatenate([jnp.concatenate([w2v, zv], axis=1),
                              jnp.concatenate([zv, w2v], axis=1)], axis=0)
    st = lambda xs: jnp.stack(xs)
    return (st(pet), st(peb), st(wt).astype(BF16), st(wb).astype(BF16), st(b1),
            w2k_g.astype(BF16), w2v_bd.astype(BF16))


def kernel(x, w_in, cmp_pe, cmp_w1, cmp_b1, cmp_w2, diff_lambda, diff_norm_g, w_branch_a, w_branch_b,
           w_o, ln1_g, ln1_b, w_gate_up, w_down, ln2_g, ln2_b):
    nb, seq, d = x.shape
    assert d == D_MODEL and seq % T == 0 and seq // SLC_LEN == 32 and seq // CMP_STRIDE == 128
    n = nb * seq
    x2 = x.reshape(n, d)
    cos_t, sa_t, sb_t, oh_t = _rope_tables(seq)
    (qraw, qrot, kcs, vcs, ksa, vst, kwa, vwt, qdf, kdf, vdft, gm, gnt) = _inproj(
        x2, w_in[0], cos_t, sa_t, sb_t, oh_t, seq)

    pet, peb, wt, wb, b1, w2k, w2v = _prep_compress(cmp_pe[0], cmp_w1[0], cmp_b1[0], cmp_w2[0])
    kca, vct = _compress(kcs, vcs, pet, peb, wt, wb, b1, w2k, w2v, nb, seq)

    ocmp, sel = _cmpsel(qraw, kca, vct, gnt, nb, seq)
    nt = n // T
    oslc = _slc(qrot, sel, ksa.reshape(nt, T, 256), vst, gnt, nb, seq)
    owin = _win(qrot, kwa.reshape(nt, T, 256), vwt, gnt, nb, seq)
    yb = _diff(qdf, kdf.reshape(nt, T, 512), vdft, diff_lambda[0], diff_norm_g[0][None, :], nb, seq)

    h1 = _merge(x2, ocmp, oslc, owin, yb, gm,
                w_branch_a[0].astype(BF16), w_branch_b[0].astype(BF16), w_o[0].astype(BF16),
                ln1_g[0][None, :], ln1_b[0][None, :])
    out = _ffn(h1, w_gate_up[0].astype(BF16), w_down[0].astype(BF16), ln2_g[0][None, :], ln2_b[0][None, :])
    return out.reshape(nb, seq, d)
```

```python
import functools
import math

import numpy as np
import jax
import jax.numpy as jnp
from jax import lax
from jax.experimental import pallas as pl
from jax.experimental.pallas import tpu as pltpu

D_MODEL = 1024
HEAD_DIM = 64
ROPE_DIM = HEAD_DIM // 4
ROPE_THETA = 500000.0
NSA_HEADS = 8
NSA_GROUPS = 2
NSA_HPG = NSA_HEADS // NSA_GROUPS
CMP_LEN = 32
CMP_STRIDE = 16
CMP_HIDDEN = 256
SLC_LEN = 64
SLC_TOPK = 8
WINDOW = 512
FORCE_BONUS = 1.0e4
DIFF_HEADS = 4
DIFF_V_DIM = 2 * HEAD_DIM
FFN_HIDDEN = ((8 * D_MODEL // 3 + 255) // 256) * 256
DEPTH = 1
DEEPNORM_ALPHA = (2 * DEPTH) ** 0.25
NEG = -1.0e30
LN_EPS = 1e-5
RMS_EPS = 1e-5
LAMBDA_INIT = 0.8 - 0.6 * math.exp(-0.3 * 0)
QK_SCALE = HEAD_DIM ** -0.5
QK_SCALE_LOG2 = QK_SCALE * math.log2(math.e)

NSA_Q = NSA_HEADS * HEAD_DIM
NSA_KV = NSA_GROUPS * HEAD_DIM
DIFF_QK = DIFF_HEADS * 2 * HEAD_DIM
DIFF_V = DIFF_HEADS * DIFF_V_DIM
IN_WIDTHS = (NSA_Q, NSA_KV, NSA_KV, NSA_KV, NSA_KV, NSA_KV, NSA_KV, 3 * NSA_HEADS,
             DIFF_QK, DIFF_QK, DIFF_V, 2 * D_MODEL)

LANE = 128
T = 256
TM = 512
VMEM_LIMIT = 56 * 1024 * 1024
GATE_ROWS = 32

BF16 = jnp.bfloat16
F32 = jnp.float32

_C_Q, _C_KC, _C_VC, _C_KS, _C_VS, _C_KW, _C_VW = 0, 512, 640, 768, 896, 1024, 1152
_C_QD, _C_KD, _C_VD, _C_GM, _C_GN, _C_END = 1280, 1792, 2304, 2816, 4864, 4992


def _nt(a, b):
    return lax.dot_general(a, b, (((1,), (1,)), ((), ())), preferred_element_type=F32)


def _nn(a, b):
    return jnp.dot(a, b, preferred_element_type=F32)


def _params(n_axes, vmem=None):
    return pltpu.CompilerParams(dimension_semantics=("arbitrary",) * n_axes,
                                vmem_limit_bytes=vmem)


def _inproj_kernel(x_ref, w_ref, cos_ref, sa_ref, sb_ref, oh_ref,
                   qraw_ref, qrot_ref, kc_ref, vc_ref, ksa_ref, vst_ref, kwa_ref, vwt_ref,
                   qdf_ref, kdf_ref, vdft_ref, gm_ref, gn_ref, w_scr):
    @pl.when(pl.program_id(0) == 0)
    def _():
        gn0 = _C_QD
        gn1 = gn0 + 3 * NSA_HEADS
        rows = 128

        def regroup(r, carry):
            rs = pl.ds(pl.multiple_of(r * rows, rows), rows)
            w_scr[rs, 0:_C_QD] = w_ref[rs, 0:gn0].astype(BF16)
            w_scr[rs, _C_QD:_C_GN] = w_ref[rs, gn1:gn1 + _C_GN - _C_QD].astype(BF16)
            gates = jnp.concatenate([w_ref[rs, gn0:gn1], jnp.zeros((rows, LANE - (gn1 - gn0)), F32)], axis=1)
            w_scr[rs, _C_GN:_C_END] = gates.astype(BF16)
            return carry

        lax.fori_loop(0, w_ref.shape[0] // rows, regroup, 0)

    xb = x_ref[...].astype(BF16)
    cos = cos_ref[...]
    sa = sa_ref[...]
    sb = sb_ref[...]

    def mm(c0, n):
        return _nn(xb, w_scr[:, c0:c0 + n])

    def rope(t):
        return t * cos + pltpu.roll(t, LANE - 8, 1) * sa + pltpu.roll(t, 8, 1) * sb

    t = mm(_C_Q, 512)
    for j in range(4):
        tj = t[:, LANE * j:LANE * (j + 1)]
        qraw_ref[:, LANE * j:LANE * (j + 1)] = (tj * QK_SCALE_LOG2).astype(BF16)
        qrot_ref[:, LANE * j:LANE * (j + 1)] = (rope(tj) * QK_SCALE_LOG2).astype(BF16)

    t = mm(_C_KC, 256)
    kc_ref[...] = t[:, :LANE]
    vc_ref[...] = t[:, LANE:]

    lane = lax.broadcasted_iota(jnp.int32, (TM, LANE), 1)
    tiles = [slice(T * u, T * (u + 1)) for u in range(TM // T)]

    def per_group(k, extra, out_ref):
        out_ref[:, 0:LANE] = jnp.where(lane < 64, k, extra).astype(BF16)
        out_ref[:, LANE:2 * LANE] = jnp.where(lane < 64, pltpu.roll(k, 64, 1), extra).astype(BF16)

    t = mm(_C_KS, 256)
    per_group(rope(t[:, :LANE]), oh_ref[...], ksa_ref)
    for u, rs in enumerate(tiles):
        vst_ref[u] = t[rs, LANE:].T.astype(BF16)

    t = mm(_C_KW, 256)
    per_group(rope(t[:, :LANE]), 0.0, kwa_ref)
    for u, rs in enumerate(tiles):
        vwt_ref[u] = t[rs, LANE:].T.astype(BF16)

    t = mm(_C_QD, 512)
    for j in range(4):
        qdf_ref[:, LANE * j:LANE * (j + 1)] = (rope(t[:, LANE * j:LANE * (j + 1)]) * QK_SCALE_LOG2).astype(BF16)
    t = mm(_C_KD, 512)
    for j in range(4):
        kdf_ref[:, LANE * j:LANE * (j + 1)] = rope(t[:, LANE * j:LANE * (j + 1)]).astype(BF16)
    t = mm(_C_VD, 512)
    for u, rs in enumerate(tiles):
        for j in range(4):
            vdft_ref[u, LANE * j:LANE * (j + 1), :] = t[rs, LANE * j:LANE * (j + 1)].T.astype(BF16)

    for j in range(4):
        t = mm(_C_GM + 512 * j, 512)
        gm_ref[:, 512 * j:512 * (j + 1)] = jax.nn.sigmoid(t).astype(BF16)

    t = jax.nn.sigmoid(mm(_C_GN, 128))
    for u, rs in enumerate(tiles):
        gn_ref[u] = t[rs].T[0:GATE_ROWS]


def _inproj(x2, w, cos_t, sa_t, sb_t, oh_t, seq):
    n = x2.shape[0]
    nt = n // T
    spt = seq // TM
    assert seq % TM == 0 and TM % T == 0
    assert w.shape == (D_MODEL, sum(IN_WIDTHS)) and sum(IN_WIDTHS[:7]) == _C_QD
    assert sum(IN_WIDTHS) - 3 * NSA_HEADS == _C_GN
    row = lambda w: pl.BlockSpec((TM, w), lambda i: (i, 0))
    whole = lambda a: pl.BlockSpec(a.shape, lambda i: (0, 0), pipeline_mode=pl.Buffered(1))
    tab = pl.BlockSpec((TM, LANE), lambda i: (i % spt, 0))
    tile_t = lambda r: pl.BlockSpec((TM // T, r, T), lambda i: (i, 0, 0))
    out_shape = (
        jax.ShapeDtypeStruct((n, 512), BF16),
        jax.ShapeDtypeStruct((n, 512), BF16),
        jax.ShapeDtypeStruct((n, LANE), F32),
        jax.ShapeDtypeStruct((n, LANE), F32),
        jax.ShapeDtypeStruct((n, 256), BF16),
        jax.ShapeDtypeStruct((nt, LANE, T), BF16),
        jax.ShapeDtypeStruct((n, 256), BF16),
        jax.ShapeDtypeStruct((nt, LANE, T), BF16),
        jax.ShapeDtypeStruct((n, 512), BF16),
        jax.ShapeDtypeStruct((n, 512), BF16),
        jax.ShapeDtypeStruct((nt, 512, T), BF16),
        jax.ShapeDtypeStruct((n, 2048), BF16),
        jax.ShapeDtypeStruct((nt, GATE_ROWS, T), F32),
    )
    out_specs = (row(512), row(512), row(LANE), row(LANE), row(256), tile_t(LANE), row(256),
                 tile_t(LANE), row(512), row(512), tile_t(512), row(2048), tile_t(GATE_ROWS))
    return pl.pallas_call(
        _inproj_kernel,
        out_shape=out_shape,
        grid=(n // TM,),
        in_specs=[row(D_MODEL), whole(w), tab, tab, tab, tab],
        out_specs=out_specs,
        scratch_shapes=[pltpu.VMEM((D_MODEL, _C_END), BF16)],
        compiler_params=_params(1, VMEM_LIMIT),
        name="inproj",
    )(x2, w, cos_t, sa_t, sb_t, oh_t)


def _gelu_tanh(x):
    c = math.sqrt(2.0 / math.pi)
    return x * (0.5 * (1.0 + jnp.tanh(c * (x + 0.044715 * (x * x * x)))))


def _compress_kernel(kf_ref, vf_ref, pet_ref, peb_ref, wt_ref, wb_ref, b1_ref, w2k_ref, w2v_ref,
                     kca_ref, vct_ref):
    nchunk = kf_ref.shape[0] // CMP_STRIDE

    def hidden(x_ref, kv):
        a = jnp.zeros((nchunk, 2 * CMP_HIDDEN), F32)
        b = jnp.zeros((nchunk, 2 * CMP_HIDDEN), F32)
        for p in range(0, CMP_STRIDE, 2):
            x = jnp.concatenate([x_ref[pl.ds(p, nchunk, stride=CMP_STRIDE), :],
                                 x_ref[pl.ds(p + 1, nchunk, stride=CMP_STRIDE), :]], axis=1)
            c0, c1 = LANE * p, LANE * (p + 2)
            a = a + _nn((x + pet_ref[kv, :, c0:c1]).astype(BF16), wt_ref[kv, c0:c1, :])
            b = b + _nn((x + peb_ref[kv, :, c0:c1]).astype(BF16), wb_ref[kv, c0:c1, :])
        h = a + pltpu.roll(b, nchunk - 1, 0) + b1_ref[kv]
        return _gelu_tanh(h).astype(BF16)

    hk = hidden(kf_ref, 0)
    for g in range(2):
        kca_ref[g] = _nn(hk, w2k_ref[g]).astype(BF16)
    hv = hidden(vf_ref, 1)
    vct_ref[...] = _nn(hv, w2v_ref[...]).T.astype(BF16)


def _compress(kf, vf, pet, peb, wt, wb, b1, w2k, w2v, nb, seq):
    full = lambda a: pl.BlockSpec(a.shape, lambda b: (0,) * a.ndim)
    return pl.pallas_call(
        _compress_kernel,
        out_shape=(jax.ShapeDtypeStruct((nb, 2, 128, LANE), BF16),
                   jax.ShapeDtypeStruct((nb, LANE, 128), BF16)),
        grid=(nb,),
        in_specs=[pl.BlockSpec((seq, LANE), lambda b: (b, 0)),
                  pl.BlockSpec((seq, LANE), lambda b: (b, 0)),
                  full(pet), full(peb), full(wt), full(wb), full(b1), full(w2k), full(w2v)],
        out_specs=(pl.BlockSpec((None, 2, 128, LANE), lambda b: (b, 0, 0, 0)),
                   pl.BlockSpec((None, LANE, 128), lambda b: (b, 0, 0))),
        compiler_params=_params(1, VMEM_LIMIT),
        name="compress",
    )(kf, vf, pet, peb, wt, wb, b1, w2k, w2v)


def _cmpsel_kernel(q_ref, kca_ref, vct_ref, gnt_ref, ocmp_ref, sel_ref, s_scr):
    nq = q_ref.shape[0] // T
    lane = lax.broadcasted_iota(jnp.int32, (T, LANE), 1)
    crow = lax.broadcasted_iota(jnp.int32, (128, T), 0)
    tcol0 = lax.broadcasted_iota(jnp.int32, (128, T), 1)
    tlane = lax.broadcasted_iota(jnp.int32, (1, T), 1)

    jj = lax.broadcasted_iota(jnp.int32, (32, 128), 0) * SLC_LEN
    cc = lax.broadcasted_iota(jnp.int32, (32, 128), 1) * CMP_STRIDE
    ov = jnp.maximum(jnp.minimum(cc + CMP_LEN, jj + SLC_LEN) - jnp.maximum(cc, jj), 0)
    ovt = (ov.astype(F32) * (1.0 / CMP_LEN)).astype(BF16)

    jrow = lax.broadcasted_iota(jnp.int32, (32, T), 0)
    jrow8 = lax.broadcasted_iota(jnp.int32, (8, T), 0)
    tblk0 = lax.broadcasted_iota(jnp.int32, (32, T), 1)

    def scores(i, buf):
        rows = pl.ds(pl.multiple_of(i * T, T), T)
        for hp in range(NSA_HEADS // 2):
            g = hp // (NSA_HPG // 2)
            qp = q_ref[rows, LANE * hp:LANE * (hp + 1)].astype(F32)
            qs = (jnp.where(lane < 64, qp, 0.0), jnp.where(lane < 64, pltpu.roll(qp, 64, 1), 0.0))
            for par in range(2):
                s_scr[buf, 2 * hp + par] = _nt(kca_ref[g], qs[par].astype(BF16))

    def attend_select(i, buf):
        t0 = i * T
        rows = pl.ds(pl.multiple_of(t0, T), T)
        cmask = (crow * CMP_STRIDE + (CMP_LEN - 1)) <= tcol0 + t0
        live = jnp.where(tlane + t0 >= (CMP_LEN - 1), 1.0, 0.0)
        tblk = (tblk0 + t0) // SLC_LEN
        valid = jrow <= tblk
        forced = (jrow == 0) | (jrow == tblk) | (jrow == tblk - 1)
        bonus = jnp.where(forced, FORCE_BONUS, 0.0)
        for g in range(NSA_GROUPS):
            vt = vct_ref[64 * g:64 * (g + 1), :]
            psum = jnp.zeros((128, T), F32)
            for hp in range(NSA_HPG // 2):
                outs = []
                for par in range(2):
                    h = NSA_HPG * g + 2 * hp + par
                    s = jnp.where(cmask, s_scr[buf, h], NEG)
                    m = jnp.max(s, axis=0, keepdims=True)
                    e = jnp.exp2(s - m)
                    p = e * (live / jnp.sum(e, axis=0, keepdims=True))
                    psum = psum + p
                    outs.append(_nn(vt, p.astype(BF16)) * gnt_ref[i, 3 * h:3 * h + 1, :])
                ocmp_ref[rows, LANE * (2 * g + hp):LANE * (2 * g + hp + 1)] = (
                    jnp.concatenate(outs, axis=0).T.astype(BF16))

            p_hi = psum.astype(BF16)
            p_lo = (psum - p_hi.astype(F32)).astype(BF16)
            pslc = _nn(ovt, p_hi) + _nn(ovt, p_lo)
            pri = jnp.where(valid, pslc + bonus, -1.0)
            rank = [jnp.zeros((8, T), F32) for _ in range(4)]
            for r in range(32):
                row = pri[r:r + 1, :]
                for a in range(4):
                    pa = pri[8 * a:8 * (a + 1), :]
                    if 8 * a > r:
                        ahead = jnp.where(row >= pa, 1.0, 0.0)
                    elif 8 * a + 7 < r:
                        ahead = jnp.where(row > pa, 1.0, 0.0)
                    else:
                        ahead = jnp.where(jrow8 + 8 * a > r, jnp.where(row >= pa, 1.0, 0.0),
                                          jnp.where(row > pa, 1.0, 0.0))
                    rank[a] = rank[a] + ahead
            rank = jnp.concatenate(rank, axis=0)
            selneg = jnp.where(rank < float(SLC_TOPK), 0.0, NEG)
            pad = jnp.concatenate([jnp.zeros((64, T), F32), selneg, jnp.zeros((32, T), F32)], axis=0)
            sel_ref[rows, LANE * g:LANE * (g + 1)] = pad.T.astype(BF16)

    scores(0, 0)

    def body(u, carry):
        scores(2 * u + 1, 1)
        attend_select(2 * u, 0)
        scores(jnp.minimum(2 * u + 2, nq - 1), 0)
        attend_select(2 * u + 1, 1)
        return carry

    lax.fori_loop(0, nq // 2, body, 0)


def _cmpsel(qraw, kca, vct, gnt, nb, seq):
    n = qraw.shape[0]
    nq = seq // T
    assert nq % 2 == 0
    return pl.pallas_call(
        _cmpsel_kernel,
        out_shape=(jax.ShapeDtypeStruct((n, 512), BF16), jax.ShapeDtypeStruct((n, 256), BF16)),
        grid=(nb,),
        in_specs=[pl.BlockSpec((seq, 512), lambda b: (b, 0)),
                  pl.BlockSpec((None, 2, 128, LANE), lambda b: (b, 0, 0, 0)),
                  pl.BlockSpec((None, LANE, 128), lambda b: (b, 0, 0)),
                  pl.BlockSpec((nq, GATE_ROWS, T), lambda b: (b, 0, 0))],
        out_specs=(pl.BlockSpec((seq, 512), lambda b: (b, 0)),
                   pl.BlockSpec((seq, 256), lambda b: (b, 0))),
        scratch_shapes=[pltpu.VMEM((2, NSA_HEADS, 128, T), F32)],
        compiler_params=_params(1, VMEM_LIMIT),
        name="cmpsel",
    )(qraw, kca, vct, gnt)


def _split_heads(q_ref, extras, qs_scr, base=0):
    lane = lax.broadcasted_iota(jnp.int32, (T, LANE), 1)
    for hp in range(NSA_HEADS // 2):
        extra = extras[hp // (NSA_HPG // 2)]
        qp = q_ref[:, LANE * hp:LANE * (hp + 1)].astype(F32)
        qs_scr[base + 2 * hp] = jnp.where(lane < 64, qp, extra).astype(BF16)
        qs_scr[base + 2 * hp + 1] = jnp.where(lane < 64, pltpu.roll(qp, 64, 1), extra).astype(BF16)


SM_ROWS = 32


RING = 4


def _tile_schedule(nq):
    below = [(i, j) for j in range(nq) for i in range(j + 1, nq)]
    assert len(below) == (RING - 1) * nq + RING
    pairs = []
    for i in range(nq):
        pairs += [(i, i)] + below[(RING - 1) * i:(RING - 1) * (i + 1)]
    pairs += below[(RING - 1) * nq:]
    return (jnp.asarray([p[0] for p in pairs], jnp.int32), jnp.asarray([p[1] for p in pairs], jnp.int32))


def _flash_scratch(nq, nc, acc_rows):
    return [pltpu.VMEM((nq * nc, T, LANE), BF16),
            pltpu.VMEM((RING, nc, T, T), F32),
            pltpu.VMEM((RING, nc, 1, T), F32),
            pltpu.VMEM((nq * nc, 1, T), F32),
            pltpu.VMEM((nq * nc, 1, T), F32),
            pltpu.VMEM((nq * acc_rows // LANE, LANE, T), F32)]


def _causal_flash(it_ref, jt_ref, nq, k_ref, vt_ref, qs_scr, s_scr, mx_scr, m_scr, l_scr, acc_scr, chains, acc_of):
    nc = len(chains)
    nsets = it_ref.shape[0]
    assert nsets % RING == 0 and nsets >= 2 * RING
    m_scr[...] = jnp.full(m_scr.shape, NEG, F32)
    l_scr[...] = jnp.zeros(l_scr.shape, F32)
    acc_scr[...] = jnp.zeros(acc_scr.shape, F32)
    krow = lax.broadcasted_iota(jnp.int32, (T, T), 0)
    qcol = lax.broadcasted_iota(jnp.int32, (T, T), 1)
    ones = jnp.ones((16, T), BF16)

    def scores(n, buf):
        i, j = it_ref[n], jt_ref[n]
        if buf == 0:
            mask = krow <= qcol + jnp.where(i == j, 0, T)
        for c, (kl, _, _) in enumerate(chains):
            s = _nt(k_ref[j, :, kl:kl + LANE], qs_scr[i * nc + c])
            if buf == 0:
                s = jnp.where(mask, s, NEG)
            s_scr[buf, c] = s
            mx_scr[buf, c] = jnp.max(s, axis=0, keepdims=True)

    def softmax_pv(n, buf):
        i, j = it_ref[n], jt_ref[n]
        for c, (_, vr, dv) in enumerate(chains):
            m_prev = m_scr[i * nc + c]
            m_new = jnp.maximum(m_prev, mx_scr[buf, c])
            alpha = jnp.exp2(m_prev - m_new)
            parts = [jnp.exp2(s_scr[buf, c, r:r + SM_ROWS, :] - m_new).astype(BF16)
                     for r in range(0, T, SM_ROWS)]
            vt = jnp.concatenate([vt_ref[j, vr:vr + dv, :], ones], axis=0)
            o = _nn(vt, jnp.concatenate(parts, axis=0))
            l_scr[i * nc + c] = alpha * l_scr[i * nc + c] + o[dv:dv + 1]
            m_scr[i * nc + c] = m_new
            acc = acc_of(i, c)
            acc[...] = acc[...] * alpha + o[0:dv]

    scores(0, 0)

    def body(u, carry):
        n = RING * u
        for r in range(RING):
            scores(n + r + 1, (r + 1) % RING)
            softmax_pv(n + r, r)
        return carry

    lax.fori_loop(0, nsets // RING - 1, body, 0)
    n = nsets - RING
    for r in range(RING):
        if r + 1 < RING:
            scores(n + r + 1, r + 1)
        softmax_pv(n + r, r)


def _slc_kernel(it_ref, jt_ref, q_ref, sel_ref, k_ref, vt_ref, gnt_ref, o_ref,
                qs_scr, s_scr, mx_scr, m_scr, l_scr, acc_scr):
    nq = q_ref.shape[0] // T
    nh = NSA_HEADS

    def prep(i, carry):
        rows = pl.ds(pl.multiple_of(i * T, T), T)
        sel = sel_ref[rows, :].astype(F32)
        _split_heads(q_ref.at[rows, :], [sel[:, 0:LANE], sel[:, LANE:2 * LANE]], qs_scr, i * nh)
        return carry

    lax.fori_loop(0, nq, prep, 0)
    chains = [(LANE * (h // NSA_HPG), 64 * (h // NSA_HPG), 64) for h in range(nh)]
    acc_of = lambda i, h: acc_scr.at[i * (nh // 2) + h // 2, 64 * (h % 2):64 * (h % 2 + 1), :]
    _causal_flash(it_ref, jt_ref, nq, k_ref, vt_ref, qs_scr, s_scr, mx_scr, m_scr, l_scr, acc_scr, chains, acc_of)

    def finish(i, carry):
        rows = pl.ds(pl.multiple_of(i * T, T), T)
        for hp in range(nh // 2):
            scale = [gnt_ref[i, 3 * h + 1:3 * h + 2, :] / l_scr[i * nh + h] for h in (2 * hp, 2 * hp + 1)]
            inv = jnp.concatenate([jnp.broadcast_to(scale[0], (64, T)), jnp.broadcast_to(scale[1], (64, T))],
                                  axis=0)
            o_ref[rows, LANE * hp:LANE * (hp + 1)] = (acc_scr[i * (nh // 2) + hp] * inv).T.astype(BF16)
        return carry

    lax.fori_loop(0, nq, finish, 0)


def _slc(qrot, sel, ksa3, vst, gnt3, nb, seq):
    n = qrot.shape[0]
    nq = seq // T
    it, jt = _tile_schedule(nq)
    grid_spec = pltpu.PrefetchScalarGridSpec(
        num_scalar_prefetch=2,
        grid=(nb,),
        in_specs=[pl.BlockSpec((seq, 512), lambda b, it, jt: (b, 0)),
                  pl.BlockSpec((seq, 256), lambda b, it, jt: (b, 0)),
                  pl.BlockSpec((nq, T, 256), lambda b, it, jt: (b, 0, 0)),
                  pl.BlockSpec((nq, LANE, T), lambda b, it, jt: (b, 0, 0)),
                  pl.BlockSpec((nq, GATE_ROWS, T), lambda b, it, jt: (b, 0, 0))],
        out_specs=pl.BlockSpec((seq, 512), lambda b, it, jt: (b, 0)),
        scratch_shapes=_flash_scratch(nq, NSA_HEADS, NSA_Q))
    return pl.pallas_call(
        _slc_kernel,
        out_shape=jax.ShapeDtypeStruct((n, 512), BF16),
        grid_spec=grid_spec,
        compiler_params=_params(1, VMEM_LIMIT),
        name="slc",
    )(it, jt, qrot, sel, ksa3, vst, gnt3)


def _win_kernel(q_ref, k_ref, vt_ref, gnt_ref, o_ref, qs_scr, s_scr, mx_scr, acc_scr):
    i = pl.program_id(1)
    _split_heads(q_ref, [0.0, 0.0], qs_scr)
    krow = lax.broadcasted_iota(jnp.int32, (T, T), 0)
    qcol = lax.broadcasted_iota(jnp.int32, (T, T), 1)
    span = WINDOW // T
    masks = {0: krow <= qcol, span: (krow > qcol) & (i >= span)}
    ones = jnp.ones((16, T), BF16)

    def scores(hp, buf):
        for par in range(2):
            h = 2 * hp + par
            g = h // NSA_HPG
            mx = None
            for d in range(span + 1):
                j = i - d
                s = _nt(k_ref[jnp.maximum(j, 0), :, LANE * g:LANE * (g + 1)], qs_scr[h])
                if d in masks:
                    s = jnp.where(masks[d], s, NEG)
                else:
                    s = s + jnp.where(j >= 0, 0.0, NEG)
                s_scr[buf, par, d] = s
                md = jnp.max(s, axis=0, keepdims=True)
                mx = md if mx is None else jnp.maximum(mx, md)
            mx_scr[buf, par] = mx

    def softmax_pv(hp, buf):
        for par in range(2):
            g = (2 * hp + par) // NSA_HPG
            m = mx_scr[buf, par]
            acc = jnp.zeros((64 + 16, T), F32)
            for d in range(span + 1):
                parts = [jnp.exp2(s_scr[buf, par, d, r:r + SM_ROWS, :] - m).astype(BF16)
                         for r in range(0, T, SM_ROWS)]
                vt = jnp.concatenate([vt_ref[jnp.maximum(i - d, 0), 64 * g:64 * (g + 1), :], ones], axis=0)
                acc = acc + _nn(vt, jnp.concatenate(parts, axis=0))
            gate = gnt_ref[3 * (2 * hp + par) + 2:3 * (2 * hp + par) + 3, :]
            acc_scr[hp, 64 * par:64 * (par + 1), :] = acc[0:64] * (gate / acc[64:65])

    npairs = NSA_HEADS // 2
    scores(0, 0)
    for hp in range(npairs):
        if hp + 1 < npairs:
            scores(hp + 1, (hp + 1) % 2)
        softmax_pv(hp, hp % 2)
    for hp in range(npairs):
        o_ref[:, LANE * hp:LANE * (hp + 1)] = acc_scr[hp].T.astype(BF16)


def _win(qrot, kwa3, vwt, gnt, nb, seq):
    n = qrot.shape[0]
    nq = seq // T
    return pl.pallas_call(
        _win_kernel,
        out_shape=jax.ShapeDtypeStruct((n, 512), BF16),
        grid=(nb, nq),
        in_specs=[pl.BlockSpec((T, 512), lambda b, i: (b * nq + i, 0)),
                  pl.BlockSpec((nq, T, 256), lambda b, i: (b, 0, 0)),
                  pl.BlockSpec((nq, LANE, T), lambda b, i: (b, 0, 0)),
                  pl.BlockSpec((None, GATE_ROWS, T), lambda b, i: (b * nq + i, 0, 0))],
        out_specs=pl.BlockSpec((T, 512), lambda b, i: (b * nq + i, 0)),
        scratch_shapes=[pltpu.VMEM((NSA_HEADS, T, LANE), BF16),
                        pltpu.VMEM((2, 2, WINDOW // T + 1, T, T), F32),
                        pltpu.VMEM((2, 2, 1, T), F32),
                        pltpu.VMEM((NSA_HEADS // 2, LANE, T), F32)],
        compiler_params=_params(2, VMEM_LIMIT),
        name="win",
    )(qrot, kwa3, vwt, gnt)


def _diff_kernel(it_ref, jt_ref, q_ref, k_ref, vt_ref, lam_ref, g_ref, o_ref,
                 qs_scr, s_scr, mx_scr, m_scr, l_scr, acc_scr):
    nq = q_ref.shape[0] // T
    nc = 2 * DIFF_HEADS
    lane = lax.broadcasted_iota(jnp.int32, (T, LANE), 1)

    def prep(i, carry):
        rows = pl.ds(pl.multiple_of(i * T, T), T)
        for h in range(DIFF_HEADS):
            q = q_ref[rows, LANE * h:LANE * (h + 1)].astype(F32)
            qs_scr[i * nc + 2 * h] = jnp.where(lane < 64, q, 0.0).astype(BF16)
            qs_scr[i * nc + 2 * h + 1] = jnp.where(lane < 64, 0.0, q).astype(BF16)
        return carry

    lax.fori_loop(0, nq, prep, 0)
    chains = [(LANE * (c // 2), LANE * (c // 2), LANE) for c in range(nc)]
    _causal_flash(it_ref, jt_ref, nq, k_ref, vt_ref, qs_scr, s_scr, mx_scr, m_scr, l_scr, acc_scr, chains,
                  lambda i, c: acc_scr.at[i * nc + c])

    lp = lam_ref[...]
    lam = (jnp.exp(jnp.sum(lp[0:1] * lp[1:2], axis=1, keepdims=True))
           - jnp.exp(jnp.sum(lp[2:3] * lp[3:4], axis=1, keepdims=True)) + LAMBDA_INIT)

    def finish(i, carry):
        rows = pl.ds(pl.multiple_of(i * T, T), T)
        for h in range(DIFF_HEADS):
            c = i * nc + 2 * h
            o = acc_scr[c] * (1.0 / l_scr[c]) - lam * (acc_scr[c + 1] * (1.0 / l_scr[c + 1]))
            o = o * lax.rsqrt(jnp.mean(o * o, axis=0, keepdims=True) + RMS_EPS)
            o_ref[rows, LANE * h:LANE * (h + 1)] = ((o.T * g_ref[...]) * (1.0 - LAMBDA_INIT)).astype(BF16)
        return carry

    lax.fori_loop(0, nq, finish, 0)


def _diff(qdf, kdf3, vdft, lam, norm_g, nb, seq):
    n = qdf.shape[0]
    nq = seq // T
    it, jt = _tile_schedule(nq)
    grid_spec = pltpu.PrefetchScalarGridSpec(
        num_scalar_prefetch=2,
        grid=(nb,),
        in_specs=[pl.BlockSpec((seq, 512), lambda b, it, jt: (b, 0)),
                  pl.BlockSpec((nq, T, 512), lambda b, it, jt: (b, 0, 0)),
                  pl.BlockSpec((nq, 512, T), lambda b, it, jt: (b, 0, 0)),
                  pl.BlockSpec((4, HEAD_DIM), lambda b, it, jt: (0, 0)),
                  pl.BlockSpec((1, LANE), lambda b, it, jt: (0, 0))],
        out_specs=pl.BlockSpec((seq, 512), lambda b, it, jt: (b, 0)),
        scratch_shapes=_flash_scratch(nq, 2 * DIFF_HEADS, 2 * DIFF_HEADS * DIFF_V_DIM))
    return pl.pallas_call(
        _diff_kernel,
        out_shape=jax.ShapeDtypeStruct((n, 512), BF16),
        grid_spec=grid_spec,
        compiler_params=_params(1, VMEM_LIMIT),
        name="diff",
    )(it, jt, qdf, kdf3, vdft, lam, norm_g)


def _layer_norm(z, g, b):
    mu = jnp.mean(z, axis=-1, keepdims=True)
    zc = z - mu
    var = jnp.mean(zc * zc, axis=-1, keepdims=True)
    return zc * lax.rsqrt(var + LN_EPS) * g + b


def _merge_kernel(x_ref, ocmp_ref, oslc_ref, owin_ref, yb_ref, gm_ref,
                  wa_ref, wb_ref, wo_ref, g_ref, b_ref, h_ref):
    ya = ocmp_ref[...].astype(F32) + oslc_ref[...].astype(F32) + owin_ref[...].astype(F32)
    ta = _nn(ya.astype(BF16), wa_ref[...])
    tb = _nn(yb_ref[...], wb_ref[...])
    merged = gm_ref[:, 0:D_MODEL].astype(F32) * ta + gm_ref[:, D_MODEL:2 * D_MODEL].astype(F32) * tb
    mix = _nn(merged.astype(BF16), wo_ref[...])
    h_ref[...] = _layer_norm(DEEPNORM_ALPHA * x_ref[...] + mix, g_ref[...], b_ref[...])


def _merge(x2, ocmp, oslc, owin, yb, gm, wa, wb, wo, g, b):
    n = x2.shape[0]
    row = lambda w: pl.BlockSpec((TM, w), lambda i: (i, 0))
    full = lambda a: pl.BlockSpec(a.shape, lambda i: (0,) * a.ndim)
    return pl.pallas_call(
        _merge_kernel,
        out_shape=jax.ShapeDtypeStruct((n, D_MODEL), F32),
        grid=(n // TM,),
        in_specs=[row(D_MODEL), row(512), row(512), row(512), row(512), row(2048),
                  full(wa), full(wb), full(wo), full(g), full(b)],
        out_specs=row(D_MODEL),
        compiler_params=_params(1, VMEM_LIMIT),
        name="merge",
    )(x2, ocmp, oslc, owin, yb, gm, wa, wb, wo, g, b)


def _ffn_kernel(h_ref, wgu_ref, wd_ref, g_ref, b_ref, o_ref):
    h = h_ref[...]
    hb = h.astype(BF16)
    gate = _nn(hb, wgu_ref[:, 0:FFN_HIDDEN])
    up = _nn(hb, wgu_ref[:, FFN_HIDDEN:2 * FFN_HIDDEN])
    act = (gate * jax.nn.sigmoid(gate) * up).astype(BF16)
    y = _nn(act, wd_ref[...])
    o_ref[...] = _layer_norm(DEEPNORM_ALPHA * h + y, g_ref[...], b_ref[...])


def _ffn(h1, wgu, wd, g, b):
    n = h1.shape[0]
    row = pl.BlockSpec((TM, D_MODEL), lambda i: (i, 0))
    full = lambda a: pl.BlockSpec(a.shape, lambda i: (0,) * a.ndim, pipeline_mode=pl.Buffered(1))
    return pl.pallas_call(
        _ffn_kernel,
        out_shape=jax.ShapeDtypeStruct((n, D_MODEL), F32),
        grid=(n // TM,),
        in_specs=[row, full(wgu), full(wd), full(g), full(b)],
        out_specs=row,
        compiler_params=_params(1, VMEM_LIMIT),
        name="ffn",
    )(h1, wgu, wd, g, b)


def _rope_tables(seq):
    half = ROPE_DIM // 2
    inv_freq = ROPE_THETA ** (-jnp.arange(half, dtype=F32) * 2.0 / ROPE_DIM)
    ang = jnp.arange(seq, dtype=F32)[:, None] * inv_freq[None, :]
    cos, sin = jnp.cos(ang), jnp.sin(ang)
    ones = jnp.ones((seq, HEAD_DIM - ROPE_DIM), F32)
    zeros8 = jnp.zeros((seq, half), F32)
    zeros48 = jnp.zeros((seq, HEAD_DIM - ROPE_DIM), F32)
    c64 = jnp.concatenate([cos, cos, ones], axis=1)
    sa64 = jnp.concatenate([-sin, zeros8, zeros48], axis=1)
    sb64 = jnp.concatenate([zeros8, sin, zeros48], axis=1)
    rep = lambda a: jnp.concatenate([a, a], axis=1)
    pos_blk = jnp.arange(seq, dtype=jnp.int32)[:, None] // SLC_LEN
    lane = jnp.arange(LANE, dtype=jnp.int32)[None, :]
    onehot = ((lane >= 64) & (lane < 96) & (lane - 64 == pos_blk)).astype(F32)
    return rep(c64), rep(sa64), rep(sb64), onehot


def _prep_compress(cmp_pe, cmp_w1, cmp_b1, cmp_w2):
    half = CMP_LEN // 2
    pet, peb, wt, wb, b1 = [], [], [], [], []
    for kv in range(2):
        pe = cmp_pe[kv]
        tile2 = lambda a: jnp.concatenate([a, a], axis=1).reshape(1, half * 2 * HEAD_DIM)
        pet.append(tile2(pe[:half]))
        peb.append(tile2(pe[half:]))
        w1 = cmp_w1[kv].reshape(CMP_LEN, HEAD_DIM, CMP_HIDDEN)
        z = jnp.zeros((half, HEAD_DIM, CMP_HIDDEN), w1.dtype)

        def spread(wh):
            g0 = jnp.concatenate([wh, z], axis=2)
            g1 = jnp.concatenate([z, wh], axis=2)
            return jnp.stack([g0, g1], axis=1).reshape(half * 2 * HEAD_DIM, 2 * CMP_HIDDEN)

        wt.append(spread(w1[:half]))
        wb.append(spread(w1[half:]))
        b1.append(jnp.concatenate([cmp_b1[kv], cmp_b1[kv]])[None, :])
    w2k, w2v = cmp_w2[0], cmp_w2[1]
    zk = jnp.zeros_like(w2k)
    w2k_g = jnp.stack([
        jnp.concatenate([jnp.concatenate([w2k, zk], axis=1), jnp.zeros((CMP_HIDDEN, LANE), w2k.dtype)], axis=0),
        jnp.concatenate([jnp.zeros((CMP_HIDDEN, LANE), w2k.dtype), jnp.concatenate([w2k, zk], axis=1)], axis=0),
    ])
    zv = jnp.zeros_like(w2v)
    w2v_bd = jnp.concatenate([jnp.concatenate([w2v, zv], axis=1),
                              jnp.concatenate([zv, w2v], axis=1)], axis=0)
    st = lambda xs: jnp.stack(xs)
    return (st(pet), st(peb), st(wt).astype(BF16), st(wb).astype(BF16), st(b1),
            w2k_g.astype(BF16), w2v_bd.astype(BF16))


def kernel(x, w_in, cmp_pe, cmp_w1, cmp_b1, cmp_w2, diff_lambda, diff_norm_g, w_branch_a, w_branch_b,
           w_o, ln1_g, ln1_b, w_gate_up, w_down, ln2_g, ln2_b):
    nb, seq, d = x.shape
    assert d == D_MODEL and seq % T == 0 and seq // SLC_LEN == 32 and seq // CMP_STRIDE == 128
    n = nb * seq
    x2 = x.reshape(n, d)
    cos_t, sa_t, sb_t, oh_t = _rope_tables(seq)
    (qraw, qrot, kcs, vcs, ksa, vst, kwa, vwt, qdf, kdf, vdft, gm, gnt) = _inproj(
        x2, w_in.reshape(w_in.shape[1:]), cos_t, sa_t, sb_t, oh_t, seq)

    pet, peb, wt, wb, b1, w2k, w2v = _prep_compress(cmp_pe[0], cmp_w1[0], cmp_b1[0], cmp_w2[0])
    kca, vct = _compress(kcs, vcs, pet, peb, wt, wb, b1, w2k, w2v, nb, seq)

    ocmp, sel = _cmpsel(qraw, kca, vct, gnt, nb, seq)
    nt = n // T
    oslc = _slc(qrot, sel, ksa.reshape(nt, T, 256), vst, gnt, nb, seq)
    owin = _win(qrot, kwa.reshape(nt, T, 256), vwt, gnt, nb, seq)
    yb = _diff(qdf, kdf.reshape(nt, T, 512), vdft, diff_lambda[0], diff_norm_g[0][None, :], nb, seq)

    h1 = _merge(x2, ocmp, oslc, owin, yb, gm,
                w_branch_a[0].astype(BF16), w_branch_b[0].astype(BF16), w_o[0].astype(BF16),
                ln1_g[0][None, :], ln1_b[0][None, :])
    out = _ffn(h1, w_gate_up[0].astype(BF16), w_down[0].astype(BF16), ln2_g[0][None, :], ln2_b[0][None, :])
    return out.reshape(nb, seq, d)
```

```python
import functools
import math

import numpy as np
import jax
import jax.numpy as jnp
from jax import lax
from jax.experimental import pallas as pl
from jax.experimental.pallas import tpu as pltpu

D_MODEL = 1024
HEAD_DIM = 64
ROPE_DIM = HEAD_DIM // 4
ROPE_THETA = 500000.0
NSA_HEADS = 8
NSA_GROUPS = 2
NSA_HPG = NSA_HEADS // NSA_GROUPS
CMP_LEN = 32
CMP_STRIDE = 16
CMP_HIDDEN = 256
SLC_LEN = 64
SLC_TOPK = 8
WINDOW = 512
FORCE_BONUS = 1.0e4
DIFF_HEADS = 4
DIFF_V_DIM = 2 * HEAD_DIM
FFN_HIDDEN = ((8 * D_MODEL // 3 + 255) // 256) * 256
DEPTH = 1
DEEPNORM_ALPHA = (2 * DEPTH) ** 0.25
NEG = -1.0e30
LN_EPS = 1e-5
RMS_EPS = 1e-5
LAMBDA_INIT = 0.8 - 0.6 * math.exp(-0.3 * 0)
QK_SCALE = HEAD_DIM ** -0.5
QK_SCALE_LOG2 = QK_SCALE * math.log2(math.e)

NSA_Q = NSA_HEADS * HEAD_DIM
NSA_KV = NSA_GROUPS * HEAD_DIM
DIFF_QK = DIFF_HEADS * 2 * HEAD_DIM
DIFF_V = DIFF_HEADS * DIFF_V_DIM
IN_WIDTHS = (NSA_Q, NSA_KV, NSA_KV, NSA_KV, NSA_KV, NSA_KV, NSA_KV, 3 * NSA_HEADS,
             DIFF_QK, DIFF_QK, DIFF_V, 2 * D_MODEL)

LANE = 128
T = 256
TM = 512
VMEM_LIMIT = 56 * 1024 * 1024
GATE_ROWS = 32

BF16 = jnp.bfloat16
F32 = jnp.float32

_C_Q, _C_KC, _C_VC, _C_KS, _C_VS, _C_KW, _C_VW = 0, 512, 640, 768, 896, 1024, 1152
_C_QD, _C_KD, _C_VD, _C_GM, _C_GN, _C_END = 1280, 1792, 2304, 2816, 4864, 4992


def _nt(a, b):
    return lax.dot_general(a, b, (((1,), (1,)), ((), ())), preferred_element_type=F32)


def _nn(a, b):
    return jnp.dot(a, b, preferred_element_type=F32)


def _params(n_axes, vmem=None):
    return pltpu.CompilerParams(dimension_semantics=("arbitrary",) * n_axes,
                                vmem_limit_bytes=vmem)


def _inproj_kernel(x_ref, w_ref, cos_ref, sa_ref, sb_ref, oh_ref,
                   qraw_ref, qrot_ref, kc_ref, vc_ref, ksa_ref, vst_ref, kwa_ref, vwt_ref,
                   qdf_ref, kdf_ref, vdft_ref, gm_ref, gn_ref, w_scr):
    @pl.when(pl.program_id(0) == 0)
    def _():
        gn0 = _C_QD
        gn1 = gn0 + 3 * NSA_HEADS
        rows = 256

        def regroup(r, carry):
            dst = pl.ds(pl.multiple_of(r * rows, rows), rows)
            src = pl.ds(pl.multiple_of(jnp.where(r * rows < gn0, r * rows, r * rows + gn1 - gn0), 8), rows)
            w_scr[dst, :] = w_ref[src, :].astype(BF16)
            return carry

        lax.fori_loop(0, _C_GN // rows, regroup, 0)
        gates = jnp.concatenate([w_ref[gn0:gn1, :], jnp.zeros((LANE - (gn1 - gn0), D_MODEL), F32)], axis=0)
        w_scr[_C_GN:_C_END, :] = gates.astype(BF16)

    xb = x_ref[...].astype(BF16)
    cos = cos_ref[...]
    sa = sa_ref[...]
    sb = sb_ref[...]

    def mm(c0, n):
        return _nt(xb, w_scr[c0:c0 + n, :])

    def rope(t):
        return t * cos + pltpu.roll(t, LANE - 8, 1) * sa + pltpu.roll(t, 8, 1) * sb

    t = mm(_C_Q, 512)
    for j in range(4):
        tj = t[:, LANE * j:LANE * (j + 1)]
        qraw_ref[:, LANE * j:LANE * (j + 1)] = (tj * QK_SCALE_LOG2).astype(BF16)
        qrot_ref[:, LANE * j:LANE * (j + 1)] = (rope(tj) * QK_SCALE_LOG2).astype(BF16)

    t = mm(_C_KC, 256)
    kc_ref[...] = t[:, :LANE]
    vc_ref[...] = t[:, LANE:]

    lane = lax.broadcasted_iota(jnp.int32, (TM, LANE), 1)
    tiles = [slice(T * u, T * (u + 1)) for u in range(TM // T)]

    def per_group(k, extra, out_ref):
        out_ref[:, 0:LANE] = jnp.where(lane < 64, k, extra).astype(BF16)
        out_ref[:, LANE:2 * LANE] = jnp.where(lane < 64, pltpu.roll(k, 64, 1), extra).astype(BF16)

    t = mm(_C_KS, 256)
    per_group(rope(t[:, :LANE]), oh_ref[...], ksa_ref)
    for u, rs in enumerate(tiles):
        vst_ref[u] = t[rs, LANE:].T.astype(BF16)

    t = mm(_C_KW, 256)
    per_group(rope(t[:, :LANE]), 0.0, kwa_ref)
    for u, rs in enumerate(tiles):
        vwt_ref[u] = t[rs, LANE:].T.astype(BF16)

    t = mm(_C_QD, 512)
    for j in range(4):
        qdf_ref[:, LANE * j:LANE * (j + 1)] = (rope(t[:, LANE * j:LANE * (j + 1)]) * QK_SCALE_LOG2).astype(BF16)
    t = mm(_C_KD, 512)
    for j in range(4):
        kdf_ref[:, LANE * j:LANE * (j + 1)] = rope(t[:, LANE * j:LANE * (j + 1)]).astype(BF16)
    t = mm(_C_VD, 512)
    for u, rs in enumerate(tiles):
        for j in range(4):
            vdft_ref[u, LANE * j:LANE * (j + 1), :] = t[rs, LANE * j:LANE * (j + 1)].T.astype(BF16)

    for j in range(4):
        t = mm(_C_GM + 512 * j, 512)
        gm_ref[:, 512 * j:512 * (j + 1)] = jax.nn.sigmoid(t).astype(BF16)

    t = jax.nn.sigmoid(mm(_C_GN, 128))
    for u, rs in enumerate(tiles):
        gn_ref[u] = t[rs].T[0:GATE_ROWS]


def _inproj(x2, w, cos_t, sa_t, sb_t, oh_t, seq):
    n = x2.shape[0]
    nt = n // T
    spt = seq // TM
    assert seq % TM == 0 and TM % T == 0
    assert w.shape == (sum(IN_WIDTHS), D_MODEL) and sum(IN_WIDTHS[:7]) == _C_QD
    assert sum(IN_WIDTHS) - 3 * NSA_HEADS == _C_GN and _C_QD % 256 == 0 and _C_GN % 256 == 0
    row = lambda w: pl.BlockSpec((TM, w), lambda i: (i, 0))
    whole = lambda a: pl.BlockSpec(a.shape, lambda i: (0, 0), pipeline_mode=pl.Buffered(1))
    tab = pl.BlockSpec((TM, LANE), lambda i: (i % spt, 0))
    tile_t = lambda r: pl.BlockSpec((TM // T, r, T), lambda i: (i, 0, 0))
    out_shape = (
        jax.ShapeDtypeStruct((n, 512), BF16),
        jax.ShapeDtypeStruct((n, 512), BF16),
        jax.ShapeDtypeStruct((n, LANE), F32),
        jax.ShapeDtypeStruct((n, LANE), F32),
        jax.ShapeDtypeStruct((n, 256), BF16),
        jax.ShapeDtypeStruct((nt, LANE, T), BF16),
        jax.ShapeDtypeStruct((n, 256), BF16),
        jax.ShapeDtypeStruct((nt, LANE, T), BF16),
        jax.ShapeDtypeStruct((n, 512), BF16),
        jax.ShapeDtypeStruct((n, 512), BF16),
        jax.ShapeDtypeStruct((nt, 512, T), BF16),
        jax.ShapeDtypeStruct((n, 2048), BF16),
        jax.ShapeDtypeStruct((nt, GATE_ROWS, T), F32),
    )
    out_specs = (row(512), row(512), row(LANE), row(LANE), row(256), tile_t(LANE), row(256),
                 tile_t(LANE), row(512), row(512), tile_t(512), row(2048), tile_t(GATE_ROWS))
    return pl.pallas_call(
        _inproj_kernel,
        out_shape=out_shape,
        grid=(n // TM,),
        in_specs=[row(D_MODEL), whole(w), tab, tab, tab, tab],
        out_specs=out_specs,
        scratch_shapes=[pltpu.VMEM((_C_END, D_MODEL), BF16)],
        compiler_params=_params(1, VMEM_LIMIT),
        name="inproj",
    )(x2, w, cos_t, sa_t, sb_t, oh_t)


def _gelu_tanh(x):
    c = math.sqrt(2.0 / math.pi)
    return x * (0.5 * (1.0 + jnp.tanh(c * (x + 0.044715 * (x * x * x)))))


def _compress_kernel(kf_ref, vf_ref, pet_ref, peb_ref, wt_ref, wb_ref, b1_ref, w2k_ref, w2v_ref,
                     kca_ref, vct_ref):
    nchunk = kf_ref.shape[0] // CMP_STRIDE

    def hidden(x_ref, kv):
        a = jnp.zeros((nchunk, 2 * CMP_HIDDEN), F32)
        b = jnp.zeros((nchunk, 2 * CMP_HIDDEN), F32)
        for p in range(0, CMP_STRIDE, 2):
            x = jnp.concatenate([x_ref[pl.ds(p, nchunk, stride=CMP_STRIDE), :],
                                 x_ref[pl.ds(p + 1, nchunk, stride=CMP_STRIDE), :]], axis=1)
            c0, c1 = LANE * p, LANE * (p + 2)
            a = a + _nn((x + pet_ref[kv, :, c0:c1]).astype(BF16), wt_ref[kv, c0:c1, :])
            b = b + _nn((x + peb_ref[kv, :, c0:c1]).astype(BF16), wb_ref[kv, c0:c1, :])
        h = a + pltpu.roll(b, nchunk - 1, 0) + b1_ref[kv]
        return _gelu_tanh(h).astype(BF16)

    hk = hidden(kf_ref, 0)
    for g in range(2):
        kca_ref[g] = _nn(hk, w2k_ref[g]).astype(BF16)
    hv = hidden(vf_ref, 1)
    vct_ref[...] = _nn(hv, w2v_ref[...]).T.astype(BF16)


def _compress(kf, vf, pet, peb, wt, wb, b1, w2k, w2v, nb, seq):
    full = lambda a: pl.BlockSpec(a.shape, lambda b: (0,) * a.ndim)
    return pl.pallas_call(
        _compress_kernel,
        out_shape=(jax.ShapeDtypeStruct((nb, 2, 128, LANE), BF16),
                   jax.ShapeDtypeStruct((nb, LANE, 128), BF16)),
        grid=(nb,),
        in_specs=[pl.BlockSpec((seq, LANE), lambda b: (b, 0)),
                  pl.BlockSpec((seq, LANE), lambda b: (b, 0)),
                  full(pet), full(peb), full(wt), full(wb), full(b1), full(w2k), full(w2v)],
        out_specs=(pl.BlockSpec((None, 2, 128, LANE), lambda b: (b, 0, 0, 0)),
                   pl.BlockSpec((None, LANE, 128), lambda b: (b, 0, 0))),
        compiler_params=_params(1, VMEM_LIMIT),
        name="compress",
    )(kf, vf, pet, peb, wt, wb, b1, w2k, w2v)


def _cmpsel_kernel(q_ref, kca_ref, vct_ref, gnt_ref, ocmp_ref, sel_ref, s_scr):
    nq = q_ref.shape[0] // T
    lane = lax.broadcasted_iota(jnp.int32, (T, LANE), 1)
    crow = lax.broadcasted_iota(jnp.int32, (128, T), 0)
    tcol0 = lax.broadcasted_iota(jnp.int32, (128, T), 1)
    tlane = lax.broadcasted_iota(jnp.int32, (1, T), 1)

    jj = lax.broadcasted_iota(jnp.int32, (32, 128), 0) * SLC_LEN
    cc = lax.broadcasted_iota(jnp.int32, (32, 128), 1) * CMP_STRIDE
    ov = jnp.maximum(jnp.minimum(cc + CMP_LEN, jj + SLC_LEN) - jnp.maximum(cc, jj), 0)
    ovt = (ov.astype(F32) * (1.0 / CMP_LEN)).astype(BF16)

    jrow = lax.broadcasted_iota(jnp.int32, (32, T), 0)
    jrow8 = lax.broadcasted_iota(jnp.int32, (8, T), 0)
    tblk0 = lax.broadcasted_iota(jnp.int32, (32, T), 1)

    def scores(i, buf):
        rows = pl.ds(pl.multiple_of(i * T, T), T)
        for hp in range(NSA_HEADS // 2):
            g = hp // (NSA_HPG // 2)
            qp = q_ref[rows, LANE * hp:LANE * (hp + 1)].astype(F32)
            qs = (jnp.where(lane < 64, qp, 0.0), jnp.where(lane < 64, pltpu.roll(qp, 64, 1), 0.0))
            for par in range(2):
                s_scr[buf, 2 * hp + par] = _nt(kca_ref[g], qs[par].astype(BF16))

    def attend_select(i, buf):
        t0 = i * T
        rows = pl.ds(pl.multiple_of(t0, T), T)
        cmask = (crow * CMP_STRIDE + (CMP_LEN - 1)) <= tcol0 + t0
        live = jnp.where(tlane + t0 >= (CMP_LEN - 1), 1.0, 0.0)
        tblk = (tblk0 + t0) // SLC_LEN
        valid = jrow <= tblk
        forced = (jrow == 0) | (jrow == tblk) | (jrow == tblk - 1)
        bonus = jnp.where(forced, FORCE_BONUS, 0.0)
        for g in range(NSA_GROUPS):
            vt = vct_ref[64 * g:64 * (g + 1), :]
            psum = jnp.zeros((128, T), F32)
            for hp in range(NSA_HPG // 2):
                outs = []
                for par in range(2):
                    h = NSA_HPG * g + 2 * hp + par
                    s = jnp.where(cmask, s_scr[buf, h], NEG)
                    m = jnp.max(s, axis=0, keepdims=True)
                    e = jnp.exp2(s - m)
                    p = e * (live / jnp.sum(e, axis=0, keepdims=True))
                    psum = psum + p
                    outs.append(_nn(vt, p.astype(BF16)) * gnt_ref[i, 3 * h:3 * h + 1, :])
                ocmp_ref[rows, LANE * (2 * g + hp):LANE * (2 * g + hp + 1)] = (
                    jnp.concatenate(outs, axis=0).T.astype(BF16))

            p_hi = psum.astype(BF16)
            p_lo = (psum - p_hi.astype(F32)).astype(BF16)
            pslc = _nn(ovt, p_hi) + _nn(ovt, p_lo)
            pri = jnp.where(valid, pslc + bonus, -1.0)
            rank = [jnp.zeros((8, T), F32) for _ in range(4)]
            for r in range(32):
                row = pri[r:r + 1, :]
                for a in range(4):
                    pa = pri[8 * a:8 * (a + 1), :]
                    if 8 * a > r:
                        ahead = jnp.where(row >= pa, 1.0, 0.0)
                    elif 8 * a + 7 < r:
                        ahead = jnp.where(row > pa, 1.0, 0.0)
                    else:
                        ahead = jnp.where(jrow8 + 8 * a > r, jnp.where(row >= pa, 1.0, 0.0),
                                          jnp.where(row > pa, 1.0, 0.0))
                    rank[a] = rank[a] + ahead
            rank = jnp.concatenate(rank, axis=0)
            selneg = jnp.where(rank < float(SLC_TOPK), 0.0, NEG)
            pad = jnp.concatenate([jnp.zeros((64, T), F32), selneg, jnp.zeros((32, T), F32)], axis=0)
            sel_ref[rows, LANE * g:LANE * (g + 1)] = pad.T.astype(BF16)

    scores(0, 0)

    def body(u, carry):
        scores(2 * u + 1, 1)
        attend_select(2 * u, 0)
        scores(jnp.minimum(2 * u + 2, nq - 1), 0)
        attend_select(2 * u + 1, 1)
        return carry

    lax.fori_loop(0, nq // 2, body, 0)


def _cmpsel(qraw, kca, vct, gnt, nb, seq):
    n = qraw.shape[0]
    nq = seq // T
    assert nq % 2 == 0
    return pl.pallas_call(
        _cmpsel_kernel,
        out_shape=(jax.ShapeDtypeStruct((n, 512), BF16), jax.ShapeDtypeStruct((n, 256), BF16)),
        grid=(nb,),
        in_specs=[pl.BlockSpec((seq, 512), lambda b: (b, 0)),
                  pl.BlockSpec((None, 2, 128, LANE), lambda b: (b, 0, 0, 0)),
                  pl.BlockSpec((None, LANE, 128), lambda b: (b, 0, 0)),
                  pl.BlockSpec((nq, GATE_ROWS, T), lambda b: (b, 0, 0))],
        out_specs=(pl.BlockSpec((seq, 512), lambda b: (b, 0)),
                   pl.BlockSpec((seq, 256), lambda b: (b, 0))),
        scratch_shapes=[pltpu.VMEM((2, NSA_HEADS, 128, T), F32)],
        compiler_params=_params(1, VMEM_LIMIT),
        name="cmpsel",
    )(qraw, kca, vct, gnt)


def _split_heads(q_ref, extras, qs_scr, base=0):
    lane = lax.broadcasted_iota(jnp.int32, (T, LANE), 1)
    for hp in range(NSA_HEADS // 2):
        extra = extras[hp // (NSA_HPG // 2)]
        qp = q_ref[:, LANE * hp:LANE * (hp + 1)].astype(F32)
        qs_scr[base + 2 * hp] = jnp.where(lane < 64, qp, extra).astype(BF16)
        qs_scr[base + 2 * hp + 1] = jnp.where(lane < 64, pltpu.roll(qp, 64, 1), extra).astype(BF16)


SM_ROWS = 32


RING = 4


def _tile_schedule(nq):
    below = [(i, j) for j in range(nq) for i in range(j + 1, nq)]
    assert len(below) == (RING - 1) * nq + RING
    pairs = []
    for i in range(nq):
        pairs += [(i, i)] + below[(RING - 1) * i:(RING - 1) * (i + 1)]
    pairs += below[(RING - 1) * nq:]
    return (jnp.asarray([p[0] for p in pairs], jnp.int32), jnp.asarray([p[1] for p in pairs], jnp.int32))


def _flash_scratch(nq, nc, acc_rows):
    return [pltpu.VMEM((nq * nc, T, LANE), BF16),
            pltpu.VMEM((RING, nc, T, T), F32),
            pltpu.VMEM((RING, nc, 1, T), F32),
            pltpu.VMEM((nq * nc, 1, T), F32),
            pltpu.VMEM((nq * nc, 1, T), F32),
            pltpu.VMEM((nq * acc_rows // LANE, LANE, T), F32)]


def _causal_flash(it_ref, jt_ref, nq, k_ref, vt_ref, qs_scr, s_scr, mx_scr, m_scr, l_scr, acc_scr, chains, acc_of):
    nc = len(chains)
    nsets = it_ref.shape[0]
    assert nsets % RING == 0 and nsets >= 2 * RING
    m_scr[...] = jnp.full(m_scr.shape, NEG, F32)
    l_scr[...] = jnp.zeros(l_scr.shape, F32)
    acc_scr[...] = jnp.zeros(acc_scr.shape, F32)
    krow = lax.broadcasted_iota(jnp.int32, (T, T), 0)
    qcol = lax.broadcasted_iota(jnp.int32, (T, T), 1)
    ones = jnp.ones((16, T), BF16)

    def scores(n, buf):
        i, j = it_ref[n], jt_ref[n]
        if buf == 0:
            mask = krow <= qcol + jnp.where(i == j, 0, T)
        for c, (kl, _, _) in enumerate(chains):
            s = _nt(k_ref[j, :, kl:kl + LANE], qs_scr[i * nc + c])
            if buf == 0:
                s = jnp.where(mask, s, NEG)
            s_scr[buf, c] = s
            mx_scr[buf, c] = jnp.max(s, axis=0, keepdims=True)

    def softmax_pv(n, buf):
        i, j = it_ref[n], jt_ref[n]
        for c, (_, vr, dv) in enumerate(chains):
            m_prev = m_scr[i * nc + c]
            m_new = jnp.maximum(m_prev, mx_scr[buf, c])
            alpha = jnp.exp2(m_prev - m_new)
            parts = [jnp.exp2(s_scr[buf, c, r:r + SM_ROWS, :] - m_new).astype(BF16)
                     for r in range(0, T, SM_ROWS)]
            vt = jnp.concatenate([vt_ref[j, vr:vr + dv, :], ones], axis=0)
            o = _nn(vt, jnp.concatenate(parts, axis=0))
            l_scr[i * nc + c] = alpha * l_scr[i * nc + c] + o[dv:dv + 1]
            m_scr[i * nc + c] = m_new
            acc = acc_of(i, c)
            acc[...] = acc[...] * alpha + o[0:dv]

    scores(0, 0)

    def body(u, carry):
        n = RING * u
        for r in range(RING):
            scores(n + r + 1, (r + 1) % RING)
            softmax_pv(n + r, r)
        return carry

    lax.fori_loop(0, nsets // RING - 1, body, 0)
    n = nsets - RING
    for r in range(RING):
        if r + 1 < RING:
            scores(n + r + 1, r + 1)
        softmax_pv(n + r, r)


def _slc_kernel(it_ref, jt_ref, q_ref, sel_ref, k_ref, vt_ref, gnt_ref, o_ref,
                qs_scr, s_scr, mx_scr, m_scr, l_scr, acc_scr):
    nq = q_ref.shape[0] // T
    nh = NSA_HEADS

    def prep(i, carry):
        rows = pl.ds(pl.multiple_of(i * T, T), T)
        sel = sel_ref[rows, :].astype(F32)
        _split_heads(q_ref.at[rows, :], [sel[:, 0:LANE], sel[:, LANE:2 * LANE]], qs_scr, i * nh)
        return carry

    lax.fori_loop(0, nq, prep, 0)
    chains = [(LANE * (h // NSA_HPG), 64 * (h // NSA_HPG), 64) for h in range(nh)]
    acc_of = lambda i, h: acc_scr.at[i * (nh // 2) + h // 2, 64 * (h % 2):64 * (h % 2 + 1), :]
    _causal_flash(it_ref, jt_ref, nq, k_ref, vt_ref, qs_scr, s_scr, mx_scr, m_scr, l_scr, acc_scr, chains, acc_of)

    def finish(i, carry):
        rows = pl.ds(pl.multiple_of(i * T, T), T)
        for hp in range(nh // 2):
            scale = [gnt_ref[i, 3 * h + 1:3 * h + 2, :] / l_scr[i * nh + h] for h in (2 * hp, 2 * hp + 1)]
            inv = jnp.concatenate([jnp.broadcast_to(scale[0], (64, T)), jnp.broadcast_to(scale[1], (64, T))],
                                  axis=0)
            o_ref[rows, LANE * hp:LANE * (hp + 1)] = (acc_scr[i * (nh // 2) + hp] * inv).T.astype(BF16)
        return carry

    lax.fori_loop(0, nq, finish, 0)


def _slc(qrot, sel, ksa3, vst, gnt3, nb, seq):
    n = qrot.shape[0]
    nq = seq // T
    it, jt = _tile_schedule(nq)
    grid_spec = pltpu.PrefetchScalarGridSpec(
        num_scalar_prefetch=2,
        grid=(nb,),
        in_specs=[pl.BlockSpec((seq, 512), lambda b, it, jt: (b, 0)),
                  pl.BlockSpec((seq, 256), lambda b, it, jt: (b, 0)),
                  pl.BlockSpec((nq, T, 256), lambda b, it, jt: (b, 0, 0)),
                  pl.BlockSpec((nq, LANE, T), lambda b, it, jt: (b, 0, 0)),
                  pl.BlockSpec((nq, GATE_ROWS, T), lambda b, it, jt: (b, 0, 0))],
        out_specs=pl.BlockSpec((seq, 512), lambda b, it, jt: (b, 0)),
        scratch_shapes=_flash_scratch(nq, NSA_HEADS, NSA_Q))
    return pl.pallas_call(
        _slc_kernel,
        out_shape=jax.ShapeDtypeStruct((n, 512), BF16),
        grid_spec=grid_spec,
        compiler_params=_params(1, VMEM_LIMIT),
        name="slc",
    )(it, jt, qrot, sel, ksa3, vst, gnt3)


def _win_kernel(q_ref, k_ref, vt_ref, gnt_ref, o_ref, qs_scr, s_scr, mx_scr, acc_scr):
    i = pl.program_id(1)
    _split_heads(q_ref, [0.0, 0.0], qs_scr)
    krow = lax.broadcasted_iota(jnp.int32, (T, T), 0)
    qcol = lax.broadcasted_iota(jnp.int32, (T, T), 1)
    span = WINDOW // T
    masks = {0: krow <= qcol, span: (krow > qcol) & (i >= span)}
    ones = jnp.ones((16, T), BF16)

    def scores(hp, buf):
        for par in range(2):
            h = 2 * hp + par
            g = h // NSA_HPG
            mx = None
            for d in range(span + 1):
                j = i - d
                s = _nt(k_ref[jnp.maximum(j, 0), :, LANE * g:LANE * (g + 1)], qs_scr[h])
                if d in masks:
                    s = jnp.where(masks[d], s, NEG)
                else:
                    s = s + jnp.where(j >= 0, 0.0, NEG)
                s_scr[buf, par, d] = s
                md = jnp.max(s, axis=0, keepdims=True)
                mx = md if mx is None else jnp.maximum(mx, md)
            mx_scr[buf, par] = mx

    def softmax_pv(hp, buf):
        for par in range(2):
            g = (2 * hp + par) // NSA_HPG
            m = mx_scr[buf, par]
            acc = jnp.zeros((64 + 16, T), F32)
            for d in range(span + 1):
                parts = [jnp.exp2(s_scr[buf, par, d, r:r + SM_ROWS, :] - m).astype(BF16)
                         for r in range(0, T, SM_ROWS)]
                vt = jnp.concatenate([vt_ref[jnp.maximum(i - d, 0), 64 * g:64 * (g + 1), :], ones], axis=0)
                acc = acc + _nn(vt, jnp.concatenate(parts, axis=0))
            gate = gnt_ref[3 * (2 * hp + par) + 2:3 * (2 * hp + par) + 3, :]
            acc_scr[hp, 64 * par:64 * (par + 1), :] = acc[0:64] * (gate / acc[64:65])

    npairs = NSA_HEADS // 2
    scores(0, 0)
    for hp in range(npairs):
        if hp + 1 < npairs:
            scores(hp + 1, (hp + 1) % 2)
        softmax_pv(hp, hp % 2)
    for hp in range(npairs):
        o_ref[:, LANE * hp:LANE * (hp + 1)] = acc_scr[hp].T.astype(BF16)


def _win(qrot, kwa3, vwt, gnt, nb, seq):
    n = qrot.shape[0]
    nq = seq // T
    return pl.pallas_call(
        _win_kernel,
        out_shape=jax.ShapeDtypeStruct((n, 512), BF16),
        grid=(nb, nq),
        in_specs=[pl.BlockSpec((T, 512), lambda b, i: (b * nq + i, 0)),
                  pl.BlockSpec((nq, T, 256), lambda b, i: (b, 0, 0)),
                  pl.BlockSpec((nq, LANE, T), lambda b, i: (b, 0, 0)),
                  pl.BlockSpec((None, GATE_ROWS, T), lambda b, i: (b * nq + i, 0, 0))],
        out_specs=pl.BlockSpec((T, 512), lambda b, i: (b * nq + i, 0)),
        scratch_shapes=[pltpu.VMEM((NSA_HEADS, T, LANE), BF16),
                        pltpu.VMEM((2, 2, WINDOW // T + 1, T, T), F32),
                        pltpu.VMEM((2, 2, 1, T), F32),
                        pltpu.VMEM((NSA_HEADS // 2, LANE, T), F32)],
        compiler_params=_params(2, VMEM_LIMIT),
        name="win",
    )(qrot, kwa3, vwt, gnt)


def _diff_kernel(it_ref, jt_ref, q_ref, k_ref, vt_ref, lam_ref, g_ref, o_ref,
                 qs_scr, s_scr, mx_scr, m_scr, l_scr, acc_scr):
    nq = q_ref.shape[0] // T
    nc = 2 * DIFF_HEADS
    lane = lax.broadcasted_iota(jnp.int32, (T, LANE), 1)

    def prep(i, carry):
        rows = pl.ds(pl.multiple_of(i * T, T), T)
        for h in range(DIFF_HEADS):
            q = q_ref[rows, LANE * h:LANE * (h + 1)].astype(F32)
            qs_scr[i * nc + 2 * h] = jnp.where(lane < 64, q, 0.0).astype(BF16)
            qs_scr[i * nc + 2 * h + 1] = jnp.where(lane < 64, 0.0, q).astype(BF16)
        return carry

    lax.fori_loop(0, nq, prep, 0)
    chains = [(LANE * (c // 2), LANE * (c // 2), LANE) for c in range(nc)]
    _causal_flash(it_ref, jt_ref, nq, k_ref, vt_ref, qs_scr, s_scr, mx_scr, m_scr, l_scr, acc_scr, chains,
                  lambda i, c: acc_scr.at[i * nc + c])

    lp = lam_ref[...]
    lam = (jnp.exp(jnp.sum(lp[0:1] * lp[1:2], axis=1, keepdims=True))
           - jnp.exp(jnp.sum(lp[2:3] * lp[3:4], axis=1, keepdims=True)) + LAMBDA_INIT)

    def finish(i, carry):
        rows = pl.ds(pl.multiple_of(i * T, T), T)
        for h in range(DIFF_HEADS):
            c = i * nc + 2 * h
            o = acc_scr[c] * (1.0 / l_scr[c]) - lam * (acc_scr[c + 1] * (1.0 / l_scr[c + 1]))
            o = o * lax.rsqrt(jnp.mean(o * o, axis=0, keepdims=True) + RMS_EPS)
            o_ref[rows, LANE * h:LANE * (h + 1)] = ((o.T * g_ref[...]) * (1.0 - LAMBDA_INIT)).astype(BF16)
        return carry

    lax.fori_loop(0, nq, finish, 0)


def _diff(qdf, kdf3, vdft, lam, norm_g, nb, seq):
    n = qdf.shape[0]
    nq = seq // T
    it, jt = _tile_schedule(nq)
    grid_spec = pltpu.PrefetchScalarGridSpec(
        num_scalar_prefetch=2,
        grid=(nb,),
        in_specs=[pl.BlockSpec((seq, 512), lambda b, it, jt: (b, 0)),
                  pl.BlockSpec((nq, T, 512), lambda b, it, jt: (b, 0, 0)),
                  pl.BlockSpec((nq, 512, T), lambda b, it, jt: (b, 0, 0)),
                  pl.BlockSpec((4, HEAD_DIM), lambda b, it, jt: (0, 0)),
                  pl.BlockSpec((1, LANE), lambda b, it, jt: (0, 0))],
        out_specs=pl.BlockSpec((seq, 512), lambda b, it, jt: (b, 0)),
        scratch_shapes=_flash_scratch(nq, 2 * DIFF_HEADS, 2 * DIFF_HEADS * DIFF_V_DIM))
    return pl.pallas_call(
        _diff_kernel,
        out_shape=jax.ShapeDtypeStruct((n, 512), BF16),
        grid_spec=grid_spec,
        compiler_params=_params(1, VMEM_LIMIT),
        name="diff",
    )(it, jt, qdf, kdf3, vdft, lam, norm_g)


def _layer_norm(z, g, b):
    mu = jnp.mean(z, axis=-1, keepdims=True)
    zc = z - mu
    var = jnp.mean(zc * zc, axis=-1, keepdims=True)
    return zc * lax.rsqrt(var + LN_EPS) * g + b


def _merge_kernel(x_ref, ocmp_ref, oslc_ref, owin_ref, yb_ref, gm_ref,
                  wa_ref, wb_ref, wo_ref, g_ref, b_ref, h_ref):
    ya = ocmp_ref[...].astype(F32) + oslc_ref[...].astype(F32) + owin_ref[...].astype(F32)
    ta = _nn(ya.astype(BF16), wa_ref[...])
    tb = _nn(yb_ref[...], wb_ref[...])
    merged = gm_ref[:, 0:D_MODEL].astype(F32) * ta + gm_ref[:, D_MODEL:2 * D_MODEL].astype(F32) * tb
    mix = _nn(merged.astype(BF16), wo_ref[...])
    h_ref[...] = _layer_norm(DEEPNORM_ALPHA * x_ref[...] + mix, g_ref[...], b_ref[...])


def _merge(x2, ocmp, oslc, owin, yb, gm, wa, wb, wo, g, b):
    n = x2.shape[0]
    row = lambda w: pl.BlockSpec((TM, w), lambda i: (i, 0))
    full = lambda a: pl.BlockSpec(a.shape, lambda i: (0,) * a.ndim)
    return pl.pallas_call(
        _merge_kernel,
        out_shape=jax.ShapeDtypeStruct((n, D_MODEL), F32),
        grid=(n // TM,),
        in_specs=[row(D_MODEL), row(512), row(512), row(512), row(512), row(2048),
                  full(wa), full(wb), full(wo), full(g), full(b)],
        out_specs=row(D_MODEL),
        compiler_params=_params(1, VMEM_LIMIT),
        name="merge",
    )(x2, ocmp, oslc, owin, yb, gm, wa, wb, wo, g, b)


def _ffn_kernel(h_ref, wgu_ref, wd_ref, g_ref, b_ref, o_ref):
    h = h_ref[...]
    hb = h.astype(BF16)
    gate = _nn(hb, wgu_ref[:, 0:FFN_HIDDEN])
    up = _nn(hb, wgu_ref[:, FFN_HIDDEN:2 * FFN_HIDDEN])
    act = (gate * jax.nn.sigmoid(gate) * up).astype(BF16)
    y = _nn(act, wd_ref[...])
    o_ref[...] = _layer_norm(DEEPNORM_ALPHA * h + y, g_ref[...], b_ref[...])


def _ffn(h1, wgu, wd, g, b):
    n = h1.shape[0]
    row = pl.BlockSpec((TM, D_MODEL), lambda i: (i, 0))
    full = lambda a: pl.BlockSpec(a.shape, lambda i: (0,) * a.ndim, pipeline_mode=pl.Buffered(1))
    return pl.pallas_call(
        _ffn_kernel,
        out_shape=jax.ShapeDtypeStruct((n, D_MODEL), F32),
        grid=(n // TM,),
        in_specs=[row, full(wgu), full(wd), full(g), full(b)],
        out_specs=row,
        compiler_params=_params(1, VMEM_LIMIT),
        name="ffn",
    )(h1, wgu, wd, g, b)


def _rope_tables(seq):
    half = ROPE_DIM // 2
    inv_freq = ROPE_THETA ** (-jnp.arange(half, dtype=F32) * 2.0 / ROPE_DIM)
    ang = jnp.arange(seq, dtype=F32)[:, None] * inv_freq[None, :]
    cos, sin = jnp.cos(ang), jnp.sin(ang)
    ones = jnp.ones((seq, HEAD_DIM - ROPE_DIM), F32)
    zeros8 = jnp.zeros((seq, half), F32)
    zeros48 = jnp.zeros((seq, HEAD_DIM - ROPE_DIM), F32)
    c64 = jnp.concatenate([cos, cos, ones], axis=1)
    sa64 = jnp.concatenate([-sin, zeros8, zeros48], axis=1)
    sb64 = jnp.concatenate([zeros8, sin, zeros48], axis=1)
    rep = lambda a: jnp.concatenate([a, a], axis=1)
    pos_blk = jnp.arange(seq, dtype=jnp.int32)[:, None] // SLC_LEN
    lane = jnp.arange(LANE, dtype=jnp.int32)[None, :]
    onehot = ((lane >= 64) & (lane < 96) & (lane - 64 == pos_blk)).astype(F32)
    return rep(c64), rep(sa64), rep(sb64), onehot


def _prep_compress(cmp_pe, cmp_w1, cmp_b1, cmp_w2):
    half = CMP_LEN // 2
    pet, peb, wt, wb, b1 = [], [], [], [], []
    for kv in range(2):
        pe = cmp_pe[kv]
        tile2 = lambda a: jnp.concatenate([a, a], axis=1).reshape(1, half * 2 * HEAD_DIM)
        pet.append(tile2(pe[:half]))
        peb.append(tile2(pe[half:]))
        w1 = cmp_w1[kv].reshape(CMP_LEN, HEAD_DIM, CMP_HIDDEN)
        z = jnp.zeros((half, HEAD_DIM, CMP_HIDDEN), w1.dtype)

        def spread(wh):
            g0 = jnp.concatenate([wh, z], axis=2)
            g1 = jnp.concatenate([z, wh], axis=2)
            return jnp.stack([g0, g1], axis=1).reshape(half * 2 * HEAD_DIM, 2 * CMP_HIDDEN)

        wt.append(spread(w1[:half]))
        wb.append(spread(w1[half:]))
        b1.append(jnp.concatenate([cmp_b1[kv], cmp_b1[kv]])[None, :])
    w2k, w2v = cmp_w2[0], cmp_w2[1]
    zk = jnp.zeros_like(w2k)
    w2k_g = jnp.stack([
        jnp.concatenate([jnp.concatenate([w2k, zk], axis=1), jnp.zeros((CMP_HIDDEN, LANE), w2k.dtype)], axis=0),
        jnp.concatenate([jnp.zeros((CMP_HIDDEN, LANE), w2k.dtype), jnp.concatenate([w2k, zk], axis=1)], axis=0),
    ])
    zv = jnp.zeros_like(w2v)
    w2v_bd = jnp.concatenate([jnp.concatenate([w2v, zv], axis=1),
                              jnp.concatenate([zv, w2v], axis=1)], axis=0)
    st = lambda xs: jnp.stack(xs)
    return (st(pet), st(peb), st(wt).astype(BF16), st(wb).astype(BF16), st(b1),
            w2k_g.astype(BF16), w2v_bd.astype(BF16))


def kernel(x, w_in, cmp_pe, cmp_w1, cmp_b1, cmp_w2, diff_lambda, diff_norm_g, w_branch_a, w_branch_b,
           w_o, ln1_g, ln1_b, w_gate_up, w_down, ln2_g, ln2_b):
    nb, seq, d = x.shape
    assert d == D_MODEL and seq % T == 0 and seq // SLC_LEN == 32 and seq // CMP_STRIDE == 128
    n = nb * seq
    x2 = x.reshape(n, d)
    cos_t, sa_t, sb_t, oh_t = _rope_tables(seq)
    (qraw, qrot, kcs, vcs, ksa, vst, kwa, vwt, qdf, kdf, vdft, gm, gnt) = _inproj(
        x2, w_in.reshape(w_in.shape[1:]).T, cos_t, sa_t, sb_t, oh_t, seq)

    pet, peb, wt, wb, b1, w2k, w2v = _prep_compress(cmp_pe[0], cmp_w1[0], cmp_b1[0], cmp_w2[0])
    kca, vct = _compress(kcs, vcs, pet, peb, wt, wb, b1, w2k, w2v, nb, seq)

    ocmp, sel = _cmpsel(qraw, kca, vct, gnt, nb, seq)
    nt = n // T
    oslc = _slc(qrot, sel, ksa.reshape(nt, T, 256), vst, gnt, nb, seq)
    owin = _win(qrot, kwa.reshape(nt, T, 256), vwt, gnt, nb, seq)
    yb = _diff(qdf, kdf.reshape(nt, T, 512), vdft, diff_lambda[0], diff_norm_g[0][None, :], nb, seq)

    h1 = _merge(x2, ocmp, oslc, owin, yb, gm,
                w_branch_a[0].astype(BF16), w_branch_b[0].astype(BF16), w_o[0].astype(BF16),
                ln1_g[0][None, :], ln1_b[0][None, :])
    out = _ffn(h1, w_gate_up[0].astype(BF16), w_down[0].astype(BF16), ln2_g[0][None, :], ln2_b[0][None, :])
    return out.reshape(nb, seq, d)
```

```python
import functools
import math

import numpy as np
import jax
import jax.numpy as jnp
from jax import lax
from jax.experimental import pallas as pl
from jax.experimental.pallas import tpu as pltpu

D_MODEL = 1024
HEAD_DIM = 64
ROPE_DIM = HEAD_DIM // 4
ROPE_THETA = 500000.0
NSA_HEADS = 8
NSA_GROUPS = 2
NSA_HPG = NSA_HEADS // NSA_GROUPS
CMP_LEN = 32
CMP_STRIDE = 16
CMP_HIDDEN = 256
SLC_LEN = 64
SLC_TOPK = 8
WINDOW = 512
FORCE_BONUS = 1.0e4
DIFF_HEADS = 4
DIFF_V_DIM = 2 * HEAD_DIM
FFN_HIDDEN = ((8 * D_MODEL // 3 + 255) // 256) * 256
DEPTH = 1
DEEPNORM_ALPHA = (2 * DEPTH) ** 0.25
NEG = -1.0e30
LN_EPS = 1e-5
RMS_EPS = 1e-5
LAMBDA_INIT = 0.8 - 0.6 * math.exp(-0.3 * 0)
QK_SCALE = HEAD_DIM ** -0.5
QK_SCALE_LOG2 = QK_SCALE * math.log2(math.e)

NSA_Q = NSA_HEADS * HEAD_DIM
NSA_KV = NSA_GROUPS * HEAD_DIM
DIFF_QK = DIFF_HEADS * 2 * HEAD_DIM
DIFF_V = DIFF_HEADS * DIFF_V_DIM
IN_WIDTHS = (NSA_Q, NSA_KV, NSA_KV, NSA_KV, NSA_KV, NSA_KV, NSA_KV, 3 * NSA_HEADS,
             DIFF_QK, DIFF_QK, DIFF_V, 2 * D_MODEL)

LANE = 128
T = 256
TM = 512
VMEM_LIMIT = 56 * 1024 * 1024
GATE_ROWS = 32

BF16 = jnp.bfloat16
F32 = jnp.float32

_C_Q, _C_KC, _C_VC, _C_KS, _C_VS, _C_KW, _C_VW = 0, 512, 640, 768, 896, 1024, 1152
_C_QD, _C_KD, _C_VD, _C_GM, _C_GN, _C_END = 1280, 1792, 2304, 2816, 4864, 4992


def _nt(a, b):
    return lax.dot_general(a, b, (((1,), (1,)), ((), ())), preferred_element_type=F32)


def _nn(a, b):
    return jnp.dot(a, b, preferred_element_type=F32)


def _params(n_axes, vmem=None):
    return pltpu.CompilerParams(dimension_semantics=("arbitrary",) * n_axes,
                                vmem_limit_bytes=vmem)


def _inproj_kernel(x_ref, w_ref, cos_ref, sa_ref, sb_ref, oh_ref,
                   qraw_ref, qrot_ref, kc_ref, vc_ref, ksa_ref, vst_ref, kwa_ref, vwt_ref,
                   qdf_ref, kdf_ref, vdft_ref, gm_ref, gn_ref, w_scr):
    @pl.when(pl.program_id(0) == 0)
    def _():
        gn0 = _C_QD
        gn1 = gn0 + 3 * NSA_HEADS
        rows = 256

        def regroup(r, carry):
            dst = pl.ds(pl.multiple_of(r * rows, rows), rows)
            src = pl.ds(pl.multiple_of(jnp.where(r * rows < gn0, r * rows, r * rows + gn1 - gn0), 8), rows)
            w_scr[dst, :] = w_ref[src, :].astype(BF16)
            return carry

        lax.fori_loop(0, _C_GN // rows, regroup, 0)
        gates = jnp.concatenate([w_ref[gn0:gn1, :], jnp.zeros((LANE - (gn1 - gn0), D_MODEL), F32)], axis=0)
        w_scr[_C_GN:_C_END, :] = gates.astype(BF16)

    xb = x_ref[...].astype(BF16)
    cos = cos_ref[...]
    sa = sa_ref[...]
    sb = sb_ref[...]

    def mm(c0, n):
        return _nt(xb, w_scr[c0:c0 + n, :])

    def rope(t):
        return t * cos + pltpu.roll(t, LANE - 8, 1) * sa + pltpu.roll(t, 8, 1) * sb

    t = mm(_C_Q, 512)
    for j in range(4):
        tj = t[:, LANE * j:LANE * (j + 1)]
        qraw_ref[:, LANE * j:LANE * (j + 1)] = (tj * QK_SCALE_LOG2).astype(BF16)
        qrot_ref[:, LANE * j:LANE * (j + 1)] = (rope(tj) * QK_SCALE_LOG2).astype(BF16)

    t = mm(_C_KC, 256)
    kc_ref[...] = t[:, :LANE]
    vc_ref[...] = t[:, LANE:]

    lane = lax.broadcasted_iota(jnp.int32, (TM, LANE), 1)
    tiles = [slice(T * u, T * (u + 1)) for u in range(TM // T)]

    def per_group(k, extra, out_ref):
        out_ref[:, 0:LANE] = jnp.where(lane < 64, k, extra).astype(BF16)
        out_ref[:, LANE:2 * LANE] = jnp.where(lane < 64, pltpu.roll(k, 64, 1), extra).astype(BF16)

    t = mm(_C_KS, 256)
    per_group(rope(t[:, :LANE]), oh_ref[...], ksa_ref)
    for u, rs in enumerate(tiles):
        vst_ref[u] = t[rs, LANE:].T.astype(BF16)

    t = mm(_C_KW, 256)
    per_group(rope(t[:, :LANE]), 0.0, kwa_ref)
    for u, rs in enumerate(tiles):
        vwt_ref[u] = t[rs, LANE:].T.astype(BF16)

    t = mm(_C_QD, 512)
    for j in range(4):
        qdf_ref[:, LANE * j:LANE * (j + 1)] = (rope(t[:, LANE * j:LANE * (j + 1)]) * QK_SCALE_LOG2).astype(BF16)
    t = mm(_C_KD, 512)
    for j in range(4):
        kdf_ref[:, LANE * j:LANE * (j + 1)] = rope(t[:, LANE * j:LANE * (j + 1)]).astype(BF16)
    t = mm(_C_VD, 512)
    for u, rs in enumerate(tiles):
        for j in range(4):
            vdft_ref[u, LANE * j:LANE * (j + 1), :] = t[rs, LANE * j:LANE * (j + 1)].T.astype(BF16)

    for j in range(4):
        t = mm(_C_GM + 512 * j, 512)
        gm_ref[:, 512 * j:512 * (j + 1)] = jax.nn.sigmoid(t).astype(BF16)

    t = jax.nn.sigmoid(mm(_C_GN, 128))
    for u, rs in enumerate(tiles):
        gn_ref[u] = t[rs].T[0:GATE_ROWS]


def _inproj(x2, w, cos_t, sa_t, sb_t, oh_t, seq):
    n = x2.shape[0]
    nt = n // T
    spt = seq // TM
    assert seq % TM == 0 and TM % T == 0
    assert w.shape == (sum(IN_WIDTHS), D_MODEL) and sum(IN_WIDTHS[:7]) == _C_QD
    assert sum(IN_WIDTHS) - 3 * NSA_HEADS == _C_GN and _C_QD % 256 == 0 and _C_GN % 256 == 0
    row = lambda w: pl.BlockSpec((TM, w), lambda i: (i, 0))
    whole = lambda a: pl.BlockSpec(a.shape, lambda i: (0, 0), pipeline_mode=pl.Buffered(1))
    tab = pl.BlockSpec((TM, LANE), lambda i: (i % spt, 0))
    tile_t = lambda r: pl.BlockSpec((TM // T, r, T), lambda i: (i, 0, 0))
    out_shape = (
        jax.ShapeDtypeStruct((n, 512), BF16),
        jax.ShapeDtypeStruct((n, 512), BF16),
        jax.ShapeDtypeStruct((n, LANE), F32),
        jax.ShapeDtypeStruct((n, LANE), F32),
        jax.ShapeDtypeStruct((n, 256), BF16),
        jax.ShapeDtypeStruct((nt, LANE, T), BF16),
        jax.ShapeDtypeStruct((n, 256), BF16),
        jax.ShapeDtypeStruct((nt, LANE, T), BF16),
        jax.ShapeDtypeStruct((n, 512), BF16),
        jax.ShapeDtypeStruct((n, 512), BF16),
        jax.ShapeDtypeStruct((nt, 512, T), BF16),
        jax.ShapeDtypeStruct((n, 2048), BF16),
        jax.ShapeDtypeStruct((nt, GATE_ROWS, T), F32),
    )
    out_specs = (row(512), row(512), row(LANE), row(LANE), row(256), tile_t(LANE), row(256),
                 tile_t(LANE), row(512), row(512), tile_t(512), row(2048), tile_t(GATE_ROWS))
    return pl.pallas_call(
        _inproj_kernel,
        out_shape=out_shape,
        grid=(n // TM,),
        in_specs=[row(D_MODEL), whole(w), tab, tab, tab, tab],
        out_specs=out_specs,
        scratch_shapes=[pltpu.VMEM((_C_END, D_MODEL), BF16)],
        compiler_params=_params(1, VMEM_LIMIT),
        name="inproj",
    )(x2, w, cos_t, sa_t, sb_t, oh_t)


def _gelu_tanh(x):
    c = math.sqrt(2.0 / math.pi)
    return x * (0.5 * (1.0 + jnp.tanh(c * (x + 0.044715 * (x * x * x)))))


def _compress_kernel(kf_ref, vf_ref, pet_ref, peb_ref, wt_ref, wb_ref, b1_ref, w2k_ref, w2v_ref,
                     kca_ref, vct_ref):
    nchunk = kf_ref.shape[0] // CMP_STRIDE

    def hidden(x_ref, kv):
        a = jnp.zeros((nchunk, 2 * CMP_HIDDEN), F32)
        b = jnp.zeros((nchunk, 2 * CMP_HIDDEN), F32)
        for p in range(0, CMP_STRIDE, 2):
            x = jnp.concatenate([x_ref[pl.ds(p, nchunk, stride=CMP_STRIDE), :],
                                 x_ref[pl.ds(p + 1, nchunk, stride=CMP_STRIDE), :]], axis=1)
            c0, c1 = LANE * p, LANE * (p + 2)
            a = a + _nn((x + pet_ref[kv, :, c0:c1]).astype(BF16), wt_ref[kv, c0:c1, :])
            b = b + _nn((x + peb_ref[kv, :, c0:c1]).astype(BF16), wb_ref[kv, c0:c1, :])
        h = a + pltpu.roll(b, nchunk - 1, 0) + b1_ref[kv]
        return _gelu_tanh(h).astype(BF16)

    hk = hidden(kf_ref, 0)
    for g in range(2):
        kca_ref[g] = _nn(hk, w2k_ref[g]).astype(BF16)
    hv = hidden(vf_ref, 1)
    vct_ref[...] = _nn(hv, w2v_ref[...]).T.astype(BF16)


def _compress(kf, vf, pet, peb, wt, wb, b1, w2k, w2v, nb, seq):
    full = lambda a: pl.BlockSpec(a.shape, lambda b: (0,) * a.ndim)
    return pl.pallas_call(
        _compress_kernel,
        out_shape=(jax.ShapeDtypeStruct((nb, 2, 128, LANE), BF16),
                   jax.ShapeDtypeStruct((nb, LANE, 128), BF16)),
        grid=(nb,),
        in_specs=[pl.BlockSpec((seq, LANE), lambda b: (b, 0)),
                  pl.BlockSpec((seq, LANE), lambda b: (b, 0)),
                  full(pet), full(peb), full(wt), full(wb), full(b1), full(w2k), full(w2v)],
        out_specs=(pl.BlockSpec((None, 2, 128, LANE), lambda b: (b, 0, 0, 0)),
                   pl.BlockSpec((None, LANE, 128), lambda b: (b, 0, 0))),
        compiler_params=_params(1, VMEM_LIMIT),
        name="compress",
    )(kf, vf, pet, peb, wt, wb, b1, w2k, w2v)


def _cmpsel_kernel(q_ref, kca_ref, vct_ref, gnt_ref, ocmp_ref, sel_ref, s_scr):
    nq = q_ref.shape[0] // T
    lane = lax.broadcasted_iota(jnp.int32, (T, LANE), 1)
    crow = lax.broadcasted_iota(jnp.int32, (128, T), 0)
    tcol0 = lax.broadcasted_iota(jnp.int32, (128, T), 1)
    tlane = lax.broadcasted_iota(jnp.int32, (1, T), 1)

    jj = lax.broadcasted_iota(jnp.int32, (32, 128), 0) * SLC_LEN
    cc = lax.broadcasted_iota(jnp.int32, (32, 128), 1) * CMP_STRIDE
    ov = jnp.maximum(jnp.minimum(cc + CMP_LEN, jj + SLC_LEN) - jnp.maximum(cc, jj), 0)
    ovt = (ov.astype(F32) * (1.0 / CMP_LEN)).astype(BF16)

    jrow = lax.broadcasted_iota(jnp.int32, (32, T), 0)
    jrow8 = lax.broadcasted_iota(jnp.int32, (8, T), 0)
    tblk0 = lax.broadcasted_iota(jnp.int32, (32, T), 1)

    def scores(i, buf):
        rows = pl.ds(pl.multiple_of(i * T, T), T)
        for hp in range(NSA_HEADS // 2):
            g = hp // (NSA_HPG // 2)
            qp = q_ref[rows, LANE * hp:LANE * (hp + 1)].astype(F32)
            qs = (jnp.where(lane < 64, qp, 0.0), jnp.where(lane < 64, pltpu.roll(qp, 64, 1), 0.0))
            for par in range(2):
                s_scr[buf, 2 * hp + par] = _nt(kca_ref[g], qs[par].astype(BF16))

    def attend_select(i, buf):
        t0 = i * T
        rows = pl.ds(pl.multiple_of(t0, T), T)
        cmask = (crow * CMP_STRIDE + (CMP_LEN - 1)) <= tcol0 + t0
        live = jnp.where(tlane + t0 >= (CMP_LEN - 1), 1.0, 0.0)
        tblk = (tblk0 + t0) // SLC_LEN
        valid = jrow <= tblk
        forced = (jrow == 0) | (jrow == tblk) | (jrow == tblk - 1)
        bonus = jnp.where(forced, FORCE_BONUS, 0.0)
        for g in range(NSA_GROUPS):
            vt = vct_ref[64 * g:64 * (g + 1), :]
            psum = jnp.zeros((128, T), F32)
            for hp in range(NSA_HPG // 2):
                outs = []
                for par in range(2):
                    h = NSA_HPG * g + 2 * hp + par
                    s = jnp.where(cmask, s_scr[buf, h], NEG)
                    m = jnp.max(s, axis=0, keepdims=True)
                    e = jnp.exp2(s - m)
                    p = e * (live / jnp.sum(e, axis=0, keepdims=True))
                    psum = psum + p
                    outs.append(_nn(vt, p.astype(BF16)) * gnt_ref[i, 3 * h:3 * h + 1, :])
                ocmp_ref[rows, LANE * (2 * g + hp):LANE * (2 * g + hp + 1)] = (
                    jnp.concatenate(outs, axis=0).T.astype(BF16))

            p_hi = psum.astype(BF16)
            p_lo = (psum - p_hi.astype(F32)).astype(BF16)
            pslc = _nn(ovt, p_hi) + _nn(ovt, p_lo)
            pri = jnp.where(valid, pslc + bonus, -1.0)
            rank = [jnp.zeros((8, T), F32) for _ in range(4)]
            for r in range(32):
                row = pri[r:r + 1, :]
                for a in range(4):
                    pa = pri[8 * a:8 * (a + 1), :]
                    if 8 * a > r:
                        ahead = jnp.where(row >= pa, 1.0, 0.0)
                    elif 8 * a + 7 < r:
                        ahead = jnp.where(row > pa, 1.0, 0.0)
                    else:
                        ahead = jnp.where(jrow8 + 8 * a > r, jnp.where(row >= pa, 1.0, 0.0),
                                          jnp.where(row > pa, 1.0, 0.0))
                    rank[a] = rank[a] + ahead
            rank = jnp.concatenate(rank, axis=0)
            selneg = jnp.where(rank < float(SLC_TOPK), 0.0, NEG)
            pad = jnp.concatenate([jnp.zeros((64, T), F32), selneg, jnp.zeros((32, T), F32)], axis=0)
            sel_ref[rows, LANE * g:LANE * (g + 1)] = pad.T.astype(BF16)

    scores(0, 0)

    def body(u, carry):
        scores(2 * u + 1, 1)
        attend_select(2 * u, 0)
        scores(jnp.minimum(2 * u + 2, nq - 1), 0)
        attend_select(2 * u + 1, 1)
        return carry

    lax.fori_loop(0, nq // 2, body, 0)


def _cmpsel(qraw, kca, vct, gnt, nb, seq):
    n = qraw.shape[0]
    nq = seq // T
    assert nq % 2 == 0
    return pl.pallas_call(
        _cmpsel_kernel,
        out_shape=(jax.ShapeDtypeStruct((n, 512), BF16), jax.ShapeDtypeStruct((n, 256), BF16)),
        grid=(nb,),
        in_specs=[pl.BlockSpec((seq, 512), lambda b: (b, 0)),
                  pl.BlockSpec((None, 2, 128, LANE), lambda b: (b, 0, 0, 0)),
                  pl.BlockSpec((None, LANE, 128), lambda b: (b, 0, 0)),
                  pl.BlockSpec((nq, GATE_ROWS, T), lambda b: (b, 0, 0))],
        out_specs=(pl.BlockSpec((seq, 512), lambda b: (b, 0)),
                   pl.BlockSpec((seq, 256), lambda b: (b, 0))),
        scratch_shapes=[pltpu.VMEM((2, NSA_HEADS, 128, T), F32)],
        compiler_params=_params(1, VMEM_LIMIT),
        name="cmpsel",
    )(qraw, kca, vct, gnt)


def _split_heads(q_ref, extras, qs_scr, base=0):
    lane = lax.broadcasted_iota(jnp.int32, (T, LANE), 1)
    for hp in range(NSA_HEADS // 2):
        extra = extras[hp // (NSA_HPG // 2)]
        qp = q_ref[:, LANE * hp:LANE * (hp + 1)].astype(F32)
        qs_scr[base + 2 * hp] = jnp.where(lane < 64, qp, extra).astype(BF16)
        qs_scr[base + 2 * hp + 1] = jnp.where(lane < 64, pltpu.roll(qp, 64, 1), extra).astype(BF16)


SM_ROWS = 32


RING = 4


def _tile_schedule(nq):
    below = [(i, j) for j in range(nq) for i in range(j + 1, nq)]
    assert len(below) == (RING - 1) * nq + RING
    pairs = []
    for i in range(nq):
        pairs += [(i, i)] + below[(RING - 1) * i:(RING - 1) * (i + 1)]
    pairs += below[(RING - 1) * nq:]
    return (jnp.asarray([p[0] for p in pairs], jnp.int32), jnp.asarray([p[1] for p in pairs], jnp.int32))


def _flash_scratch(nq, nc, acc_rows):
    return [pltpu.VMEM((nq * nc, T, LANE), BF16),
            pltpu.VMEM((RING, nc, T, T), F32),
            pltpu.VMEM((RING, nc, 1, T), F32),
            pltpu.VMEM((nq * nc, 1, T), F32),
            pltpu.VMEM((nq * nc, 1, T), F32),
            pltpu.VMEM((nq * acc_rows // LANE, LANE, T), F32)]


def _causal_flash(it_ref, jt_ref, nq, k_ref, vt_ref, qs_scr, s_scr, mx_scr, m_scr, l_scr, acc_scr, chains, acc_of):
    nc = len(chains)
    nsets = it_ref.shape[0]
    assert nsets % RING == 0 and nsets >= 2 * RING
    m_scr[...] = jnp.full(m_scr.shape, NEG, F32)
    l_scr[...] = jnp.zeros(l_scr.shape, F32)
    acc_scr[...] = jnp.zeros(acc_scr.shape, F32)
    krow = lax.broadcasted_iota(jnp.int32, (T, T), 0)
    qcol = lax.broadcasted_iota(jnp.int32, (T, T), 1)
    ones = jnp.ones((16, T), BF16)

    def scores(n, buf):
        i, j = it_ref[n], jt_ref[n]
        if buf == 0:
            mask = krow <= qcol + jnp.where(i == j, 0, T)
        for c, (kl, _, _) in enumerate(chains):
            s = _nt(k_ref[j, :, kl:kl + LANE], qs_scr[i * nc + c])
            if buf == 0:
                s = jnp.where(mask, s, NEG)
            s_scr[buf, c] = s
            mx_scr[buf, c] = jnp.max(s, axis=0, keepdims=True)

    def softmax_pv(n, buf):
        i, j = it_ref[n], jt_ref[n]
        for c, (_, vr, dv) in enumerate(chains):
            m_prev = m_scr[i * nc + c]
            m_new = jnp.maximum(m_prev, mx_scr[buf, c])
            alpha = jnp.exp2(m_prev - m_new)
            parts = [jnp.exp2(s_scr[buf, c, r:r + SM_ROWS, :] - m_new).astype(BF16)
                     for r in range(0, T, SM_ROWS)]
            vt = jnp.concatenate([vt_ref[j, vr:vr + dv, :], ones], axis=0)
            o = _nn(vt, jnp.concatenate(parts, axis=0))
            l_scr[i * nc + c] = alpha * l_scr[i * nc + c] + o[dv:dv + 1]
            m_scr[i * nc + c] = m_new
            acc = acc_of(i, c)
            acc[...] = acc[...] * alpha + o[0:dv]

    scores(0, 0)

    def body(u, carry):
        n = RING * u
        for r in range(RING):
            scores(n + r + 1, (r + 1) % RING)
            softmax_pv(n + r, r)
        return carry

    lax.fori_loop(0, nsets // RING - 1, body, 0)
    n = nsets - RING
    for r in range(RING):
        if r + 1 < RING:
            scores(n + r + 1, r + 1)
        softmax_pv(n + r, r)


def _slc_kernel(it_ref, jt_ref, q_ref, sel_ref, k_ref, vt_ref, gnt_ref, o_ref,
                qs_scr, s_scr, mx_scr, m_scr, l_scr, acc_scr):
    nq = q_ref.shape[0] // T
    nh = NSA_HEADS

    def prep(i, carry):
        rows = pl.ds(pl.multiple_of(i * T, T), T)
        sel = sel_ref[rows, :].astype(F32)
        _split_heads(q_ref.at[rows, :], [sel[:, 0:LANE], sel[:, LANE:2 * LANE]], qs_scr, i * nh)
        return carry

    lax.fori_loop(0, nq, prep, 0)
    chains = [(LANE * (h // NSA_HPG), 64 * (h // NSA_HPG), 64) for h in range(nh)]
    acc_of = lambda i, h: acc_scr.at[i * (nh // 2) + h // 2, 64 * (h % 2):64 * (h % 2 + 1), :]
    _causal_flash(it_ref, jt_ref, nq, k_ref, vt_ref, qs_scr, s_scr, mx_scr, m_scr, l_scr, acc_scr, chains, acc_of)

    def finish(i, carry):
        rows = pl.ds(pl.multiple_of(i * T, T), T)
        for hp in range(nh // 2):
            scale = [gnt_ref[i, 3 * h + 1:3 * h + 2, :] / l_scr[i * nh + h] for h in (2 * hp, 2 * hp + 1)]
            inv = jnp.concatenate([jnp.broadcast_to(scale[0], (64, T)), jnp.broadcast_to(scale[1], (64, T))],
                                  axis=0)
            o_ref[rows, LANE * hp:LANE * (hp + 1)] = (acc_scr[i * (nh // 2) + hp] * inv).T.astype(BF16)
        return carry

    lax.fori_loop(0, nq, finish, 0)


def _slc(qrot, sel, ksa3, vst, gnt3, nb, seq):
    n = qrot.shape[0]
    nq = seq // T
    it, jt = _tile_schedule(nq)
    grid_spec = pltpu.PrefetchScalarGridSpec(
        num_scalar_prefetch=2,
        grid=(nb,),
        in_specs=[pl.BlockSpec((seq, 512), lambda b, it, jt: (b, 0)),
                  pl.BlockSpec((seq, 256), lambda b, it, jt: (b, 0)),
                  pl.BlockSpec((nq, T, 256), lambda b, it, jt: (b, 0, 0)),
                  pl.BlockSpec((nq, LANE, T), lambda b, it, jt: (b, 0, 0)),
                  pl.BlockSpec((nq, GATE_ROWS, T), lambda b, it, jt: (b, 0, 0))],
        out_specs=pl.BlockSpec((seq, 512), lambda b, it, jt: (b, 0)),
        scratch_shapes=_flash_scratch(nq, NSA_HEADS, NSA_Q))
    return pl.pallas_call(
        _slc_kernel,
        out_shape=jax.ShapeDtypeStruct((n, 512), BF16),
        grid_spec=grid_spec,
        compiler_params=_params(1, VMEM_LIMIT),
        name="slc",
    )(it, jt, qrot, sel, ksa3, vst, gnt3)


def _win_kernel(q_ref, k_ref, vt_ref, gnt_ref, o_ref, qs_scr, s_scr, mx_scr, acc_scr):
    i = pl.program_id(1)
    _split_heads(q_ref, [0.0, 0.0], qs_scr)
    krow = lax.broadcasted_iota(jnp.int32, (T, T), 0)
    qcol = lax.broadcasted_iota(jnp.int32, (T, T), 1)
    span = WINDOW // T
    masks = {0: krow <= qcol, span: (krow > qcol) & (i >= span)}
    ones = jnp.ones((16, T), BF16)

    def scores(hp, buf):
        for par in range(2):
            h = 2 * hp + par
            g = h // NSA_HPG
            mx = None
            for d in range(span + 1):
                j = i - d
                s = _nt(k_ref[jnp.maximum(j, 0), :, LANE * g:LANE * (g + 1)], qs_scr[h])
                if d in masks:
                    s = jnp.where(masks[d], s, NEG)
                else:
                    s = s + jnp.where(j >= 0, 0.0, NEG)
                s_scr[buf, par, d] = s
                md = jnp.max(s, axis=0, keepdims=True)
                mx = md if mx is None else jnp.maximum(mx, md)
            mx_scr[buf, par] = mx

    def softmax_pv(hp, buf):
        for par in range(2):
            g = (2 * hp + par) // NSA_HPG
            m = mx_scr[buf, par]
            acc = jnp.zeros((64 + 16, T), F32)
            for d in range(span + 1):
                parts = [jnp.exp2(s_scr[buf, par, d, r:r + SM_ROWS, :] - m).astype(BF16)
                         for r in range(0, T, SM_ROWS)]
                vt = jnp.concatenate([vt_ref[jnp.maximum(i - d, 0), 64 * g:64 * (g + 1), :], ones], axis=0)
                acc = acc + _nn(vt, jnp.concatenate(parts, axis=0))
            gate = gnt_ref[3 * (2 * hp + par) + 2:3 * (2 * hp + par) + 3, :]
            acc_scr[hp, 64 * par:64 * (par + 1), :] = acc[0:64] * (gate / acc[64:65])

    npairs = NSA_HEADS // 2
    scores(0, 0)
    for hp in range(npairs):
        if hp + 1 < npairs:
            scores(hp + 1, (hp + 1) % 2)
        softmax_pv(hp, hp % 2)
    for hp in range(npairs):
        o_ref[:, LANE * hp:LANE * (hp + 1)] = acc_scr[hp].T.astype(BF16)


def _win(qrot, kwa3, vwt, gnt, nb, seq):
    n = qrot.shape[0]
    nq = seq // T
    return pl.pallas_call(
        _win_kernel,
        out_shape=jax.ShapeDtypeStruct((n, 512), BF16),
        grid=(nb, nq),
        in_specs=[pl.BlockSpec((T, 512), lambda b, i: (b * nq + i, 0)),
                  pl.BlockSpec((nq, T, 256), lambda b, i: (b, 0, 0)),
                  pl.BlockSpec((nq, LANE, T), lambda b, i: (b, 0, 0)),
                  pl.BlockSpec((None, GATE_ROWS, T), lambda b, i: (b * nq + i, 0, 0))],
        out_specs=pl.BlockSpec((T, 512), lambda b, i: (b * nq + i, 0)),
        scratch_shapes=[pltpu.VMEM((NSA_HEADS, T, LANE), BF16),
                        pltpu.VMEM((2, 2, WINDOW // T + 1, T, T), F32),
                        pltpu.VMEM((2, 2, 1, T), F32),
                        pltpu.VMEM((NSA_HEADS // 2, LANE, T), F32)],
        compiler_params=_params(2, VMEM_LIMIT),
        name="win",
    )(qrot, kwa3, vwt, gnt)


def _diff_kernel(it_ref, jt_ref, q_ref, k_ref, vt_ref, lam_ref, g_ref, o_ref,
                 qs_scr, s_scr, mx_scr, m_scr, l_scr, acc_scr):
    nq = q_ref.shape[0] // T
    nc = 2 * DIFF_HEADS
    lane = lax.broadcasted_iota(jnp.int32, (T, LANE), 1)

    def prep(i, carry):
        rows = pl.ds(pl.multiple_of(i * T, T), T)
        for h in range(DIFF_HEADS):
            q = q_ref[rows, LANE * h:LANE * (h + 1)].astype(F32)
            qs_scr[i * nc + 2 * h] = jnp.where(lane < 64, q, 0.0).astype(BF16)
            qs_scr[i * nc + 2 * h + 1] = jnp.where(lane < 64, 0.0, q).astype(BF16)
        return carry

    lax.fori_loop(0, nq, prep, 0)
    chains = [(LANE * (c // 2), LANE * (c // 2), LANE) for c in range(nc)]
    _causal_flash(it_ref, jt_ref, nq, k_ref, vt_ref, qs_scr, s_scr, mx_scr, m_scr, l_scr, acc_scr, chains,
                  lambda i, c: acc_scr.at[i * nc + c])

    lp = lam_ref[...]
    lam = (jnp.exp(jnp.sum(lp[0:1] * lp[1:2], axis=1, keepdims=True))
           - jnp.exp(jnp.sum(lp[2:3] * lp[3:4], axis=1, keepdims=True)) + LAMBDA_INIT)

    def finish(i, carry):
        rows = pl.ds(pl.multiple_of(i * T, T), T)
        for h in range(DIFF_HEADS):
            c = i * nc + 2 * h
            o = acc_scr[c] * (1.0 / l_scr[c]) - lam * (acc_scr[c + 1] * (1.0 / l_scr[c + 1]))
            o = o * lax.rsqrt(jnp.mean(o * o, axis=0, keepdims=True) + RMS_EPS)
            o_ref[rows, LANE * h:LANE * (h + 1)] = ((o.T * g_ref[...]) * (1.0 - LAMBDA_INIT)).astype(BF16)
        return carry

    lax.fori_loop(0, nq, finish, 0)


def _diff(qdf, kdf3, vdft, lam, norm_g, nb, seq):
    n = qdf.shape[0]
    nq = seq // T
    it, jt = _tile_schedule(nq)
    grid_spec = pltpu.PrefetchScalarGridSpec(
        num_scalar_prefetch=2,
        grid=(nb,),
        in_specs=[pl.BlockSpec((seq, 512), lambda b, it, jt: (b, 0)),
                  pl.BlockSpec((nq, T, 512), lambda b, it, jt: (b, 0, 0)),
                  pl.BlockSpec((nq, 512, T), lambda b, it, jt: (b, 0, 0)),
                  pl.BlockSpec((4, HEAD_DIM), lambda b, it, jt: (0, 0)),
                  pl.BlockSpec((1, LANE), lambda b, it, jt: (0, 0))],
        out_specs=pl.BlockSpec((seq, 512), lambda b, it, jt: (b, 0)),
        scratch_shapes=_flash_scratch(nq, 2 * DIFF_HEADS, 2 * DIFF_HEADS * DIFF_V_DIM))
    return pl.pallas_call(
        _diff_kernel,
        out_shape=jax.ShapeDtypeStruct((n, 512), BF16),
        grid_spec=grid_spec,
        compiler_params=_params(1, VMEM_LIMIT),
        name="diff",
    )(it, jt, qdf, kdf3, vdft, lam, norm_g)


def _layer_norm(z, g, b):
    mu = jnp.mean(z, axis=-1, keepdims=True)
    zc = z - mu
    var = jnp.mean(zc * zc, axis=-1, keepdims=True)
    return zc * lax.rsqrt(var + LN_EPS) * g + b


def _mlp_kernel(x_ref, ocmp_ref, oslc_ref, owin_ref, yb_ref, gm_ref, wa_ref, wb_ref, wo_ref, g1_ref, b1_ref,
                wgu_ref, wd_ref, g2_ref, b2_ref, o_ref, h_scr):
    s = pl.program_id(0)

    @pl.when(s == 0)
    def _():
        h_scr[1] = jnp.zeros(h_scr.shape[1:], F32)

    h = h_scr[(s + 1) % 2]
    hb = h.astype(BF16)
    gate = _nn(hb, wgu_ref[:, 0:FFN_HIDDEN])
    up = _nn(hb, wgu_ref[:, FFN_HIDDEN:2 * FFN_HIDDEN])
    act = (gate * jax.nn.sigmoid(gate) * up).astype(BF16)
    y = _nn(act, wd_ref[...])
    o_ref[...] = _layer_norm(DEEPNORM_ALPHA * h + y, g2_ref[...], b2_ref[...])

    ya = ocmp_ref[...].astype(F32) + oslc_ref[...].astype(F32) + owin_ref[...].astype(F32)
    ta = _nn(ya.astype(BF16), wa_ref[...])
    tb = _nn(yb_ref[...], wb_ref[...])
    merged = gm_ref[:, 0:D_MODEL].astype(F32) * ta + gm_ref[:, D_MODEL:2 * D_MODEL].astype(F32) * tb
    mix = _nn(merged.astype(BF16), wo_ref[...])
    h_scr[s % 2] = _layer_norm(DEEPNORM_ALPHA * x_ref[...] + mix, g1_ref[...], b1_ref[...])


def _mlp(x2, ocmp, oslc, owin, yb, gm, wa, wb, wo, g1, b1, wgu, wd, g2, b2):
    n = x2.shape[0]
    nt = n // TM
    row = lambda w: pl.BlockSpec((TM, w), lambda s: (jnp.minimum(s, nt - 1), 0))
    full = lambda a: pl.BlockSpec(a.shape, lambda s: (0,) * a.ndim, pipeline_mode=pl.Buffered(1))
    return pl.pallas_call(
        _mlp_kernel,
        out_shape=jax.ShapeDtypeStruct((n, D_MODEL), F32),
        grid=(nt + 1,),
        in_specs=[row(D_MODEL), row(512), row(512), row(512), row(512), row(2048),
                  full(wa), full(wb), full(wo), full(g1), full(b1), full(wgu), full(wd), full(g2), full(b2)],
        out_specs=pl.BlockSpec((TM, D_MODEL), lambda s: (jnp.maximum(s - 1, 0), 0)),
        scratch_shapes=[pltpu.VMEM((2, TM, D_MODEL), F32)],
        compiler_params=_params(1, VMEM_LIMIT),
        name="mlp",
    )(x2, ocmp, oslc, owin, yb, gm, wa, wb, wo, g1, b1, wgu, wd, g2, b2)


def _rope_tables(seq):
    half = ROPE_DIM // 2
    inv_freq = ROPE_THETA ** (-jnp.arange(half, dtype=F32) * 2.0 / ROPE_DIM)
    ang = jnp.arange(seq, dtype=F32)[:, None] * inv_freq[None, :]
    cos, sin = jnp.cos(ang), jnp.sin(ang)
    ones = jnp.ones((seq, HEAD_DIM - ROPE_DIM), F32)
    zeros8 = jnp.zeros((seq, half), F32)
    zeros48 = jnp.zeros((seq, HEAD_DIM - ROPE_DIM), F32)
    c64 = jnp.concatenate([cos, cos, ones], axis=1)
    sa64 = jnp.concatenate([-sin, zeros8, zeros48], axis=1)
    sb64 = jnp.concatenate([zeros8, sin, zeros48], axis=1)
    rep = lambda a: jnp.concatenate([a, a], axis=1)
    pos_blk = jnp.arange(seq, dtype=jnp.int32)[:, None] // SLC_LEN
    lane = jnp.arange(LANE, dtype=jnp.int32)[None, :]
    onehot = ((lane >= 64) & (lane < 96) & (lane - 64 == pos_blk)).astype(F32)
    return rep(c64), rep(sa64), rep(sb64), onehot


def _prep_compress(cmp_pe, cmp_w1, cmp_b1, cmp_w2):
    half = CMP_LEN // 2
    pet, peb, wt, wb, b1 = [], [], [], [], []
    for kv in range(2):
        pe = cmp_pe[kv]
        tile2 = lambda a: jnp.concatenate([a, a], axis=1).reshape(1, half * 2 * HEAD_DIM)
        pet.append(tile2(pe[:half]))
        peb.append(tile2(pe[half:]))
        w1 = cmp_w1[kv].reshape(CMP_LEN, HEAD_DIM, CMP_HIDDEN)
        z = jnp.zeros((half, HEAD_DIM, CMP_HIDDEN), w1.dtype)

        def spread(wh):
            g0 = jnp.concatenate([wh, z], axis=2)
            g1 = jnp.concatenate([z, wh], axis=2)
            return jnp.stack([g0, g1], axis=1).reshape(half * 2 * HEAD_DIM, 2 * CMP_HIDDEN)

        wt.append(spread(w1[:half]))
        wb.append(spread(w1[half:]))
        b1.append(jnp.concatenate([cmp_b1[kv], cmp_b1[kv]])[None, :])
    w2k, w2v = cmp_w2[0], cmp_w2[1]
    zk = jnp.zeros_like(w2k)
    w2k_g = jnp.stack([
        jnp.concatenate([jnp.concatenate([w2k, zk], axis=1), jnp.zeros((CMP_HIDDEN, LANE), w2k.dtype)], axis=0),
        jnp.concatenate([jnp.zeros((CMP_HIDDEN, LANE), w2k.dtype), jnp.concatenate([w2k, zk], axis=1)], axis=0),
    ])
    zv = jnp.zeros_like(w2v)
    w2v_bd = jnp.concatenate([jnp.concatenate([w2v, zv], axis=1),
                              jnp.concatenate([zv, w2v], axis=1)], axis=0)
    st = lambda xs: jnp.stack(xs)
    return (st(pet), st(peb), st(wt).astype(BF16), st(wb).astype(BF16), st(b1),
            w2k_g.astype(BF16), w2v_bd.astype(BF16))


def kernel(x, w_in, cmp_pe, cmp_w1, cmp_b1, cmp_w2, diff_lambda, diff_norm_g, w_branch_a, w_branch_b,
           w_o, ln1_g, ln1_b, w_gate_up, w_down, ln2_g, ln2_b):
    nb, seq, d = x.shape
    assert d == D_MODEL and seq % T == 0 and seq // SLC_LEN == 32 and seq // CMP_STRIDE == 128
    n = nb * seq
    x2 = x.reshape(n, d)
    cos_t, sa_t, sb_t, oh_t = _rope_tables(seq)
    (qraw, qrot, kcs, vcs, ksa, vst, kwa, vwt, qdf, kdf, vdft, gm, gnt) = _inproj(
        x2, w_in.reshape(w_in.shape[1:]).T, cos_t, sa_t, sb_t, oh_t, seq)

    pet, peb, wt, wb, b1, w2k, w2v = _prep_compress(cmp_pe[0], cmp_w1[0], cmp_b1[0], cmp_w2[0])
    kca, vct = _compress(kcs, vcs, pet, peb, wt, wb, b1, w2k, w2v, nb, seq)

    ocmp, sel = _cmpsel(qraw, kca, vct, gnt, nb, seq)
    nt = n // T
    oslc = _slc(qrot, sel, ksa.reshape(nt, T, 256), vst, gnt, nb, seq)
    owin = _win(qrot, kwa.reshape(nt, T, 256), vwt, gnt, nb, seq)
    yb = _diff(qdf, kdf.reshape(nt, T, 512), vdft, diff_lambda[0], diff_norm_g[0][None, :], nb, seq)

    out = _mlp(x2, ocmp, oslc, owin, yb, gm,
               w_branch_a[0].astype(BF16), w_branch_b[0].astype(BF16), w_o[0].astype(BF16),
               ln1_g[0][None, :], ln1_b[0][None, :],
               w_gate_up[0].astype(BF16), w_down[0].astype(BF16), ln2_g[0][None, :], ln2_b[0][None, :])
    return out.reshape(nb, seq, d)
```

```python
import math

import jax
import jax.numpy as jnp
from jax import lax
from jax.experimental import pallas as pl
from jax.experimental.pallas import tpu as pltpu

D_MODEL = 1024
HEAD_DIM = 64
ROPE_DIM = HEAD_DIM // 4
ROPE_THETA = 500000.0
NSA_HEADS = 8
NSA_GROUPS = 2
NSA_HPG = NSA_HEADS // NSA_GROUPS
CMP_LEN = 32
CMP_STRIDE = 16
CMP_HIDDEN = 256
SLC_LEN = 64
SLC_TOPK = 8
WINDOW = 512
FORCE_BONUS = 1.0e4
DIFF_HEADS = 4
DIFF_V_DIM = 2 * HEAD_DIM
FFN_HIDDEN = ((8 * D_MODEL // 3 + 255) // 256) * 256
DEPTH = 1
DEEPNORM_ALPHA = (2 * DEPTH) ** 0.25
NEG = -1.0e30
LN_EPS = 1e-5
RMS_EPS = 1e-5
LAMBDA_INIT = 0.8 - 0.6 * math.exp(-0.3 * 0)
QK_SCALE = HEAD_DIM ** -0.5
QK_SCALE_LOG2 = QK_SCALE * math.log2(math.e)

NSA_Q = NSA_HEADS * HEAD_DIM
NSA_KV = NSA_GROUPS * HEAD_DIM
DIFF_QK = DIFF_HEADS * 2 * HEAD_DIM
DIFF_V = DIFF_HEADS * DIFF_V_DIM
IN_WIDTHS = (NSA_Q, NSA_KV, NSA_KV, NSA_KV, NSA_KV, NSA_KV, NSA_KV, 3 * NSA_HEADS,
             DIFF_QK, DIFF_QK, DIFF_V, 2 * D_MODEL)

LANE = 128
T = 256
TM = 512
VMEM_LIMIT = 56 * 1024 * 1024
GATE_ROWS = 32

BF16 = jnp.bfloat16
F32 = jnp.float32

_C_Q, _C_KC, _C_VC, _C_KS, _C_VS, _C_KW, _C_VW = 0, 512, 640, 768, 896, 1024, 1152
_C_QD, _C_KD, _C_VD, _C_GM, _C_GN, _C_END = 1280, 1792, 2304, 2816, 4864, 4992


def _nt(a, b):
    return lax.dot_general(a, b, (((1,), (1,)), ((), ())), preferred_element_type=F32)


def _nn(a, b):
    return jnp.dot(a, b, preferred_element_type=F32)


def _params(n_axes, vmem=None):
    return pltpu.CompilerParams(dimension_semantics=("arbitrary",) * n_axes,
                                vmem_limit_bytes=vmem)


def _inproj_kernel(x_ref, w_ref, cos_ref, sa_ref, sb_ref, oh_ref,
                   qraw_ref, qrot_ref, kc_ref, vc_ref, ksa_ref, vst_ref, kwa_ref, vwt_ref,
                   qdf_ref, kdf_ref, vdft_ref, gm_ref, gn_ref, w_scr):
    @pl.when(pl.program_id(0) == 0)
    def _():
        gn0 = _C_QD
        gn1 = gn0 + 3 * NSA_HEADS
        rows = 256

        def regroup(r, carry):
            dst = pl.ds(pl.multiple_of(r * rows, rows), rows)
            src = pl.ds(pl.multiple_of(jnp.where(r * rows < gn0, r * rows, r * rows + gn1 - gn0), 8), rows)
            w_scr[dst, :] = w_ref[src, :].astype(BF16)
            return carry

        lax.fori_loop(0, _C_GN // rows, regroup, 0)
        gates = jnp.concatenate([w_ref[gn0:gn1, :], jnp.zeros((LANE - (gn1 - gn0), D_MODEL), F32)], axis=0)
        w_scr[_C_GN:_C_END, :] = gates.astype(BF16)

    xb = x_ref[...].astype(BF16)
    cos = cos_ref[...]
    sa = sa_ref[...]
    sb = sb_ref[...]

    def mm(c0, n):
        return _nt(xb, w_scr[c0:c0 + n, :])

    def rope(t):
        return t * cos + pltpu.roll(t, LANE - 8, 1) * sa + pltpu.roll(t, 8, 1) * sb

    t = mm(_C_Q, 512)
    for j in range(4):
        tj = t[:, LANE * j:LANE * (j + 1)]
        qraw_ref[:, LANE * j:LANE * (j + 1)] = (tj * QK_SCALE_LOG2).astype(BF16)
        qrot_ref[:, LANE * j:LANE * (j + 1)] = (rope(tj) * QK_SCALE_LOG2).astype(BF16)

    t = mm(_C_KC, 256)
    kc_ref[...] = t[:, :LANE]
    vc_ref[...] = t[:, LANE:]

    lane = lax.broadcasted_iota(jnp.int32, (TM, LANE), 1)
    tiles = [slice(T * u, T * (u + 1)) for u in range(TM // T)]

    def per_group(k, extra, out_ref):
        out_ref[:, 0:LANE] = jnp.where(lane < 64, k, extra).astype(BF16)
        out_ref[:, LANE:2 * LANE] = jnp.where(lane < 64, pltpu.roll(k, 64, 1), extra).astype(BF16)

    t = mm(_C_KS, 256)
    per_group(rope(t[:, :LANE]), oh_ref[...], ksa_ref)
    for u, rs in enumerate(tiles):
        vst_ref[u] = t[rs, LANE:].T.astype(BF16)

    t = mm(_C_KW, 256)
    per_group(rope(t[:, :LANE]), 0.0, kwa_ref)
    for u, rs in enumerate(tiles):
        vwt_ref[u] = t[rs, LANE:].T.astype(BF16)

    t = mm(_C_QD, 512)
    for j in range(4):
        qdf_ref[:, LANE * j:LANE * (j + 1)] = (rope(t[:, LANE * j:LANE * (j + 1)]) * QK_SCALE_LOG2).astype(BF16)
    t = mm(_C_KD, 512)
    for j in range(4):
        kdf_ref[:, LANE * j:LANE * (j + 1)] = rope(t[:, LANE * j:LANE * (j + 1)]).astype(BF16)
    t = mm(_C_VD, 512)
    for u, rs in enumerate(tiles):
        for j in range(4):
            vdft_ref[u, LANE * j:LANE * (j + 1), :] = t[rs, LANE * j:LANE * (j + 1)].T.astype(BF16)

    for j in range(4):
        t = mm(_C_GM + 512 * j, 512)
        gm_ref[:, 512 * j:512 * (j + 1)] = jax.nn.sigmoid(t).astype(BF16)

    t = jax.nn.sigmoid(mm(_C_GN, 128))
    for u, rs in enumerate(tiles):
        gn_ref[u] = t[rs].T[0:GATE_ROWS]


def _inproj(x2, w, cos_t, sa_t, sb_t, oh_t, seq):
    n = x2.shape[0]
    nt = n // T
    spt = seq // TM
    assert seq % TM == 0 and TM % T == 0
    assert w.shape == (sum(IN_WIDTHS), D_MODEL) and sum(IN_WIDTHS[:7]) == _C_QD
    assert sum(IN_WIDTHS) - 3 * NSA_HEADS == _C_GN and _C_QD % 256 == 0 and _C_GN % 256 == 0
    row = lambda w: pl.BlockSpec((TM, w), lambda i: (i, 0))
    whole = lambda a: pl.BlockSpec(a.shape, lambda i: (0, 0), pipeline_mode=pl.Buffered(1))
    tab = pl.BlockSpec((TM, LANE), lambda i: (i % spt, 0))
    tile_t = lambda r: pl.BlockSpec((TM // T, r, T), lambda i: (i, 0, 0))
    out_shape = (
        jax.ShapeDtypeStruct((n, 512), BF16),
        jax.ShapeDtypeStruct((n, 512), BF16),
        jax.ShapeDtypeStruct((n, LANE), F32),
        jax.ShapeDtypeStruct((n, LANE), F32),
        jax.ShapeDtypeStruct((n, 256), BF16),
        jax.ShapeDtypeStruct((nt, LANE, T), BF16),
        jax.ShapeDtypeStruct((n, 256), BF16),
        jax.ShapeDtypeStruct((nt, LANE, T), BF16),
        jax.ShapeDtypeStruct((n, 512), BF16),
        jax.ShapeDtypeStruct((n, 512), BF16),
        jax.ShapeDtypeStruct((nt, 512, T), BF16),
        jax.ShapeDtypeStruct((n, 2048), BF16),
        jax.ShapeDtypeStruct((nt, GATE_ROWS, T), F32),
    )
    out_specs = (row(512), row(512), row(LANE), row(LANE), row(256), tile_t(LANE), row(256),
                 tile_t(LANE), row(512), row(512), tile_t(512), row(2048), tile_t(GATE_ROWS))
    return pl.pallas_call(
        _inproj_kernel,
        out_shape=out_shape,
        grid=(n // TM,),
        in_specs=[row(D_MODEL), whole(w), tab, tab, tab, tab],
        out_specs=out_specs,
        scratch_shapes=[pltpu.VMEM((_C_END, D_MODEL), BF16)],
        compiler_params=_params(1, VMEM_LIMIT),
        name="inproj",
    )(x2, w, cos_t, sa_t, sb_t, oh_t)


def _gelu_tanh(x):
    c = math.sqrt(2.0 / math.pi)
    return x * (0.5 * (1.0 + jnp.tanh(c * (x + 0.044715 * (x * x * x)))))


def _compress_kernel(kf_ref, vf_ref, pet_ref, peb_ref, wt_ref, wb_ref, b1_ref, w2k_ref, w2v_ref,
                     kca_ref, vct_ref):
    nchunk = kf_ref.shape[0] // CMP_STRIDE

    def hidden(x_ref, kv):
        a = jnp.zeros((nchunk, 2 * CMP_HIDDEN), F32)
        b = jnp.zeros((nchunk, 2 * CMP_HIDDEN), F32)
        for p in range(0, CMP_STRIDE, 2):
            x = jnp.concatenate([x_ref[pl.ds(p, nchunk, stride=CMP_STRIDE), :],
                                 x_ref[pl.ds(p + 1, nchunk, stride=CMP_STRIDE), :]], axis=1)
            c0, c1 = LANE * p, LANE * (p + 2)
            a = a + _nn((x + pet_ref[kv, :, c0:c1]).astype(BF16), wt_ref[kv, c0:c1, :])
            b = b + _nn((x + peb_ref[kv, :, c0:c1]).astype(BF16), wb_ref[kv, c0:c1, :])
        h = a + pltpu.roll(b, nchunk - 1, 0) + b1_ref[kv]
        return _gelu_tanh(h).astype(BF16)

    hk = hidden(kf_ref, 0)
    for g in range(2):
        kca_ref[g] = _nn(hk, w2k_ref[g]).astype(BF16)
    hv = hidden(vf_ref, 1)
    vct_ref[...] = _nn(hv, w2v_ref[...]).T.astype(BF16)


def _compress(kf, vf, pet, peb, wt, wb, b1, w2k, w2v, nb, seq):
    full = lambda a: pl.BlockSpec(a.shape, lambda b: (0,) * a.ndim)
    return pl.pallas_call(
        _compress_kernel,
        out_shape=(jax.ShapeDtypeStruct((nb, 2, 128, LANE), BF16),
                   jax.ShapeDtypeStruct((nb, LANE, 128), BF16)),
        grid=(nb,),
        in_specs=[pl.BlockSpec((seq, LANE), lambda b: (b, 0)),
                  pl.BlockSpec((seq, LANE), lambda b: (b, 0)),
                  full(pet), full(peb), full(wt), full(wb), full(b1), full(w2k), full(w2v)],
        out_specs=(pl.BlockSpec((None, 2, 128, LANE), lambda b: (b, 0, 0, 0)),
                   pl.BlockSpec((None, LANE, 128), lambda b: (b, 0, 0))),
        compiler_params=_params(1, VMEM_LIMIT),
        name="compress",
    )(kf, vf, pet, peb, wt, wb, b1, w2k, w2v)


def _cmpsel_kernel(q_ref, kca_ref, vct_ref, gnt_ref, ocmp_ref, sel_ref, s_scr):
    nq = q_ref.shape[0] // T
    lane = lax.broadcasted_iota(jnp.int32, (T, LANE), 1)
    crow = lax.broadcasted_iota(jnp.int32, (128, T), 0)
    tcol0 = lax.broadcasted_iota(jnp.int32, (128, T), 1)
    tlane = lax.broadcasted_iota(jnp.int32, (1, T), 1)

    jj = lax.broadcasted_iota(jnp.int32, (32, 128), 0) * SLC_LEN
    cc = lax.broadcasted_iota(jnp.int32, (32, 128), 1) * CMP_STRIDE
    ov = jnp.maximum(jnp.minimum(cc + CMP_LEN, jj + SLC_LEN) - jnp.maximum(cc, jj), 0)
    ovt = (ov.astype(F32) * (1.0 / CMP_LEN)).astype(BF16)

    jrow = lax.broadcasted_iota(jnp.int32, (32, T), 0)
    jrow8 = lax.broadcasted_iota(jnp.int32, (8, T), 0)
    tblk0 = lax.broadcasted_iota(jnp.int32, (32, T), 1)

    def scores(i, buf):
        rows = pl.ds(pl.multiple_of(i * T, T), T)
        for hp in range(NSA_HEADS // 2):
            g = hp // (NSA_HPG // 2)
            qp = q_ref[rows, LANE * hp:LANE * (hp + 1)].astype(F32)
            qs = (jnp.where(lane < 64, qp, 0.0), jnp.where(lane < 64, pltpu.roll(qp, 64, 1), 0.0))
            for par in range(2):
                s_scr[buf, 2 * hp + par] = _nt(kca_ref[g], qs[par].astype(BF16))

    def attend_select(i, buf):
        t0 = i * T
        rows = pl.ds(pl.multiple_of(t0, T), T)
        cmask = (crow * CMP_STRIDE + (CMP_LEN - 1)) <= tcol0 + t0
        live = jnp.where(tlane + t0 >= (CMP_LEN - 1), 1.0, 0.0)
        tblk = (tblk0 + t0) // SLC_LEN
        valid = jrow <= tblk
        forced = (jrow == 0) | (jrow == tblk) | (jrow == tblk - 1)
        bonus = jnp.where(forced, FORCE_BONUS, 0.0)
        for g in range(NSA_GROUPS):
            vt = vct_ref[64 * g:64 * (g + 1), :]
            psum = jnp.zeros((128, T), F32)
            for hp in range(NSA_HPG // 2):
                outs = []
                for par in range(2):
                    h = NSA_HPG * g + 2 * hp + par
                    s = jnp.where(cmask, s_scr[buf, h], NEG)
                    m = jnp.max(s, axis=0, keepdims=True)
                    e = jnp.exp2(s - m)
                    p = e * (live / jnp.sum(e, axis=0, keepdims=True))
                    psum = psum + p
                    outs.append(_nn(vt, p.astype(BF16)) * gnt_ref[i, 3 * h:3 * h + 1, :])
                ocmp_ref[rows, LANE * (2 * g + hp):LANE * (2 * g + hp + 1)] = (
                    jnp.concatenate(outs, axis=0).T.astype(BF16))

            p_hi = psum.astype(BF16)
            p_lo = (psum - p_hi.astype(F32)).astype(BF16)
            pslc = _nn(ovt, p_hi) + _nn(ovt, p_lo)
            pri = jnp.where(valid, pslc + bonus, -1.0)
            rank = [jnp.zeros((8, T), F32) for _ in range(4)]
            for r in range(32):
                row = pri[r:r + 1, :]
                for a in range(4):
                    pa = pri[8 * a:8 * (a + 1), :]
                    if 8 * a > r:
                        ahead = jnp.where(row >= pa, 1.0, 0.0)
                    elif 8 * a + 7 < r:
                        ahead = jnp.where(row > pa, 1.0, 0.0)
                    else:
                        ahead = jnp.where(jrow8 + 8 * a > r, jnp.where(row >= pa, 1.0, 0.0),
                                          jnp.where(row > pa, 1.0, 0.0))
                    rank[a] = rank[a] + ahead
            rank = jnp.concatenate(rank, axis=0)
            selneg = jnp.where(rank < float(SLC_TOPK), 0.0, NEG)
            pad = jnp.concatenate([jnp.zeros((64, T), F32), selneg, jnp.zeros((32, T), F32)], axis=0)
            sel_ref[rows, LANE * g:LANE * (g + 1)] = pad.T.astype(BF16)

    scores(0, 0)

    def body(u, carry):
        scores(2 * u + 1, 1)
        attend_select(2 * u, 0)
        scores(jnp.minimum(2 * u + 2, nq - 1), 0)
        attend_select(2 * u + 1, 1)
        return carry

    lax.fori_loop(0, nq // 2, body, 0)


def _cmpsel(qraw, kca, vct, gnt, nb, seq):
    n = qraw.shape[0]
    nq = seq // T
    assert nq % 2 == 0
    return pl.pallas_call(
        _cmpsel_kernel,
        out_shape=(jax.ShapeDtypeStruct((n, 512), BF16), jax.ShapeDtypeStruct((n, 256), BF16)),
        grid=(nb,),
        in_specs=[pl.BlockSpec((seq, 512), lambda b: (b, 0)),
                  pl.BlockSpec((None, 2, 128, LANE), lambda b: (b, 0, 0, 0)),
                  pl.BlockSpec((None, LANE, 128), lambda b: (b, 0, 0)),
                  pl.BlockSpec((nq, GATE_ROWS, T), lambda b: (b, 0, 0))],
        out_specs=(pl.BlockSpec((seq, 512), lambda b: (b, 0)),
                   pl.BlockSpec((seq, 256), lambda b: (b, 0))),
        scratch_shapes=[pltpu.VMEM((2, NSA_HEADS, 128, T), F32)],
        compiler_params=_params(1, VMEM_LIMIT),
        name="cmpsel",
    )(qraw, kca, vct, gnt)


def _split_heads(q_ref, extras, qs_scr, base=0):
    lane = lax.broadcasted_iota(jnp.int32, (T, LANE), 1)
    for hp in range(NSA_HEADS // 2):
        extra = extras[hp // (NSA_HPG // 2)]
        qp = q_ref[:, LANE * hp:LANE * (hp + 1)].astype(F32)
        qs_scr[base + 2 * hp] = jnp.where(lane < 64, qp, extra).astype(BF16)
        qs_scr[base + 2 * hp + 1] = jnp.where(lane < 64, pltpu.roll(qp, 64, 1), extra).astype(BF16)


SM_ROWS = 32


RING = 4


def _tile_schedule(nq):
    below = [(i, j) for j in range(nq) for i in range(j + 1, nq)]
    assert len(below) == (RING - 1) * nq + RING
    pairs = []
    for i in range(nq):
        pairs += [(i, i)] + below[(RING - 1) * i:(RING - 1) * (i + 1)]
    pairs += below[(RING - 1) * nq:]
    return (jnp.asarray([p[0] for p in pairs], jnp.int32), jnp.asarray([p[1] for p in pairs], jnp.int32))


def _flash_scratch(nq, nc, acc_rows):
    return [pltpu.VMEM((nq * nc, T, LANE), BF16),
            pltpu.VMEM((RING, nc, T, T), F32),
            pltpu.VMEM((RING, nc, 1, T), F32),
            pltpu.VMEM((nq * nc, 1, T), F32),
            pltpu.VMEM((nq * nc, 1, T), F32),
            pltpu.VMEM((nq * acc_rows // LANE, LANE, T), F32)]


def _causal_flash(it_ref, jt_ref, nq, k_ref, vt_ref, qs_scr, s_scr, mx_scr, m_scr, l_scr, acc_scr, chains, acc_of):
    nc = len(chains)
    nsets = it_ref.shape[0]
    assert nsets % RING == 0 and nsets >= 2 * RING
    m_scr[...] = jnp.full(m_scr.shape, NEG, F32)
    l_scr[...] = jnp.zeros(l_scr.shape, F32)
    acc_scr[...] = jnp.zeros(acc_scr.shape, F32)
    krow = lax.broadcasted_iota(jnp.int32, (T, T), 0)
    qcol = lax.broadcasted_iota(jnp.int32, (T, T), 1)
    ones = jnp.ones((16, T), BF16)

    def scores(n, buf):
        i, j = it_ref[n], jt_ref[n]
        if buf == 0:
            mask = krow <= qcol + jnp.where(i == j, 0, T)
        for c, (kl, _, _) in enumerate(chains):
            s = _nt(k_ref[j, :, kl:kl + LANE], qs_scr[i * nc + c])
            if buf == 0:
                s = jnp.where(mask, s, NEG)
            s_scr[buf, c] = s
            mx_scr[buf, c] = jnp.max(s, axis=0, keepdims=True)

    def softmax_pv(n, buf):
        i, j = it_ref[n], jt_ref[n]
        for c, (_, vr, dv) in enumerate(chains):
            m_prev = m_scr[i * nc + c]
            m_new = jnp.maximum(m_prev, mx_scr[buf, c])
            alpha = jnp.exp2(m_prev - m_new)
            parts = [jnp.exp2(s_scr[buf, c, r:r + SM_ROWS, :] - m_new).astype(BF16)
                     for r in range(0, T, SM_ROWS)]
            vt = jnp.concatenate([vt_ref[j, vr:vr + dv, :], ones], axis=0)
            o = _nn(vt, jnp.concatenate(parts, axis=0))
            l_scr[i * nc + c] = alpha * l_scr[i * nc + c] + o[dv:dv + 1]
            m_scr[i * nc + c] = m_new
            acc = acc_of(i, c)
            acc[...] = acc[...] * alpha + o[0:dv]

    scores(0, 0)

    def body(u, carry):
        n = RING * u
        for r in range(RING):
            scores(n + r + 1, (r + 1) % RING)
            softmax_pv(n + r, r)
        return carry

    lax.fori_loop(0, nsets // RING - 1, body, 0)
    n = nsets - RING
    for r in range(RING):
        if r + 1 < RING:
            scores(n + r + 1, r + 1)
        softmax_pv(n + r, r)


def _slc_kernel(it_ref, jt_ref, q_ref, sel_ref, k_ref, vt_ref, gnt_ref, o_ref,
                qs_scr, s_scr, mx_scr, m_scr, l_scr, acc_scr):
    nq = q_ref.shape[0] // T
    nh = NSA_HEADS

    def prep(i, carry):
        rows = pl.ds(pl.multiple_of(i * T, T), T)
        sel = sel_ref[rows, :].astype(F32)
        _split_heads(q_ref.at[rows, :], [sel[:, 0:LANE], sel[:, LANE:2 * LANE]], qs_scr, i * nh)
        return carry

    lax.fori_loop(0, nq, prep, 0)
    chains = [(LANE * (h // NSA_HPG), 64 * (h // NSA_HPG), 64) for h in range(nh)]
    acc_of = lambda i, h: acc_scr.at[i * (nh // 2) + h // 2, 64 * (h % 2):64 * (h % 2 + 1), :]
    _causal_flash(it_ref, jt_ref, nq, k_ref, vt_ref, qs_scr, s_scr, mx_scr, m_scr, l_scr, acc_scr, chains, acc_of)

    def finish(i, carry):
        rows = pl.ds(pl.multiple_of(i * T, T), T)
        for hp in range(nh // 2):
            scale = [gnt_ref[i, 3 * h + 1:3 * h + 2, :] / l_scr[i * nh + h] for h in (2 * hp, 2 * hp + 1)]
            inv = jnp.concatenate([jnp.broadcast_to(scale[0], (64, T)), jnp.broadcast_to(scale[1], (64, T))],
                                  axis=0)
            o_ref[rows, LANE * hp:LANE * (hp + 1)] = (acc_scr[i * (nh // 2) + hp] * inv).T.astype(BF16)
        return carry

    lax.fori_loop(0, nq, finish, 0)


def _slc(qrot, sel, ksa3, vst, gnt3, nb, seq):
    n = qrot.shape[0]
    nq = seq // T
    it, jt = _tile_schedule(nq)
    grid_spec = pltpu.PrefetchScalarGridSpec(
        num_scalar_prefetch=2,
        grid=(nb,),
        in_specs=[pl.BlockSpec((seq, 512), lambda b, it, jt: (b, 0)),
                  pl.BlockSpec((seq, 256), lambda b, it, jt: (b, 0)),
                  pl.BlockSpec((nq, T, 256), lambda b, it, jt: (b, 0, 0)),
                  pl.BlockSpec((nq, LANE, T), lambda b, it, jt: (b, 0, 0)),
                  pl.BlockSpec((nq, GATE_ROWS, T), lambda b, it, jt: (b, 0, 0))],
        out_specs=pl.BlockSpec((seq, 512), lambda b, it, jt: (b, 0)),
        scratch_shapes=_flash_scratch(nq, NSA_HEADS, NSA_Q))
    return pl.pallas_call(
        _slc_kernel,
        out_shape=jax.ShapeDtypeStruct((n, 512), BF16),
        grid_spec=grid_spec,
        compiler_params=_params(1, VMEM_LIMIT),
        name="slc",
    )(it, jt, qrot, sel, ksa3, vst, gnt3)


def _win_kernel(q_ref, k_ref, vt_ref, gnt_ref, o_ref, qs_scr, s_scr, mx_scr, acc_scr):
    i = pl.program_id(1)
    _split_heads(q_ref, [0.0, 0.0], qs_scr)
    krow = lax.broadcasted_iota(jnp.int32, (T, T), 0)
    qcol = lax.broadcasted_iota(jnp.int32, (T, T), 1)
    span = WINDOW // T
    masks = {0: krow <= qcol, span: (krow > qcol) & (i >= span)}
    ones = jnp.ones((16, T), BF16)

    def scores(hp, buf):
        for par in range(2):
            h = 2 * hp + par
            g = h // NSA_HPG
            mx = None
            for d in range(span + 1):
                j = i - d
                s = _nt(k_ref[jnp.maximum(j, 0), :, LANE * g:LANE * (g + 1)], qs_scr[h])
                if d in masks:
                    s = jnp.where(masks[d], s, NEG)
                else:
                    s = s + jnp.where(j >= 0, 0.0, NEG)
                s_scr[buf, par, d] = s
                md = jnp.max(s, axis=0, keepdims=True)
                mx = md if mx is None else jnp.maximum(mx, md)
            mx_scr[buf, par] = mx

    def softmax_pv(hp, buf):
        for par in range(2):
            g = (2 * hp + par) // NSA_HPG
            m = mx_scr[buf, par]
            acc = jnp.zeros((64 + 16, T), F32)
            for d in range(span + 1):
                parts = [jnp.exp2(s_scr[buf, par, d, r:r + SM_ROWS, :] - m).astype(BF16)
                         for r in range(0, T, SM_ROWS)]
                vt = jnp.concatenate([vt_ref[jnp.maximum(i - d, 0), 64 * g:64 * (g + 1), :], ones], axis=0)
                acc = acc + _nn(vt, jnp.concatenate(parts, axis=0))
            gate = gnt_ref[3 * (2 * hp + par) + 2:3 * (2 * hp + par) + 3, :]
            acc_scr[hp, 64 * par:64 * (par + 1), :] = acc[0:64] * (gate / acc[64:65])

    npairs = NSA_HEADS // 2
    scores(0, 0)
    for hp in range(npairs):
        if hp + 1 < npairs:
            scores(hp + 1, (hp + 1) % 2)
        softmax_pv(hp, hp % 2)
    for hp in range(npairs):
        o_ref[:, LANE * hp:LANE * (hp + 1)] = acc_scr[hp].T.astype(BF16)


def _win(qrot, kwa3, vwt, gnt, nb, seq):
    n = qrot.shape[0]
    nq = seq // T
    return pl.pallas_call(
        _win_kernel,
        out_shape=jax.ShapeDtypeStruct((n, 512), BF16),
        grid=(nb, nq),
        in_specs=[pl.BlockSpec((T, 512), lambda b, i: (b * nq + i, 0)),
                  pl.BlockSpec((nq, T, 256), lambda b, i: (b, 0, 0)),
                  pl.BlockSpec((nq, LANE, T), lambda b, i: (b, 0, 0)),
                  pl.BlockSpec((None, GATE_ROWS, T), lambda b, i: (b * nq + i, 0, 0))],
        out_specs=pl.BlockSpec((T, 512), lambda b, i: (b * nq + i, 0)),
        scratch_shapes=[pltpu.VMEM((NSA_HEADS, T, LANE), BF16),
                        pltpu.VMEM((2, 2, WINDOW // T + 1, T, T), F32),
                        pltpu.VMEM((2, 2, 1, T), F32),
                        pltpu.VMEM((NSA_HEADS // 2, LANE, T), F32)],
        compiler_params=_params(2, VMEM_LIMIT),
        name="win",
    )(qrot, kwa3, vwt, gnt)


def _diff_kernel(it_ref, jt_ref, q_ref, k_ref, vt_ref, lam_ref, g_ref, o_ref,
                 qs_scr, s_scr, mx_scr, m_scr, l_scr, acc_scr):
    nq = q_ref.shape[0] // T
    nc = 2 * DIFF_HEADS
    lane = lax.broadcasted_iota(jnp.int32, (T, LANE), 1)

    def prep(i, carry):
        rows = pl.ds(pl.multiple_of(i * T, T), T)
        for h in range(DIFF_HEADS):
            q = q_ref[rows, LANE * h:LANE * (h + 1)].astype(F32)
            qs_scr[i * nc + 2 * h] = jnp.where(lane < 64, q, 0.0).astype(BF16)
            qs_scr[i * nc + 2 * h + 1] = jnp.where(lane < 64, 0.0, q).astype(BF16)
        return carry

    lax.fori_loop(0, nq, prep, 0)
    chains = [(LANE * (c // 2), LANE * (c // 2), LANE) for c in range(nc)]
    _causal_flash(it_ref, jt_ref, nq, k_ref, vt_ref, qs_scr, s_scr, mx_scr, m_scr, l_scr, acc_scr, chains,
                  lambda i, c: acc_scr.at[i * nc + c])

    lp = lam_ref[...]
    lam = (jnp.exp(jnp.sum(lp[0:1] * lp[1:2], axis=1, keepdims=True))
           - jnp.exp(jnp.sum(lp[2:3] * lp[3:4], axis=1, keepdims=True)) + LAMBDA_INIT)

    def finish(i, carry):
        rows = pl.ds(pl.multiple_of(i * T, T), T)
        for h in range(DIFF_HEADS):
            c = i * nc + 2 * h
            o = acc_scr[c] * (1.0 / l_scr[c]) - lam * (acc_scr[c + 1] * (1.0 / l_scr[c + 1]))
            o = o * lax.rsqrt(jnp.mean(o * o, axis=0, keepdims=True) + RMS_EPS)
            o_ref[rows, LANE * h:LANE * (h + 1)] = ((o.T * g_ref[...]) * (1.0 - LAMBDA_INIT)).astype(BF16)
        return carry

    lax.fori_loop(0, nq, finish, 0)


def _diff(qdf, kdf3, vdft, lam, norm_g, nb, seq):
    n = qdf.shape[0]
    nq = seq // T
    it, jt = _tile_schedule(nq)
    grid_spec = pltpu.PrefetchScalarGridSpec(
        num_scalar_prefetch=2,
        grid=(nb,),
        in_specs=[pl.BlockSpec((seq, 512), lambda b, it, jt: (b, 0)),
                  pl.BlockSpec((nq, T, 512), lambda b, it, jt: (b, 0, 0)),
                  pl.BlockSpec((nq, 512, T), lambda b, it, jt: (b, 0, 0)),
                  pl.BlockSpec((4, HEAD_DIM), lambda b, it, jt: (0, 0)),
                  pl.BlockSpec((1, LANE), lambda b, it, jt: (0, 0))],
        out_specs=pl.BlockSpec((seq, 512), lambda b, it, jt: (b, 0)),
        scratch_shapes=_flash_scratch(nq, 2 * DIFF_HEADS, 2 * DIFF_HEADS * DIFF_V_DIM))
    return pl.pallas_call(
        _diff_kernel,
        out_shape=jax.ShapeDtypeStruct((n, 512), BF16),
        grid_spec=grid_spec,
        compiler_params=_params(1, VMEM_LIMIT),
        name="diff",
    )(it, jt, qdf, kdf3, vdft, lam, norm_g)


def _layer_norm(z, g, b):
    mu = jnp.mean(z, axis=-1, keepdims=True)
    zc = z - mu
    var = jnp.mean(zc * zc, axis=-1, keepdims=True)
    return zc * lax.rsqrt(var + LN_EPS) * g + b


def _mlp_kernel(x_ref, ocmp_ref, oslc_ref, owin_ref, yb_ref, gm_ref, wa_ref, wb_ref, wo_ref, g1_ref, b1_ref,
                wgu_ref, wd_ref, g2_ref, b2_ref, o_ref, h_scr):
    s = pl.program_id(0)

    @pl.when(s == 0)
    def _():
        h_scr[1] = jnp.zeros(h_scr.shape[1:], F32)

    h = h_scr[(s + 1) % 2]
    hb = h.astype(BF16)
    gate = _nn(hb, wgu_ref[:, 0:FFN_HIDDEN])
    up = _nn(hb, wgu_ref[:, FFN_HIDDEN:2 * FFN_HIDDEN])
    act = (gate * jax.nn.sigmoid(gate) * up).astype(BF16)
    y = _nn(act, wd_ref[...])
    o_ref[...] = _layer_norm(DEEPNORM_ALPHA * h + y, g2_ref[...], b2_ref[...])

    ya = ocmp_ref[...].astype(F32) + oslc_ref[...].astype(F32) + owin_ref[...].astype(F32)
    ta = _nn(ya.astype(BF16), wa_ref[...])
    tb = _nn(yb_ref[...], wb_ref[...])
    merged = gm_ref[:, 0:D_MODEL].astype(F32) * ta + gm_ref[:, D_MODEL:2 * D_MODEL].astype(F32) * tb
    mix = _nn(merged.astype(BF16), wo_ref[...])
    h_scr[s % 2] = _layer_norm(DEEPNORM_ALPHA * x_ref[...] + mix, g1_ref[...], b1_ref[...])


def _mlp(x2, ocmp, oslc, owin, yb, gm, wa, wb, wo, g1, b1, wgu, wd, g2, b2):
    n = x2.shape[0]
    nt = n // TM
    row = lambda w: pl.BlockSpec((TM, w), lambda s: (jnp.minimum(s, nt - 1), 0))
    full = lambda a: pl.BlockSpec(a.shape, lambda s: (0,) * a.ndim, pipeline_mode=pl.Buffered(1))
    return pl.pallas_call(
        _mlp_kernel,
        out_shape=jax.ShapeDtypeStruct((n, D_MODEL), F32),
        grid=(nt + 1,),
        in_specs=[row(D_MODEL), row(512), row(512), row(512), row(512), row(2048),
                  full(wa), full(wb), full(wo), full(g1), full(b1), full(wgu), full(wd), full(g2), full(b2)],
        out_specs=pl.BlockSpec((TM, D_MODEL), lambda s: (jnp.maximum(s - 1, 0), 0)),
        scratch_shapes=[pltpu.VMEM((2, TM, D_MODEL), F32)],
        compiler_params=_params(1, VMEM_LIMIT),
        name="mlp",
    )(x2, ocmp, oslc, owin, yb, gm, wa, wb, wo, g1, b1, wgu, wd, g2, b2)


def _rope_tables(seq):
    half = ROPE_DIM // 2
    inv_freq = ROPE_THETA ** (-jnp.arange(half, dtype=F32) * 2.0 / ROPE_DIM)
    ang = jnp.arange(seq, dtype=F32)[:, None] * inv_freq[None, :]
    cos, sin = jnp.cos(ang), jnp.sin(ang)
    ones = jnp.ones((seq, HEAD_DIM - ROPE_DIM), F32)
    zeros8 = jnp.zeros((seq, half), F32)
    zeros48 = jnp.zeros((seq, HEAD_DIM - ROPE_DIM), F32)
    c64 = jnp.concatenate([cos, cos, ones], axis=1)
    sa64 = jnp.concatenate([-sin, zeros8, zeros48], axis=1)
    sb64 = jnp.concatenate([zeros8, sin, zeros48], axis=1)
    rep = lambda a: jnp.concatenate([a, a], axis=1)
    pos_blk = jnp.arange(seq, dtype=jnp.int32)[:, None] // SLC_LEN
    lane = jnp.arange(LANE, dtype=jnp.int32)[None, :]
    onehot = ((lane >= 64) & (lane < 96) & (lane - 64 == pos_blk)).astype(F32)
    return rep(c64), rep(sa64), rep(sb64), onehot


def _prep_compress(cmp_pe, cmp_w1, cmp_b1, cmp_w2):
    half = CMP_LEN // 2
    pet, peb, wt, wb, b1 = [], [], [], [], []
    for kv in range(2):
        pe = cmp_pe[kv]
        tile2 = lambda a: jnp.concatenate([a, a], axis=1).reshape(1, half * 2 * HEAD_DIM)
        pet.append(tile2(pe[:half]))
        peb.append(tile2(pe[half:]))
        w1 = cmp_w1[kv].reshape(CMP_LEN, HEAD_DIM, CMP_HIDDEN)
        z = jnp.zeros((half, HEAD_DIM, CMP_HIDDEN), w1.dtype)

        def spread(wh):
            g0 = jnp.concatenate([wh, z], axis=2)
            g1 = jnp.concatenate([z, wh], axis=2)
            return jnp.stack([g0, g1], axis=1).reshape(half * 2 * HEAD_DIM, 2 * CMP_HIDDEN)

        wt.append(spread(w1[:half]))
        wb.append(spread(w1[half:]))
        b1.append(jnp.concatenate([cmp_b1[kv], cmp_b1[kv]])[None, :])
    w2k, w2v = cmp_w2[0], cmp_w2[1]
    zk = jnp.zeros_like(w2k)
    w2k_g = jnp.stack([
        jnp.concatenate([jnp.concatenate([w2k, zk], axis=1), jnp.zeros((CMP_HIDDEN, LANE), w2k.dtype)], axis=0),
        jnp.concatenate([jnp.zeros((CMP_HIDDEN, LANE), w2k.dtype), jnp.concatenate([w2k, zk], axis=1)], axis=0),
    ])
    zv = jnp.zeros_like(w2v)
    w2v_bd = jnp.concatenate([jnp.concatenate([w2v, zv], axis=1),
                              jnp.concatenate([zv, w2v], axis=1)], axis=0)
    st = lambda xs: jnp.stack(xs)
    return (st(pet), st(peb), st(wt).astype(BF16), st(wb).astype(BF16), st(b1),
            w2k_g.astype(BF16), w2v_bd.astype(BF16))


def kernel(x, w_in, cmp_pe, cmp_w1, cmp_b1, cmp_w2, diff_lambda, diff_norm_g, w_branch_a, w_branch_b,
           w_o, ln1_g, ln1_b, w_gate_up, w_down, ln2_g, ln2_b):
    nb, seq, d = x.shape
    assert d == D_MODEL and seq % T == 0 and seq // SLC_LEN == 32 and seq // CMP_STRIDE == 128
    n = nb * seq
    x2 = x.reshape(n, d)
    cos_t, sa_t, sb_t, oh_t = _rope_tables(seq)
    (qraw, qrot, kcs, vcs, ksa, vst, kwa, vwt, qdf, kdf, vdft, gm, gnt) = _inproj(
        x2, w_in.reshape(w_in.shape[1:]).T, cos_t, sa_t, sb_t, oh_t, seq)

    pet, peb, wt, wb, b1, w2k, w2v = _prep_compress(cmp_pe[0], cmp_w1[0], cmp_b1[0], cmp_w2[0])
    kca, vct = _compress(kcs, vcs, pet, peb, wt, wb, b1, w2k, w2v, nb, seq)

    ocmp, sel = _cmpsel(qraw, kca, vct, gnt, nb, seq)
    nt = n // T
    oslc = _slc(qrot, sel, ksa.reshape(nt, T, 256), vst, gnt, nb, seq)
    owin = _win(qrot, kwa.reshape(nt, T, 256), vwt, gnt, nb, seq)
    yb = _diff(qdf, kdf.reshape(nt, T, 512), vdft, diff_lambda[0], diff_norm_g[0][None, :], nb, seq)

    out = _mlp(x2, ocmp, oslc, owin, yb, gm,
               w_branch_a[0].astype(BF16), w_branch_b[0].astype(BF16), w_o[0].astype(BF16),
               ln1_g[0][None, :], ln1_b[0][None, :],
               w_gate_up[0].astype(BF16), w_down[0].astype(BF16), ln2_g[0][None, :], ln2_b[0][None, :])
    return out.reshape(nb, seq, d)
```

```python
import math

import jax
import jax.numpy as jnp
from jax import lax
from jax.experimental import pallas as pl
from jax.experimental.pallas import tpu as pltpu

D_MODEL = 1024
HEAD_DIM = 64
ROPE_DIM = HEAD_DIM // 4
ROPE_THETA = 500000.0
NSA_HEADS = 8
NSA_GROUPS = 2
NSA_HPG = NSA_HEADS // NSA_GROUPS
CMP_LEN = 32
CMP_STRIDE = 16
CMP_HIDDEN = 256
SLC_LEN = 64
SLC_TOPK = 8
WINDOW = 512
FORCE_BONUS = 1.0e4
DIFF_HEADS = 4
DIFF_V_DIM = 2 * HEAD_DIM
FFN_HIDDEN = ((8 * D_MODEL // 3 + 255) // 256) * 256
DEPTH = 1
DEEPNORM_ALPHA = (2 * DEPTH) ** 0.25
NEG = -1.0e30
LN_EPS = 1e-5
RMS_EPS = 1e-5
LAMBDA_INIT = 0.8 - 0.6 * math.exp(-0.3 * 0)
QK_SCALE = HEAD_DIM ** -0.5
QK_SCALE_LOG2 = QK_SCALE * math.log2(math.e)

NSA_Q = NSA_HEADS * HEAD_DIM
NSA_KV = NSA_GROUPS * HEAD_DIM
DIFF_QK = DIFF_HEADS * 2 * HEAD_DIM
DIFF_V = DIFF_HEADS * DIFF_V_DIM
IN_WIDTHS = (NSA_Q, NSA_KV, NSA_KV, NSA_KV, NSA_KV, NSA_KV, NSA_KV, 3 * NSA_HEADS,
             DIFF_QK, DIFF_QK, DIFF_V, 2 * D_MODEL)

LANE = 128
T = 256
TM = 512
VMEM_LIMIT = 56 * 1024 * 1024
GATE_ROWS = 32

BF16 = jnp.bfloat16
F32 = jnp.float32

_C_Q, _C_KC, _C_VC, _C_KS, _C_VS, _C_KW, _C_VW = 0, 512, 640, 768, 896, 1024, 1152
_C_QD, _C_KD, _C_VD, _C_GM, _C_GN, _C_END = 1280, 1792, 2304, 2816, 4864, 4992


def _nt(a, b):
    return lax.dot_general(a, b, (((1,), (1,)), ((), ())), preferred_element_type=F32)


def _nn(a, b):
    return jnp.dot(a, b, preferred_element_type=F32)


def _params(n_axes, vmem=None):
    return pltpu.CompilerParams(dimension_semantics=("arbitrary",) * n_axes,
                                vmem_limit_bytes=vmem)


def _inproj_kernel(x_ref, w_ref, cos_ref, sa_ref, sb_ref, oh_ref,
                   qraw_ref, qrot_ref, kc_ref, vc_ref, ksa_ref, vst_ref, kwa_ref, vwt_ref,
                   qdf_ref, kdf_ref, vdft_ref, gm_ref, gn_ref, w_scr):
    @pl.when(pl.program_id(0) == 0)
    def _():
        gn0 = _C_QD
        gn1 = gn0 + 3 * NSA_HEADS
        rows = 256

        def regroup(r, carry):
            dst = pl.ds(pl.multiple_of(r * rows, rows), rows)
            src = pl.ds(pl.multiple_of(jnp.where(r * rows < gn0, r * rows, r * rows + gn1 - gn0), 8), rows)
            w_scr[dst, :] = w_ref[src, :].astype(BF16)
            return carry

        lax.fori_loop(0, _C_GN // rows, regroup, 0)
        gates = jnp.concatenate([w_ref[gn0:gn1, :], jnp.zeros((LANE - (gn1 - gn0), D_MODEL), F32)], axis=0)
        w_scr[_C_GN:_C_END, :] = gates.astype(BF16)

    xb = x_ref[...].astype(BF16)
    cos = cos_ref[...]
    sa = sa_ref[...]
    sb = sb_ref[...]

    def mm(c0, n):
        return _nt(xb, w_scr[c0:c0 + n, :])

    def rope(t):
        return t * cos + pltpu.roll(t, LANE - 8, 1) * sa + pltpu.roll(t, 8, 1) * sb

    t = mm(_C_Q, 512)
    for j in range(4):
        tj = t[:, LANE * j:LANE * (j + 1)]
        qraw_ref[:, LANE * j:LANE * (j + 1)] = (tj * QK_SCALE_LOG2).astype(BF16)
        qrot_ref[:, LANE * j:LANE * (j + 1)] = (rope(tj) * QK_SCALE_LOG2).astype(BF16)

    t = mm(_C_KC, 256)
    kc_ref[...] = t[:, :LANE]
    vc_ref[...] = t[:, LANE:]

    lane = lax.broadcasted_iota(jnp.int32, (TM, LANE), 1)
    tiles = [slice(T * u, T * (u + 1)) for u in range(TM // T)]

    def per_group(k, extra, out_ref):
        out_ref[:, 0:LANE] = jnp.where(lane < 64, k, extra).astype(BF16)
        out_ref[:, LANE:2 * LANE] = jnp.where(lane < 64, pltpu.roll(k, 64, 1), extra).astype(BF16)

    t = mm(_C_KS, 256)
    per_group(rope(t[:, :LANE]), oh_ref[...], ksa_ref)
    for u, rs in enumerate(tiles):
        vst_ref[u] = t[rs, LANE:].T.astype(BF16)

    t = mm(_C_KW, 256)
    per_group(rope(t[:, :LANE]), 0.0, kwa_ref)
    for u, rs in enumerate(tiles):
        vwt_ref[u] = t[rs, LANE:].T.astype(BF16)

    t = mm(_C_QD, 512)
    for j in range(4):
        qdf_ref[:, LANE * j:LANE * (j + 1)] = (rope(t[:, LANE * j:LANE * (j + 1)]) * QK_SCALE_LOG2).astype(BF16)
    t = mm(_C_KD, 512)
    for j in range(4):
        kdf_ref[:, LANE * j:LANE * (j + 1)] = rope(t[:, LANE * j:LANE * (j + 1)]).astype(BF16)
    t = mm(_C_VD, 512)
    for u, rs in enumerate(tiles):
        for j in range(4):
            vdft_ref[u, LANE * j:LANE * (j + 1), :] = t[rs, LANE * j:LANE * (j + 1)].T.astype(BF16)

    for j in range(4):
        t = mm(_C_GM + 512 * j, 512)
        gm_ref[:, 512 * j:512 * (j + 1)] = jax.nn.sigmoid(t).astype(BF16)

    t = jax.nn.sigmoid(mm(_C_GN, 128))
    for u, rs in enumerate(tiles):
        gn_ref[u] = t[rs].T[0:GATE_ROWS]


def _inproj(x2, w, cos_t, sa_t, sb_t, oh_t, seq):
    n = x2.shape[0]
    nt = n // T
    spt = seq // TM
    assert seq % TM == 0 and TM % T == 0
    assert w.shape == (sum(IN_WIDTHS), D_MODEL) and sum(IN_WIDTHS[:7]) == _C_QD
    assert sum(IN_WIDTHS) - 3 * NSA_HEADS == _C_GN and _C_QD % 256 == 0 and _C_GN % 256 == 0
    row = lambda w: pl.BlockSpec((TM, w), lambda i: (i, 0))
    whole = lambda a: pl.BlockSpec(a.shape, lambda i: (0, 0), pipeline_mode=pl.Buffered(1))
    tab = pl.BlockSpec((TM, LANE), lambda i: (i % spt, 0))
    tile_t = lambda r: pl.BlockSpec((TM // T, r, T), lambda i: (i, 0, 0))
    out_shape = (
        jax.ShapeDtypeStruct((n, 512), BF16),
        jax.ShapeDtypeStruct((n, 512), BF16),
        jax.ShapeDtypeStruct((n, LANE), F32),
        jax.ShapeDtypeStruct((n, LANE), F32),
        jax.ShapeDtypeStruct((n, 256), BF16),
        jax.ShapeDtypeStruct((nt, LANE, T), BF16),
        jax.ShapeDtypeStruct((n, 256), BF16),
        jax.ShapeDtypeStruct((nt, LANE, T), BF16),
        jax.ShapeDtypeStruct((n, 512), BF16),
        jax.ShapeDtypeStruct((n, 512), BF16),
        jax.ShapeDtypeStruct((nt, 512, T), BF16),
        jax.ShapeDtypeStruct((n, 2048), BF16),
        jax.ShapeDtypeStruct((nt, GATE_ROWS, T), F32),
    )
    out_specs = (row(512), row(512), row(LANE), row(LANE), row(256), tile_t(LANE), row(256),
                 tile_t(LANE), row(512), row(512), tile_t(512), row(2048), tile_t(GATE_ROWS))
    return pl.pallas_call(
        _inproj_kernel,
        out_shape=out_shape,
        grid=(n // TM,),
        in_specs=[row(D_MODEL), whole(w), tab, tab, tab, tab],
        out_specs=out_specs,
        scratch_shapes=[pltpu.VMEM((_C_END, D_MODEL), BF16)],
        compiler_params=_params(1, VMEM_LIMIT),
        name="inproj",
    )(x2, w, cos_t, sa_t, sb_t, oh_t)


def _gelu_tanh(x):
    c = math.sqrt(2.0 / math.pi)
    return x * (0.5 * (1.0 + jnp.tanh(c * (x + 0.044715 * (x * x * x)))))


def _compress_kernel(kf_ref, vf_ref, pet_ref, peb_ref, wt_ref, wb_ref, b1_ref, w2k_ref, w2v_ref,
                     kca_ref, vct_ref):
    nchunk = kf_ref.shape[0] // CMP_STRIDE

    def hidden(x_ref, kv):
        a = jnp.zeros((nchunk, 2 * CMP_HIDDEN), F32)
        b = jnp.zeros((nchunk, 2 * CMP_HIDDEN), F32)
        for p in range(0, CMP_STRIDE, 2):
            x = jnp.concatenate([x_ref[pl.ds(p, nchunk, stride=CMP_STRIDE), :],
                                 x_ref[pl.ds(p + 1, nchunk, stride=CMP_STRIDE), :]], axis=1)
            c0, c1 = LANE * p, LANE * (p + 2)
            a = a + _nn((x + pet_ref[kv, :, c0:c1]).astype(BF16), wt_ref[kv, c0:c1, :])
            b = b + _nn((x + peb_ref[kv, :, c0:c1]).astype(BF16), wb_ref[kv, c0:c1, :])
        h = a + pltpu.roll(b, nchunk - 1, 0) + b1_ref[kv]
        return _gelu_tanh(h).astype(BF16)

    hk = hidden(kf_ref, 0)
    for g in range(2):
        kca_ref[g] = _nn(hk, w2k_ref[g]).astype(BF16)
    hv = hidden(vf_ref, 1)
    vct_ref[...] = _nn(hv, w2v_ref[...]).T.astype(BF16)


def _compress(kf, vf, pet, peb, wt, wb, b1, w2k, w2v, nb, seq):
    full = lambda a: pl.BlockSpec(a.shape, lambda b: (0,) * a.ndim)
    return pl.pallas_call(
        _compress_kernel,
        out_shape=(jax.ShapeDtypeStruct((nb, 2, 128, LANE), BF16),
                   jax.ShapeDtypeStruct((nb, LANE, 128), BF16)),
        grid=(nb,),
        in_specs=[pl.BlockSpec((seq, LANE), lambda b: (b, 0)),
                  pl.BlockSpec((seq, LANE), lambda b: (b, 0)),
                  full(pet), full(peb), full(wt), full(wb), full(b1), full(w2k), full(w2v)],
        out_specs=(pl.BlockSpec((None, 2, 128, LANE), lambda b: (b, 0, 0, 0)),
                   pl.BlockSpec((None, LANE, 128), lambda b: (b, 0, 0))),
        compiler_params=_params(1, VMEM_LIMIT),
        name="compress",
    )(kf, vf, pet, peb, wt, wb, b1, w2k, w2v)


def _cmpsel_kernel(q_ref, kca_ref, vct_ref, gnt_ref, ocmp_ref, sel_ref, s_scr):
    nq = q_ref.shape[0] // T
    lane = lax.broadcasted_iota(jnp.int32, (T, LANE), 1)
    crow = lax.broadcasted_iota(jnp.int32, (128, T), 0)
    tcol0 = lax.broadcasted_iota(jnp.int32, (128, T), 1)
    tlane = lax.broadcasted_iota(jnp.int32, (1, T), 1)

    jj = lax.broadcasted_iota(jnp.int32, (32, 128), 0) * SLC_LEN
    cc = lax.broadcasted_iota(jnp.int32, (32, 128), 1) * CMP_STRIDE
    ov = jnp.maximum(jnp.minimum(cc + CMP_LEN, jj + SLC_LEN) - jnp.maximum(cc, jj), 0)
    ovt = (ov.astype(F32) * (1.0 / CMP_LEN)).astype(BF16)

    jrow = lax.broadcasted_iota(jnp.int32, (32, T), 0)
    jrow8 = lax.broadcasted_iota(jnp.int32, (8, T), 0)
    tblk0 = lax.broadcasted_iota(jnp.int32, (32, T), 1)

    def scores(i, buf):
        rows = pl.ds(pl.multiple_of(i * T, T), T)
        for hp in range(NSA_HEADS // 2):
            g = hp // (NSA_HPG // 2)
            qp = q_ref[rows, LANE * hp:LANE * (hp + 1)].astype(F32)
            qs = (jnp.where(lane < 64, qp, 0.0), jnp.where(lane < 64, pltpu.roll(qp, 64, 1), 0.0))
            for par in range(2):
                s_scr[buf, 2 * hp + par] = _nt(kca_ref[g], qs[par].astype(BF16))

    def attend_select(i, buf):
        t0 = i * T
        rows = pl.ds(pl.multiple_of(t0, T), T)
        cmask = (crow * CMP_STRIDE + (CMP_LEN - 1)) <= tcol0 + t0
        live = jnp.where(tlane + t0 >= (CMP_LEN - 1), 1.0, 0.0)
        tblk = (tblk0 + t0) // SLC_LEN
        valid = jrow <= tblk
        forced = (jrow == 0) | (jrow == tblk) | (jrow == tblk - 1)
        bonus = jnp.where(forced, FORCE_BONUS, 0.0)
        for g in range(NSA_GROUPS):
            vt = vct_ref[64 * g:64 * (g + 1), :]
            psum = jnp.zeros((128, T), F32)
            for hp in range(NSA_HPG // 2):
                outs = []
                for par in range(2):
                    h = NSA_HPG * g + 2 * hp + par
                    s = jnp.where(cmask, s_scr[buf, h], NEG)
                    m = jnp.max(s, axis=0, keepdims=True)
                    e = jnp.exp2(s - m)
                    p = e * (live / jnp.sum(e, axis=0, keepdims=True))
                    psum = psum + p
                    outs.append(_nn(vt, p.astype(BF16)) * gnt_ref[i, 3 * h:3 * h + 1, :])
                ocmp_ref[rows, LANE * (2 * g + hp):LANE * (2 * g + hp + 1)] = (
                    jnp.concatenate(outs, axis=0).T.astype(BF16))

            p_hi = psum.astype(BF16)
            p_lo = (psum - p_hi.astype(F32)).astype(BF16)
            pslc = _nn(ovt, p_hi) + _nn(ovt, p_lo)
            pri = jnp.where(valid, pslc + bonus, -1.0)
            rank = [jnp.zeros((8, T), F32) for _ in range(4)]
            for r in range(32):
                row = pri[r:r + 1, :]
                for a in range(4):
                    pa = pri[8 * a:8 * (a + 1), :]
                    if 8 * a > r:
                        ahead = jnp.where(row >= pa, 1.0, 0.0)
                    elif 8 * a + 7 < r:
                        ahead = jnp.where(row > pa, 1.0, 0.0)
                    else:
                        ahead = jnp.where(jrow8 + 8 * a > r, jnp.where(row >= pa, 1.0, 0.0),
                                          jnp.where(row > pa, 1.0, 0.0))
                    rank[a] = rank[a] + ahead
            rank = jnp.concatenate(rank, axis=0)
            selneg = jnp.where(rank < float(SLC_TOPK), 0.0, NEG)
            pad = jnp.concatenate([jnp.zeros((64, T), F32), selneg, jnp.zeros((32, T), F32)], axis=0)
            sel_ref[rows, LANE * g:LANE * (g + 1)] = pad.T.astype(BF16)

    scores(0, 0)

    def body(u, carry):
        scores(2 * u + 1, 1)
        attend_select(2 * u, 0)
        scores(jnp.minimum(2 * u + 2, nq - 1), 0)
        attend_select(2 * u + 1, 1)
        return carry

    lax.fori_loop(0, nq // 2, body, 0)


def _cmpsel(qraw, kca, vct, gnt, nb, seq):
    n = qraw.shape[0]
    nq = seq // T
    assert nq % 2 == 0
    return pl.pallas_call(
        _cmpsel_kernel,
        out_shape=(jax.ShapeDtypeStruct((n, 512), BF16), jax.ShapeDtypeStruct((n, 256), BF16)),
        grid=(nb,),
        in_specs=[pl.BlockSpec((seq, 512), lambda b: (b, 0)),
                  pl.BlockSpec((None, 2, 128, LANE), lambda b: (b, 0, 0, 0)),
                  pl.BlockSpec((None, LANE, 128), lambda b: (b, 0, 0)),
                  pl.BlockSpec((nq, GATE_ROWS, T), lambda b: (b, 0, 0))],
        out_specs=(pl.BlockSpec((seq, 512), lambda b: (b, 0)),
                   pl.BlockSpec((seq, 256), lambda b: (b, 0))),
        scratch_shapes=[pltpu.VMEM((2, NSA_HEADS, 128, T), F32)],
        compiler_params=_params(1, VMEM_LIMIT),
        name="cmpsel",
    )(qraw, kca, vct, gnt)


def _split_heads(q_ref, extras, qs_scr, base=0):
    lane = lax.broadcasted_iota(jnp.int32, (T, LANE), 1)
    for hp in range(NSA_HEADS // 2):
        extra = extras[hp // (NSA_HPG // 2)]
        qp = q_ref[:, LANE * hp:LANE * (hp + 1)].astype(F32)
        qs_scr[base + 2 * hp] = jnp.where(lane < 64, qp, extra).astype(BF16)
        qs_scr[base + 2 * hp + 1] = jnp.where(lane < 64, pltpu.roll(qp, 64, 1), extra).astype(BF16)


SM_ROWS = 32


RING = 4


def _tile_schedule(nq):
    below = [(i, j) for j in range(nq) for i in range(j + 1, nq)]
    assert len(below) == (RING - 1) * nq + RING
    pairs = []
    for i in range(nq):
        pairs += [(i, i)] + below[(RING - 1) * i:(RING - 1) * (i + 1)]
    pairs += below[(RING - 1) * nq:]
    return (jnp.asarray([p[0] for p in pairs], jnp.int32), jnp.asarray([p[1] for p in pairs], jnp.int32))


def _flash_scratch(nq, nc, acc_rows):
    return [pltpu.VMEM((nq * nc, T, LANE), BF16),
            pltpu.VMEM((RING, nc, T, T), F32),
            pltpu.VMEM((RING, nc, 1, T), F32),
            pltpu.VMEM((nq * nc, 1, T), F32),
            pltpu.VMEM((nq * nc, 1, T), F32),
            pltpu.VMEM((nq * acc_rows // LANE, LANE, T), F32)]


def _causal_flash(it_ref, jt_ref, nq, k_ref, vt_ref, qs_scr, s_scr, mx_scr, m_scr, l_scr, acc_scr, chains, acc_of):
    nc = len(chains)
    nsets = it_ref.shape[0]
    assert nsets % RING == 0 and nsets >= 2 * RING
    m_scr[...] = jnp.full(m_scr.shape, NEG, F32)
    l_scr[...] = jnp.zeros(l_scr.shape, F32)
    acc_scr[...] = jnp.zeros(acc_scr.shape, F32)
    krow = lax.broadcasted_iota(jnp.int32, (T, T), 0)
    qcol = lax.broadcasted_iota(jnp.int32, (T, T), 1)
    ones = jnp.ones((16, T), BF16)

    def scores(n, buf):
        i, j = it_ref[n], jt_ref[n]
        if buf == 0:
            mask = krow <= qcol + jnp.where(i == j, 0, T)
        for c, (kl, _, _) in enumerate(chains):
            s = _nt(k_ref[j, :, kl:kl + LANE], qs_scr[i * nc + c])
            if buf == 0:
                s = jnp.where(mask, s, NEG)
            s_scr[buf, c] = s
            mx_scr[buf, c] = jnp.max(s, axis=0, keepdims=True)

    def softmax_pv(n, buf):
        i, j = it_ref[n], jt_ref[n]
        for c, (_, vr, dv) in enumerate(chains):
            m_prev = m_scr[i * nc + c]
            m_new = jnp.maximum(m_prev, mx_scr[buf, c])
            alpha = jnp.exp2(m_prev - m_new)
            parts = [jnp.exp2(s_scr[buf, c, r:r + SM_ROWS, :] - m_new).astype(BF16)
                     for r in range(0, T, SM_ROWS)]
            vt = jnp.concatenate([vt_ref[j, vr:vr + dv, :], ones], axis=0)
            o = _nn(vt, jnp.concatenate(parts, axis=0))
            l_scr[i * nc + c] = alpha * l_scr[i * nc + c] + o[dv:dv + 1]
            m_scr[i * nc + c] = m_new
            acc = acc_of(i, c)
            acc[...] = acc[...] * alpha + o[0:dv]

    scores(0, 0)

    def body(u, carry):
        n = RING * u
        for r in range(RING):
            scores(n + r + 1, (r + 1) % RING)
            softmax_pv(n + r, r)
        return carry

    lax.fori_loop(0, nsets // RING - 1, body, 0)
    n = nsets - RING
    for r in range(RING):
        if r + 1 < RING:
            scores(n + r + 1, r + 1)
        softmax_pv(n + r, r)


def _slc_kernel(it_ref, jt_ref, q_ref, sel_ref, k_ref, vt_ref, gnt_ref, o_ref,
                qs_scr, s_scr, mx_scr, m_scr, l_scr, acc_scr):
    nq = q_ref.shape[0] // T
    nh = NSA_HEADS

    def prep(i, carry):
        rows = pl.ds(pl.multiple_of(i * T, T), T)
        sel = sel_ref[rows, :].astype(F32)
        _split_heads(q_ref.at[rows, :], [sel[:, 0:LANE], sel[:, LANE:2 * LANE]], qs_scr, i * nh)
        return carry

    lax.fori_loop(0, nq, prep, 0)
    chains = [(LANE * (h // NSA_HPG), 64 * (h // NSA_HPG), 64) for h in range(nh)]
    acc_of = lambda i, h: acc_scr.at[i * (nh // 2) + h // 2, 64 * (h % 2):64 * (h % 2 + 1), :]
    _causal_flash(it_ref, jt_ref, nq, k_ref, vt_ref, qs_scr, s_scr, mx_scr, m_scr, l_scr, acc_scr, chains, acc_of)

    def finish(i, carry):
        rows = pl.ds(pl.multiple_of(i * T, T), T)
        for hp in range(nh // 2):
            scale = [gnt_ref[i, 3 * h + 1:3 * h + 2, :] / l_scr[i * nh + h] for h in (2 * hp, 2 * hp + 1)]
            inv = jnp.concatenate([jnp.broadcast_to(scale[0], (64, T)), jnp.broadcast_to(scale[1], (64, T))],
                                  axis=0)
            o_ref[rows, LANE * hp:LANE * (hp + 1)] = (acc_scr[i * (nh // 2) + hp] * inv).T.astype(BF16)
        return carry

    lax.fori_loop(0, nq, finish, 0)


def _slc(qrot, sel, ksa3, vst, gnt3, nb, seq):
    n = qrot.shape[0]
    nq = seq // T
    it, jt = _tile_schedule(nq)
    grid_spec = pltpu.PrefetchScalarGridSpec(
        num_scalar_prefetch=2,
        grid=(nb,),
        in_specs=[pl.BlockSpec((seq, 512), lambda b, it, jt: (b, 0)),
                  pl.BlockSpec((seq, 256), lambda b, it, jt: (b, 0)),
                  pl.BlockSpec((nq, T, 256), lambda b, it, jt: (b, 0, 0)),
                  pl.BlockSpec((nq, LANE, T), lambda b, it, jt: (b, 0, 0)),
                  pl.BlockSpec((nq, GATE_ROWS, T), lambda b, it, jt: (b, 0, 0))],
        out_specs=pl.BlockSpec((seq, 512), lambda b, it, jt: (b, 0)),
        scratch_shapes=_flash_scratch(nq, NSA_HEADS, NSA_Q))
    return pl.pallas_call(
        _slc_kernel,
        out_shape=jax.ShapeDtypeStruct((n, 512), BF16),
        grid_spec=grid_spec,
        compiler_params=_params(1, VMEM_LIMIT),
        name="slc",
    )(it, jt, qrot, sel, ksa3, vst, gnt3)


def _win_kernel(q_ref, k_ref, vt_ref, gnt_ref, o_ref, qs_scr, s_scr, mx_scr, acc_scr):
    i = pl.program_id(1)
    _split_heads(q_ref, [0.0, 0.0], qs_scr)
    krow = lax.broadcasted_iota(jnp.int32, (T, T), 0)
    qcol = lax.broadcasted_iota(jnp.int32, (T, T), 1)
    span = WINDOW // T
    masks = {0: krow <= qcol, span: (krow > qcol) & (i >= span)}
    ones = jnp.ones((16, T), BF16)

    def scores(hp, buf):
        for par in range(2):
            h = 2 * hp + par
            g = h // NSA_HPG
            mx = None
            for d in range(span + 1):
                j = i - d
                s = _nt(k_ref[jnp.maximum(j, 0), :, LANE * g:LANE * (g + 1)], qs_scr[h])
                if d in masks:
                    s = jnp.where(masks[d], s, NEG)
                else:
                    s = s + jnp.where(j >= 0, 0.0, NEG)
                s_scr[buf, par, d] = s
                md = jnp.max(s, axis=0, keepdims=True)
                mx = md if mx is None else jnp.maximum(mx, md)
            mx_scr[buf, par] = mx

    def softmax_pv(hp, buf):
        for par in range(2):
            g = (2 * hp + par) // NSA_HPG
            m = mx_scr[buf, par]
            acc = jnp.zeros((64 + 16, T), F32)
            for d in range(span + 1):
                parts = [jnp.exp2(s_scr[buf, par, d, r:r + SM_ROWS, :] - m).astype(BF16)
                         for r in range(0, T, SM_ROWS)]
                vt = jnp.concatenate([vt_ref[jnp.maximum(i - d, 0), 64 * g:64 * (g + 1), :], ones], axis=0)
                acc = acc + _nn(vt, jnp.concatenate(parts, axis=0))
            gate = gnt_ref[3 * (2 * hp + par) + 2:3 * (2 * hp + par) + 3, :]
            acc_scr[hp, 64 * par:64 * (par + 1), :] = acc[0:64] * (gate / acc[64:65])

    npairs = NSA_HEADS // 2
    scores(0, 0)
    for hp in range(npairs):
        if hp + 1 < npairs:
            scores(hp + 1, (hp + 1) % 2)
        softmax_pv(hp, hp % 2)
    for hp in range(npairs):
        o_ref[:, LANE * hp:LANE * (hp + 1)] = acc_scr[hp].T.astype(BF16)


def _win(qrot, kwa3, vwt, gnt, nb, seq):
    n = qrot.shape[0]
    nq = seq // T
    return pl.pallas_call(
        _win_kernel,
        out_shape=jax.ShapeDtypeStruct((n, 512), BF16),
        grid=(nb, nq),
        in_specs=[pl.BlockSpec((T, 512), lambda b, i: (b * nq + i, 0)),
                  pl.BlockSpec((nq, T, 256), lambda b, i: (b, 0, 0)),
                  pl.BlockSpec((nq, LANE, T), lambda b, i: (b, 0, 0)),
                  pl.BlockSpec((None, GATE_ROWS, T), lambda b, i: (b * nq + i, 0, 0))],
        out_specs=pl.BlockSpec((T, 512), lambda b, i: (b * nq + i, 0)),
        scratch_shapes=[pltpu.VMEM((NSA_HEADS, T, LANE), BF16),
                        pltpu.VMEM((2, 2, WINDOW // T + 1, T, T), F32),
                        pltpu.VMEM((2, 2, 1, T), F32),
                        pltpu.VMEM((NSA_HEADS // 2, LANE, T), F32)],
        compiler_params=_params(2, VMEM_LIMIT),
        name="win",
    )(qrot, kwa3, vwt, gnt)


def _diff_kernel(it_ref, jt_ref, q_ref, k_ref, vt_ref, lam_ref, g_ref, o_ref,
                 qs_scr, s_scr, mx_scr, m_scr, l_scr, acc_scr):
    nq = q_ref.shape[0] // T
    nc = 2 * DIFF_HEADS
    lane = lax.broadcasted_iota(jnp.int32, (T, LANE), 1)

    def prep(i, carry):
        rows = pl.ds(pl.multiple_of(i * T, T), T)
        for h in range(DIFF_HEADS):
            q = q_ref[rows, LANE * h:LANE * (h + 1)].astype(F32)
            qs_scr[i * nc + 2 * h] = jnp.where(lane < 64, q, 0.0).astype(BF16)
            qs_scr[i * nc + 2 * h + 1] = jnp.where(lane < 64, 0.0, q).astype(BF16)
        return carry

    lax.fori_loop(0, nq, prep, 0)
    chains = [(LANE * (c // 2), LANE * (c // 2), LANE) for c in range(nc)]
    _causal_flash(it_ref, jt_ref, nq, k_ref, vt_ref, qs_scr, s_scr, mx_scr, m_scr, l_scr, acc_scr, chains,
                  lambda i, c: acc_scr.at[i * nc + c])

    lp = lam_ref[...]
    lam = (jnp.exp(jnp.sum(lp[0:1] * lp[1:2], axis=1, keepdims=True))
           - jnp.exp(jnp.sum(lp[2:3] * lp[3:4], axis=1, keepdims=True)) + LAMBDA_INIT)

    def finish(i, carry):
        rows = pl.ds(pl.multiple_of(i * T, T), T)
        for h in range(DIFF_HEADS):
            c = i * nc + 2 * h
            o = acc_scr[c] * (1.0 / l_scr[c]) - lam * (acc_scr[c + 1] * (1.0 / l_scr[c + 1]))
            o = o * lax.rsqrt(jnp.mean(o * o, axis=0, keepdims=True) + RMS_EPS)
            o_ref[rows, LANE * h:LANE * (h + 1)] = ((o.T * g_ref[...]) * (1.0 - LAMBDA_INIT)).astype(BF16)
        return carry

    lax.fori_loop(0, nq, finish, 0)


def _diff(qdf, kdf3, vdft, lam, norm_g, nb, seq):
    n = qdf.shape[0]
    nq = seq // T
    it, jt = _tile_schedule(nq)
    grid_spec = pltpu.PrefetchScalarGridSpec(
        num_scalar_prefetch=2,
        grid=(nb,),
        in_specs=[pl.BlockSpec((seq, 512), lambda b, it, jt: (b, 0)),
                  pl.BlockSpec((nq, T, 512), lambda b, it, jt: (b, 0, 0)),
                  pl.BlockSpec((nq, 512, T), lambda b, it, jt: (b, 0, 0)),
                  pl.BlockSpec((4, HEAD_DIM), lambda b, it, jt: (0, 0)),
                  pl.BlockSpec((1, LANE), lambda b, it, jt: (0, 0))],
        out_specs=pl.BlockSpec((seq, 512), lambda b, it, jt: (b, 0)),
        scratch_shapes=_flash_scratch(nq, 2 * DIFF_HEADS, 2 * DIFF_HEADS * DIFF_V_DIM))
    return pl.pallas_call(
        _diff_kernel,
        out_shape=jax.ShapeDtypeStruct((n, 512), BF16),
        grid_spec=grid_spec,
        compiler_params=_params(1, VMEM_LIMIT),
        name="diff",
    )(it, jt, qdf, kdf3, vdft, lam, norm_g)


def _layer_norm(z, g, b):
    mu = jnp.mean(z, axis=-1, keepdims=True)
    zc = z - mu
    var = jnp.mean(zc * zc, axis=-1, keepdims=True)
    return zc * lax.rsqrt(var + LN_EPS) * g + b


def _mlp_kernel(x_ref, ocmp_ref, oslc_ref, owin_ref, yb_ref, gm_ref, wa_ref, wb_ref, wo_ref, g1_ref, b1_ref,
                wgu_ref, wd_ref, g2_ref, b2_ref, o_ref, h_scr):
    s = pl.program_id(0)

    @pl.when(s == 0)
    def _():
        h_scr[1] = jnp.zeros(h_scr.shape[1:], F32)

    for u in range(TM // T):
        rs = slice(T * u, T * (u + 1))
        h = h_scr[(s + 1) % 2, rs, :]
        hb = h.astype(BF16)
        gate = _nn(hb, wgu_ref[:, 0:FFN_HIDDEN])
        up = _nn(hb, wgu_ref[:, FFN_HIDDEN:2 * FFN_HIDDEN])
        act = (gate * jax.nn.sigmoid(gate) * up).astype(BF16)
        y = _nn(act, wd_ref[...])
        o_ref[rs, :] = _layer_norm(DEEPNORM_ALPHA * h + y, g2_ref[...], b2_ref[...])

        ya = ocmp_ref[rs, :].astype(F32) + oslc_ref[rs, :].astype(F32) + owin_ref[rs, :].astype(F32)
        ta = _nn(ya.astype(BF16), wa_ref[...])
        tb = _nn(yb_ref[rs, :], wb_ref[...])
        merged = gm_ref[rs, 0:D_MODEL].astype(F32) * ta + gm_ref[rs, D_MODEL:2 * D_MODEL].astype(F32) * tb
        mix = _nn(merged.astype(BF16), wo_ref[...])
        h_scr[s % 2, rs, :] = _layer_norm(DEEPNORM_ALPHA * x_ref[rs, :] + mix, g1_ref[...], b1_ref[...])


def _mlp(x2, ocmp, oslc, owin, yb, gm, wa, wb, wo, g1, b1, wgu, wd, g2, b2):
    n = x2.shape[0]
    nt = n // TM
    row = lambda w: pl.BlockSpec((TM, w), lambda s: (jnp.minimum(s, nt - 1), 0))
    full = lambda a: pl.BlockSpec(a.shape, lambda s: (0,) * a.ndim, pipeline_mode=pl.Buffered(1))
    return pl.pallas_call(
        _mlp_kernel,
        out_shape=jax.ShapeDtypeStruct((n, D_MODEL), F32),
        grid=(nt + 1,),
        in_specs=[row(D_MODEL), row(512), row(512), row(512), row(512), row(2048),
                  full(wa), full(wb), full(wo), full(g1), full(b1), full(wgu), full(wd), full(g2), full(b2)],
        out_specs=pl.BlockSpec((TM, D_MODEL), lambda s: (jnp.maximum(s - 1, 0), 0)),
        scratch_shapes=[pltpu.VMEM((2, TM, D_MODEL), F32)],
        compiler_params=_params(1, VMEM_LIMIT),
        name="mlp",
    )(x2, ocmp, oslc, owin, yb, gm, wa, wb, wo, g1, b1, wgu, wd, g2, b2)


def _rope_tables(seq):
    half = ROPE_DIM // 2
    inv_freq = ROPE_THETA ** (-jnp.arange(half, dtype=F32) * 2.0 / ROPE_DIM)
    ang = jnp.arange(seq, dtype=F32)[:, None] * inv_freq[None, :]
    cos, sin = jnp.cos(ang), jnp.sin(ang)
    ones = jnp.ones((seq, HEAD_DIM - ROPE_DIM), F32)
    zeros8 = jnp.zeros((seq, half), F32)
    zeros48 = jnp.zeros((seq, HEAD_DIM - ROPE_DIM), F32)
    c64 = jnp.concatenate([cos, cos, ones], axis=1)
    sa64 = jnp.concatenate([-sin, zeros8, zeros48], axis=1)
    sb64 = jnp.concatenate([zeros8, sin, zeros48], axis=1)
    rep = lambda a: jnp.concatenate([a, a], axis=1)
    pos_blk = jnp.arange(seq, dtype=jnp.int32)[:, None] // SLC_LEN
    lane = jnp.arange(LANE, dtype=jnp.int32)[None, :]
    onehot = ((lane >= 64) & (lane < 96) & (lane - 64 == pos_blk)).astype(F32)
    return rep(c64), rep(sa64), rep(sb64), onehot


def _prep_compress(cmp_pe, cmp_w1, cmp_b1, cmp_w2):
    half = CMP_LEN // 2
    pet, peb, wt, wb, b1 = [], [], [], [], []
    for kv in range(2):
        pe = cmp_pe[kv]
        tile2 = lambda a: jnp.concatenate([a, a], axis=1).reshape(1, half * 2 * HEAD_DIM)
        pet.append(tile2(pe[:half]))
        peb.append(tile2(pe[half:]))
        w1 = cmp_w1[kv].reshape(CMP_LEN, HEAD_DIM, CMP_HIDDEN)
        z = jnp.zeros((half, HEAD_DIM, CMP_HIDDEN), w1.dtype)

        def spread(wh):
            g0 = jnp.concatenate([wh, z], axis=2)
            g1 = jnp.concatenate([z, wh], axis=2)
            return jnp.stack([g0, g1], axis=1).reshape(half * 2 * HEAD_DIM, 2 * CMP_HIDDEN)

        wt.append(spread(w1[:half]))
        wb.append(spread(w1[half:]))
        b1.append(jnp.concatenate([cmp_b1[kv], cmp_b1[kv]])[None, :])
    w2k, w2v = cmp_w2[0], cmp_w2[1]
    zk = jnp.zeros_like(w2k)
    w2k_g = jnp.stack([
        jnp.concatenate([jnp.concatenate([w2k, zk], axis=1), jnp.zeros((CMP_HIDDEN, LANE), w2k.dtype)], axis=0),
        jnp.concatenate([jnp.zeros((CMP_HIDDEN, LANE), w2k.dtype), jnp.concatenate([w2k, zk], axis=1)], axis=0),
    ])
    zv = jnp.zeros_like(w2v)
    w2v_bd = jnp.concatenate([jnp.concatenate([w2v, zv], axis=1),
                              jnp.concatenate([zv, w2v], axis=1)], axis=0)
    st = lambda xs: jnp.stack(xs)
    return (st(pet), st(peb), st(wt).astype(BF16), st(wb).astype(BF16), st(b1),
            w2k_g.astype(BF16), w2v_bd.astype(BF16))


def kernel(x, w_in, cmp_pe, cmp_w1, cmp_b1, cmp_w2, diff_lambda, diff_norm_g, w_branch_a, w_branch_b,
           w_o, ln1_g, ln1_b, w_gate_up, w_down, ln2_g, ln2_b):
    nb, seq, d = x.shape
    assert d == D_MODEL and seq % T == 0 and seq // SLC_LEN == 32 and seq // CMP_STRIDE == 128
    n = nb * seq
    x2 = x.reshape(n, d)
    cos_t, sa_t, sb_t, oh_t = _rope_tables(seq)
    (qraw, qrot, kcs, vcs, ksa, vst, kwa, vwt, qdf, kdf, vdft, gm, gnt) = _inproj(
        x2, w_in.reshape(w_in.shape[1:]).T, cos_t, sa_t, sb_t, oh_t, seq)

    pet, peb, wt, wb, b1, w2k, w2v = _prep_compress(cmp_pe[0], cmp_w1[0], cmp_b1[0], cmp_w2[0])
    kca, vct = _compress(kcs, vcs, pet, peb, wt, wb, b1, w2k, w2v, nb, seq)

    ocmp, sel = _cmpsel(qraw, kca, vct, gnt, nb, seq)
    nt = n // T
    oslc = _slc(qrot, sel, ksa.reshape(nt, T, 256), vst, gnt, nb, seq)
    owin = _win(qrot, kwa.reshape(nt, T, 256), vwt, gnt, nb, seq)
    yb = _diff(qdf, kdf.reshape(nt, T, 512), vdft, diff_lambda[0], diff_norm_g[0][None, :], nb, seq)

    out = _mlp(x2, ocmp, oslc, owin, yb, gm,
               w_branch_a[0].astype(BF16), w_branch_b[0].astype(BF16), w_o[0].astype(BF16),
               ln1_g[0][None, :], ln1_b[0][None, :],
               w_gate_up[0].astype(BF16), w_down[0].astype(BF16), ln2_g[0][None, :], ln2_b[0][None, :])
    return out.reshape(nb, seq, d)
```

```python
import math

import jax
import jax.numpy as jnp
from jax import lax
from jax.experimental import pallas as pl
from jax.experimental.pallas import tpu as pltpu

D_MODEL = 1024
HEAD_DIM = 64
ROPE_DIM = HEAD_DIM // 4
ROPE_THETA = 500000.0
NSA_HEADS = 8
NSA_GROUPS = 2
NSA_HPG = NSA_HEADS // NSA_GROUPS
CMP_LEN = 32
CMP_STRIDE = 16
CMP_HIDDEN = 256
SLC_LEN = 64
SLC_TOPK = 8
WINDOW = 512
FORCE_BONUS = 1.0e4
DIFF_HEADS = 4
DIFF_V_DIM = 2 * HEAD_DIM
FFN_HIDDEN = ((8 * D_MODEL // 3 + 255) // 256) * 256
DEPTH = 1
DEEPNORM_ALPHA = (2 * DEPTH) ** 0.25
NEG = -1.0e30
LN_EPS = 1e-5
RMS_EPS = 1e-5
LAMBDA_INIT = 0.8 - 0.6 * math.exp(-0.3 * 0)
QK_SCALE = HEAD_DIM ** -0.5
QK_SCALE_LOG2 = QK_SCALE * math.log2(math.e)

NSA_Q = NSA_HEADS * HEAD_DIM
NSA_KV = NSA_GROUPS * HEAD_DIM
DIFF_QK = DIFF_HEADS * 2 * HEAD_DIM
DIFF_V = DIFF_HEADS * DIFF_V_DIM
IN_WIDTHS = (NSA_Q, NSA_KV, NSA_KV, NSA_KV, NSA_KV, NSA_KV, NSA_KV, 3 * NSA_HEADS,
             DIFF_QK, DIFF_QK, DIFF_V, 2 * D_MODEL)

LANE = 128
T = 256
TM = 512
VMEM_LIMIT = 56 * 1024 * 1024
GATE_ROWS = 32

BF16 = jnp.bfloat16
F32 = jnp.float32

_C_Q, _C_KC, _C_VC, _C_KS, _C_VS, _C_KW, _C_VW = 0, 512, 640, 768, 896, 1024, 1152
_C_QD, _C_KD, _C_VD, _C_GM, _C_GN, _C_END = 1280, 1792, 2304, 2816, 4864, 4992


def _nt(a, b):
    return lax.dot_general(a, b, (((1,), (1,)), ((), ())), preferred_element_type=F32)


def _nn(a, b):
    return jnp.dot(a, b, preferred_element_type=F32)


def _params(n_axes, vmem=None):
    return pltpu.CompilerParams(dimension_semantics=("arbitrary",) * n_axes,
                                vmem_limit_bytes=vmem)


def _inproj_kernel(x_ref, w_ref, cos_ref, sa_ref, sb_ref, oh_ref,
                   qraw_ref, qrot_ref, kc_ref, vc_ref, ksa_ref, vst_ref, kwa_ref, vwt_ref,
                   qdf_ref, kdf_ref, vdft_ref, gm_ref, gn_ref, w_scr):
    @pl.when(pl.program_id(0) == 0)
    def _():
        gn0 = _C_QD
        gn1 = gn0 + 3 * NSA_HEADS
        rows = 256

        def regroup(r, carry):
            dst = pl.ds(pl.multiple_of(r * rows, rows), rows)
            src = pl.ds(pl.multiple_of(jnp.where(r * rows < gn0, r * rows, r * rows + gn1 - gn0), 8), rows)
            w_scr[dst, :] = w_ref[src, :].astype(BF16)
            return carry

        lax.fori_loop(0, _C_GN // rows, regroup, 0)
        gates = jnp.concatenate([w_ref[gn0:gn1, :], jnp.zeros((LANE - (gn1 - gn0), D_MODEL), F32)], axis=0)
        w_scr[_C_GN:_C_END, :] = gates.astype(BF16)

    lane = lax.broadcasted_iota(jnp.int32, (T, LANE), 1)
    for u in range(TM // T):
        rs = slice(T * u, T * (u + 1))
        xb = x_ref[rs, :].astype(BF16)
        cos = cos_ref[rs, :]
        sa = sa_ref[rs, :]
        sb = sb_ref[rs, :]

        def mm(c0, n):
            return _nt(xb, w_scr[c0:c0 + n, :])

        def rope(t):
            return t * cos + pltpu.roll(t, LANE - 8, 1) * sa + pltpu.roll(t, 8, 1) * sb

        def per_group(k, extra, out_ref):
            out_ref[rs, 0:LANE] = jnp.where(lane < 64, k, extra).astype(BF16)
            out_ref[rs, LANE:2 * LANE] = jnp.where(lane < 64, pltpu.roll(k, 64, 1), extra).astype(BF16)

        t = mm(_C_Q, 512)
        for j in range(4):
            tj = t[:, LANE * j:LANE * (j + 1)]
            qraw_ref[rs, LANE * j:LANE * (j + 1)] = (tj * QK_SCALE_LOG2).astype(BF16)
            qrot_ref[rs, LANE * j:LANE * (j + 1)] = (rope(tj) * QK_SCALE_LOG2).astype(BF16)

        t = mm(_C_KC, 256)
        kc_ref[rs, :] = t[:, :LANE]
        vc_ref[rs, :] = t[:, LANE:]

        t = mm(_C_KS, 256)
        per_group(rope(t[:, :LANE]), oh_ref[rs, :], ksa_ref)
        vst_ref[u] = t[:, LANE:].T.astype(BF16)

        t = mm(_C_KW, 256)
        per_group(rope(t[:, :LANE]), 0.0, kwa_ref)
        vwt_ref[u] = t[:, LANE:].T.astype(BF16)

        t = mm(_C_QD, 512)
        for j in range(4):
            qdf_ref[rs, LANE * j:LANE * (j + 1)] = (rope(t[:, LANE * j:LANE * (j + 1)]) * QK_SCALE_LOG2).astype(BF16)
        t = mm(_C_KD, 512)
        for j in range(4):
            kdf_ref[rs, LANE * j:LANE * (j + 1)] = rope(t[:, LANE * j:LANE * (j + 1)]).astype(BF16)
        t = mm(_C_VD, 512)
        for j in range(4):
            vdft_ref[u, LANE * j:LANE * (j + 1), :] = t[:, LANE * j:LANE * (j + 1)].T.astype(BF16)

        for j in range(4):
            t = mm(_C_GM + 512 * j, 512)
            gm_ref[rs, 512 * j:512 * (j + 1)] = jax.nn.sigmoid(t).astype(BF16)

        gn_ref[u] = jax.nn.sigmoid(mm(_C_GN, 128)).T[0:GATE_ROWS]


def _inproj(x2, w, cos_t, sa_t, sb_t, oh_t, seq):
    n = x2.shape[0]
    nt = n // T
    spt = seq // TM
    assert seq % TM == 0 and TM % T == 0
    assert w.shape == (sum(IN_WIDTHS), D_MODEL) and sum(IN_WIDTHS[:7]) == _C_QD
    assert sum(IN_WIDTHS) - 3 * NSA_HEADS == _C_GN and _C_QD % 256 == 0 and _C_GN % 256 == 0
    row = lambda w: pl.BlockSpec((TM, w), lambda i: (i, 0))
    whole = lambda a: pl.BlockSpec(a.shape, lambda i: (0, 0), pipeline_mode=pl.Buffered(1))
    tab = pl.BlockSpec((TM, LANE), lambda i: (i % spt, 0))
    tile_t = lambda r: pl.BlockSpec((TM // T, r, T), lambda i: (i, 0, 0))
    out_shape = (
        jax.ShapeDtypeStruct((n, 512), BF16),
        jax.ShapeDtypeStruct((n, 512), BF16),
        jax.ShapeDtypeStruct((n, LANE), F32),
        jax.ShapeDtypeStruct((n, LANE), F32),
        jax.ShapeDtypeStruct((n, 256), BF16),
        jax.ShapeDtypeStruct((nt, LANE, T), BF16),
        jax.ShapeDtypeStruct((n, 256), BF16),
        jax.ShapeDtypeStruct((nt, LANE, T), BF16),
        jax.ShapeDtypeStruct((n, 512), BF16),
        jax.ShapeDtypeStruct((n, 512), BF16),
        jax.ShapeDtypeStruct((nt, 512, T), BF16),
        jax.ShapeDtypeStruct((n, 2048), BF16),
        jax.ShapeDtypeStruct((nt, GATE_ROWS, T), F32),
    )
    out_specs = (row(512), row(512), row(LANE), row(LANE), row(256), tile_t(LANE), row(256),
                 tile_t(LANE), row(512), row(512), tile_t(512), row(2048), tile_t(GATE_ROWS))
    return pl.pallas_call(
        _inproj_kernel,
        out_shape=out_shape,
        grid=(n // TM,),
        in_specs=[row(D_MODEL), whole(w), tab, tab, tab, tab],
        out_specs=out_specs,
        scratch_shapes=[pltpu.VMEM((_C_END, D_MODEL), BF16)],
        compiler_params=_params(1, VMEM_LIMIT),
        name="inproj",
    )(x2, w, cos_t, sa_t, sb_t, oh_t)


def _gelu_tanh(x):
    c = math.sqrt(2.0 / math.pi)
    return x * (0.5 * (1.0 + jnp.tanh(c * (x + 0.044715 * (x * x * x)))))


def _compress_kernel(kf_ref, vf_ref, pet_ref, peb_ref, wt_ref, wb_ref, b1_ref, w2k_ref, w2v_ref,
                     kca_ref, vct_ref):
    nchunk = kf_ref.shape[0] // CMP_STRIDE

    def hidden(x_ref, kv):
        a = jnp.zeros((nchunk, 2 * CMP_HIDDEN), F32)
        b = jnp.zeros((nchunk, 2 * CMP_HIDDEN), F32)
        for p in range(0, CMP_STRIDE, 2):
            x = jnp.concatenate([x_ref[pl.ds(p, nchunk, stride=CMP_STRIDE), :],
                                 x_ref[pl.ds(p + 1, nchunk, stride=CMP_STRIDE), :]], axis=1)
            c0, c1 = LANE * p, LANE * (p + 2)
            a = a + _nn((x + pet_ref[kv, :, c0:c1]).astype(BF16), wt_ref[kv, c0:c1, :])
            b = b + _nn((x + peb_ref[kv, :, c0:c1]).astype(BF16), wb_ref[kv, c0:c1, :])
        h = a + pltpu.roll(b, nchunk - 1, 0) + b1_ref[kv]
        return _gelu_tanh(h).astype(BF16)

    hk = hidden(kf_ref, 0)
    for g in range(2):
        kca_ref[g] = _nn(hk, w2k_ref[g]).astype(BF16)
    hv = hidden(vf_ref, 1)
    vct_ref[...] = _nn(hv, w2v_ref[...]).T.astype(BF16)


def _compress(kf, vf, pet, peb, wt, wb, b1, w2k, w2v, nb, seq):
    full = lambda a: pl.BlockSpec(a.shape, lambda b: (0,) * a.ndim)
    return pl.pallas_call(
        _compress_kernel,
        out_shape=(jax.ShapeDtypeStruct((nb, 2, 128, LANE), BF16),
                   jax.ShapeDtypeStruct((nb, LANE, 128), BF16)),
        grid=(nb,),
        in_specs=[pl.BlockSpec((seq, LANE), lambda b: (b, 0)),
                  pl.BlockSpec((seq, LANE), lambda b: (b, 0)),
                  full(pet), full(peb), full(wt), full(wb), full(b1), full(w2k), full(w2v)],
        out_specs=(pl.BlockSpec((None, 2, 128, LANE), lambda b: (b, 0, 0, 0)),
                   pl.BlockSpec((None, LANE, 128), lambda b: (b, 0, 0))),
        compiler_params=_params(1, VMEM_LIMIT),
        name="compress",
    )(kf, vf, pet, peb, wt, wb, b1, w2k, w2v)


def _cmpsel_kernel(q_ref, kca_ref, vct_ref, gnt_ref, ocmp_ref, sel_ref, s_scr):
    nq = q_ref.shape[0] // T
    lane = lax.broadcasted_iota(jnp.int32, (T, LANE), 1)
    crow = lax.broadcasted_iota(jnp.int32, (128, T), 0)
    tcol0 = lax.broadcasted_iota(jnp.int32, (128, T), 1)
    tlane = lax.broadcasted_iota(jnp.int32, (1, T), 1)

    jj = lax.broadcasted_iota(jnp.int32, (32, 128), 0) * SLC_LEN
    cc = lax.broadcasted_iota(jnp.int32, (32, 128), 1) * CMP_STRIDE
    ov = jnp.maximum(jnp.minimum(cc + CMP_LEN, jj + SLC_LEN) - jnp.maximum(cc, jj), 0)
    ovt = (ov.astype(F32) * (1.0 / CMP_LEN)).astype(BF16)

    jrow = lax.broadcasted_iota(jnp.int32, (32, T), 0)
    jrow8 = lax.broadcasted_iota(jnp.int32, (8, T), 0)
    tblk0 = lax.broadcasted_iota(jnp.int32, (32, T), 1)

    def scores(i, buf):
        rows = pl.ds(pl.multiple_of(i * T, T), T)
        for hp in range(NSA_HEADS // 2):
            g = hp // (NSA_HPG // 2)
            qp = q_ref[rows, LANE * hp:LANE * (hp + 1)].astype(F32)
            qs = (jnp.where(lane < 64, qp, 0.0), jnp.where(lane < 64, pltpu.roll(qp, 64, 1), 0.0))
            for par in range(2):
                s_scr[buf, 2 * hp + par] = _nt(kca_ref[g], qs[par].astype(BF16))

    def attend_select(i, buf):
        t0 = i * T
        rows = pl.ds(pl.multiple_of(t0, T), T)
        cmask = (crow * CMP_STRIDE + (CMP_LEN - 1)) <= tcol0 + t0
        live = jnp.where(tlane + t0 >= (CMP_LEN - 1), 1.0, 0.0)
        tblk = (tblk0 + t0) // SLC_LEN
        valid = jrow <= tblk
        forced = (jrow == 0) | (jrow == tblk) | (jrow == tblk - 1)
        bonus = jnp.where(forced, FORCE_BONUS, 0.0)
        for g in range(NSA_GROUPS):
            vt = vct_ref[64 * g:64 * (g + 1), :]
            psum = jnp.zeros((128, T), F32)
            for hp in range(NSA_HPG // 2):
                outs = []
                for par in range(2):
                    h = NSA_HPG * g + 2 * hp + par
                    s = jnp.where(cmask, s_scr[buf, h], NEG)
                    m = jnp.max(s, axis=0, keepdims=True)
                    e = jnp.exp2(s - m)
                    p = e * (live / jnp.sum(e, axis=0, keepdims=True))
                    psum = psum + p
                    outs.append(_nn(vt, p.astype(BF16)) * gnt_ref[i, 3 * h:3 * h + 1, :])
                ocmp_ref[rows, LANE * (2 * g + hp):LANE * (2 * g + hp + 1)] = (
                    jnp.concatenate(outs, axis=0).T.astype(BF16))

            p_hi = psum.astype(BF16)
            p_lo = (psum - p_hi.astype(F32)).astype(BF16)
            pslc = _nn(ovt, p_hi) + _nn(ovt, p_lo)
            pri = jnp.where(valid, pslc + bonus, -1.0)
            rank = [jnp.zeros((8, T), F32) for _ in range(4)]
            for r in range(32):
                row = pri[r:r + 1, :]
                for a in range(4):
                    pa = pri[8 * a:8 * (a + 1), :]
                    if 8 * a > r:
                        ahead = jnp.where(row >= pa, 1.0, 0.0)
                    elif 8 * a + 7 < r:
                        ahead = jnp.where(row > pa, 1.0, 0.0)
                    else:
                        ahead = jnp.where(jrow8 + 8 * a > r, jnp.where(row >= pa, 1.0, 0.0),
                                          jnp.where(row > pa, 1.0, 0.0))
                    rank[a] = rank[a] + ahead
            rank = jnp.concatenate(rank, axis=0)
            selneg = jnp.where(rank < float(SLC_TOPK), 0.0, NEG)
            pad = jnp.concatenate([jnp.zeros((64, T), F32), selneg, jnp.zeros((32, T), F32)], axis=0)
            sel_ref[rows, LANE * g:LANE * (g + 1)] = pad.T.astype(BF16)

    scores(0, 0)

    def body(u, carry):
        scores(2 * u + 1, 1)
        attend_select(2 * u, 0)
        scores(jnp.minimum(2 * u + 2, nq - 1), 0)
        attend_select(2 * u + 1, 1)
        return carry

    lax.fori_loop(0, nq // 2, body, 0)


def _cmpsel(qraw, kca, vct, gnt, nb, seq):
    n = qraw.shape[0]
    nq = seq // T
    assert nq % 2 == 0
    return pl.pallas_call(
        _cmpsel_kernel,
        out_shape=(jax.ShapeDtypeStruct((n, 512), BF16), jax.ShapeDtypeStruct((n, 256), BF16)),
        grid=(nb,),
        in_specs=[pl.BlockSpec((seq, 512), lambda b: (b, 0)),
                  pl.BlockSpec((None, 2, 128, LANE), lambda b: (b, 0, 0, 0)),
                  pl.BlockSpec((None, LANE, 128), lambda b: (b, 0, 0)),
                  pl.BlockSpec((nq, GATE_ROWS, T), lambda b: (b, 0, 0))],
        out_specs=(pl.BlockSpec((seq, 512), lambda b: (b, 0)),
                   pl.BlockSpec((seq, 256), lambda b: (b, 0))),
        scratch_shapes=[pltpu.VMEM((2, NSA_HEADS, 128, T), F32)],
        compiler_params=_params(1, VMEM_LIMIT),
        name="cmpsel",
    )(qraw, kca, vct, gnt)


def _split_heads(q_ref, extras, qs_scr, base=0):
    lane = lax.broadcasted_iota(jnp.int32, (T, LANE), 1)
    for hp in range(NSA_HEADS // 2):
        extra = extras[hp // (NSA_HPG // 2)]
        qp = q_ref[:, LANE * hp:LANE * (hp + 1)].astype(F32)
        qs_scr[base + 2 * hp] = jnp.where(lane < 64, qp, extra).astype(BF16)
        qs_scr[base + 2 * hp + 1] = jnp.where(lane < 64, pltpu.roll(qp, 64, 1), extra).astype(BF16)


SM_ROWS = 32


RING = 4


def _tile_schedule(nq):
    below = [(i, j) for j in range(nq) for i in range(j + 1, nq)]
    assert len(below) == (RING - 1) * nq + RING
    pairs = []
    for i in range(nq):
        pairs += [(i, i)] + below[(RING - 1) * i:(RING - 1) * (i + 1)]
    pairs += below[(RING - 1) * nq:]
    return (jnp.asarray([p[0] for p in pairs], jnp.int32), jnp.asarray([p[1] for p in pairs], jnp.int32))


def _flash_scratch(nq, nc, acc_rows):
    return [pltpu.VMEM((nq * nc, T, LANE), BF16),
            pltpu.VMEM((RING, nc, T, T), F32),
            pltpu.VMEM((RING, nc, 1, T), F32),
            pltpu.VMEM((nq * nc, 1, T), F32),
            pltpu.VMEM((nq * nc, 1, T), F32),
            pltpu.VMEM((nq * acc_rows // LANE, LANE, T), F32)]


def _causal_flash(it_ref, jt_ref, nq, k_ref, vt_ref, qs_scr, s_scr, mx_scr, m_scr, l_scr, acc_scr, chains, acc_of):
    nc = len(chains)
    nsets = it_ref.shape[0]
    assert nsets % RING == 0 and nsets >= 2 * RING
    m_scr[...] = jnp.full(m_scr.shape, NEG, F32)
    l_scr[...] = jnp.zeros(l_scr.shape, F32)
    acc_scr[...] = jnp.zeros(acc_scr.shape, F32)
    krow = lax.broadcasted_iota(jnp.int32, (T, T), 0)
    qcol = lax.broadcasted_iota(jnp.int32, (T, T), 1)
    ones = jnp.ones((16, T), BF16)

    def scores(n, buf):
        i, j = it_ref[n], jt_ref[n]
        if buf == 0:
            mask = krow <= qcol + jnp.where(i == j, 0, T)
        for c, (kl, _, _) in enumerate(chains):
            s = _nt(k_ref[j, :, kl:kl + LANE], qs_scr[i * nc + c])
            if buf == 0:
                s = jnp.where(mask, s, NEG)
            s_scr[buf, c] = s
            mx_scr[buf, c] = jnp.max(s, axis=0, keepdims=True)

    def softmax_pv(n, buf):
        i, j = it_ref[n], jt_ref[n]
        for c, (_, vr, dv) in enumerate(chains):
            m_prev = m_scr[i * nc + c]
            m_new = jnp.maximum(m_prev, mx_scr[buf, c])
            alpha = jnp.exp2(m_prev - m_new)
            parts = [jnp.exp2(s_scr[buf, c, r:r + SM_ROWS, :] - m_new).astype(BF16)
                     for r in range(0, T, SM_ROWS)]
            vt = jnp.concatenate([vt_ref[j, vr:vr + dv, :], ones], axis=0)
            o = _nn(vt, jnp.concatenate(parts, axis=0))
            l_scr[i * nc + c] = alpha * l_scr[i * nc + c] + o[dv:dv + 1]
            m_scr[i * nc + c] = m_new
            acc = acc_of(i, c)
            acc[...] = acc[...] * alpha + o[0:dv]

    scores(0, 0)

    def body(u, carry):
        n = RING * u
        for r in range(RING):
            scores(n + r + 1, (r + 1) % RING)
            softmax_pv(n + r, r)
        return carry

    lax.fori_loop(0, nsets // RING - 1, body, 0)
    n = nsets - RING
    for r in range(RING):
        if r + 1 < RING:
            scores(n + r + 1, r + 1)
        softmax_pv(n + r, r)


def _slc_kernel(it_ref, jt_ref, q_ref, sel_ref, k_ref, vt_ref, gnt_ref, o_ref,
                qs_scr, s_scr, mx_scr, m_scr, l_scr, acc_scr):
    nq = q_ref.shape[0] // T
    nh = NSA_HEADS

    def prep(i, carry):
        rows = pl.ds(pl.multiple_of(i * T, T), T)
        sel = sel_ref[rows, :].astype(F32)
        _split_heads(q_ref.at[rows, :], [sel[:, 0:LANE], sel[:, LANE:2 * LANE]], qs_scr, i * nh)
        return carry

    lax.fori_loop(0, nq, prep, 0)
    chains = [(LANE * (h // NSA_HPG), 64 * (h // NSA_HPG), 64) for h in range(nh)]
    acc_of = lambda i, h: acc_scr.at[i * (nh // 2) + h // 2, 64 * (h % 2):64 * (h % 2 + 1), :]
    _causal_flash(it_ref, jt_ref, nq, k_ref, vt_ref, qs_scr, s_scr, mx_scr, m_scr, l_scr, acc_scr, chains, acc_of)

    def finish(i, carry):
        rows = pl.ds(pl.multiple_of(i * T, T), T)
        for hp in range(nh // 2):
            scale = [gnt_ref[i, 3 * h + 1:3 * h + 2, :] / l_scr[i * nh + h] for h in (2 * hp, 2 * hp + 1)]
            inv = jnp.concatenate([jnp.broadcast_to(scale[0], (64, T)), jnp.broadcast_to(scale[1], (64, T))],
                                  axis=0)
            o_ref[rows, LANE * hp:LANE * (hp + 1)] = (acc_scr[i * (nh // 2) + hp] * inv).T.astype(BF16)
        return carry

    lax.fori_loop(0, nq, finish, 0)


def _slc(qrot, sel, ksa3, vst, gnt3, nb, seq):
    n = qrot.shape[0]
    nq = seq // T
    it, jt = _tile_schedule(nq)
    grid_spec = pltpu.PrefetchScalarGridSpec(
        num_scalar_prefetch=2,
        grid=(nb,),
        in_specs=[pl.BlockSpec((seq, 512), lambda b, it, jt: (b, 0)),
                  pl.BlockSpec((seq, 256), lambda b, it, jt: (b, 0)),
                  pl.BlockSpec((nq, T, 256), lambda b, it, jt: (b, 0, 0)),
                  pl.BlockSpec((nq, LANE, T), lambda b, it, jt: (b, 0, 0)),
                  pl.BlockSpec((nq, GATE_ROWS, T), lambda b, it, jt: (b, 0, 0))],
        out_specs=pl.BlockSpec((seq, 512), lambda b, it, jt: (b, 0)),
        scratch_shapes=_flash_scratch(nq, NSA_HEADS, NSA_Q))
    return pl.pallas_call(
        _slc_kernel,
        out_shape=jax.ShapeDtypeStruct((n, 512), BF16),
        grid_spec=grid_spec,
        compiler_params=_params(1, VMEM_LIMIT),
        name="slc",
    )(it, jt, qrot, sel, ksa3, vst, gnt3)


def _win_kernel(q_ref, k_ref, vt_ref, gnt_ref, o_ref, qs_scr, s_scr, mx_scr, acc_scr):
    i = pl.program_id(1)
    _split_heads(q_ref, [0.0, 0.0], qs_scr)
    krow = lax.broadcasted_iota(jnp.int32, (T, T), 0)
    qcol = lax.broadcasted_iota(jnp.int32, (T, T), 1)
    span = WINDOW // T
    masks = {0: krow <= qcol, span: (krow > qcol) & (i >= span)}
    ones = jnp.ones((16, T), BF16)

    def scores(hp, buf):
        for par in range(2):
            h = 2 * hp + par
            g = h // NSA_HPG
            mx = None
            for d in range(span + 1):
                j = i - d
                s = _nt(k_ref[jnp.maximum(j, 0), :, LANE * g:LANE * (g + 1)], qs_scr[h])
                if d in masks:
                    s = jnp.where(masks[d], s, NEG)
                else:
                    s = s + jnp.where(j >= 0, 0.0, NEG)
                s_scr[buf, par, d] = s
                md = jnp.max(s, axis=0, keepdims=True)
                mx = md if mx is None else jnp.maximum(mx, md)
            mx_scr[buf, par] = mx

    def softmax_pv(hp, buf):
        for par in range(2):
            g = (2 * hp + par) // NSA_HPG
            m = mx_scr[buf, par]
            acc = jnp.zeros((64 + 16, T), F32)
            for d in range(span + 1):
                parts = [jnp.exp2(s_scr[buf, par, d, r:r + SM_ROWS, :] - m).astype(BF16)
                         for r in range(0, T, SM_ROWS)]
                vt = jnp.concatenate([vt_ref[jnp.maximum(i - d, 0), 64 * g:64 * (g + 1), :], ones], axis=0)
                acc = acc + _nn(vt, jnp.concatenate(parts, axis=0))
            gate = gnt_ref[3 * (2 * hp + par) + 2:3 * (2 * hp + par) + 3, :]
            acc_scr[hp, 64 * par:64 * (par + 1), :] = acc[0:64] * (gate / acc[64:65])

    npairs = NSA_HEADS // 2
    scores(0, 0)
    for hp in range(npairs):
        if hp + 1 < npairs:
            scores(hp + 1, (hp + 1) % 2)
        softmax_pv(hp, hp % 2)
    for hp in range(npairs):
        o_ref[:, LANE * hp:LANE * (hp + 1)] = acc_scr[hp].T.astype(BF16)


def _win(qrot, kwa3, vwt, gnt, nb, seq):
    n = qrot.shape[0]
    nq = seq // T
    return pl.pallas_call(
        _win_kernel,
        out_shape=jax.ShapeDtypeStruct((n, 512), BF16),
        grid=(nb, nq),
        in_specs=[pl.BlockSpec((T, 512), lambda b, i: (b * nq + i, 0)),
                  pl.BlockSpec((nq, T, 256), lambda b, i: (b, 0, 0)),
                  pl.BlockSpec((nq, LANE, T), lambda b, i: (b, 0, 0)),
                  pl.BlockSpec((None, GATE_ROWS, T), lambda b, i: (b * nq + i, 0, 0))],
        out_specs=pl.BlockSpec((T, 512), lambda b, i: (b * nq + i, 0)),
        scratch_shapes=[pltpu.VMEM((NSA_HEADS, T, LANE), BF16),
                        pltpu.VMEM((2, 2, WINDOW // T + 1, T, T), F32),
                        pltpu.VMEM((2, 2, 1, T), F32),
                        pltpu.VMEM((NSA_HEADS // 2, LANE, T), F32)],
        compiler_params=_params(2, VMEM_LIMIT),
        name="win",
    )(qrot, kwa3, vwt, gnt)


def _diff_kernel(it_ref, jt_ref, q_ref, k_ref, vt_ref, lam_ref, g_ref, o_ref,
                 qs_scr, s_scr, mx_scr, m_scr, l_scr, acc_scr):
    nq = q_ref.shape[0] // T
    nc = 2 * DIFF_HEADS
    lane = lax.broadcasted_iota(jnp.int32, (T, LANE), 1)

    def prep(i, carry):
        rows = pl.ds(pl.multiple_of(i * T, T), T)
        for h in range(DIFF_HEADS):
            q = q_ref[rows, LANE * h:LANE * (h + 1)].astype(F32)
            qs_scr[i * nc + 2 * h] = jnp.where(lane < 64, q, 0.0).astype(BF16)
            qs_scr[i * nc + 2 * h + 1] = jnp.where(lane < 64, 0.0, q).astype(BF16)
        return carry

    lax.fori_loop(0, nq, prep, 0)
    chains = [(LANE * (c // 2), LANE * (c // 2), LANE) for c in range(nc)]
    _causal_flash(it_ref, jt_ref, nq, k_ref, vt_ref, qs_scr, s_scr, mx_scr, m_scr, l_scr, acc_scr, chains,
                  lambda i, c: acc_scr.at[i * nc + c])

    lp = lam_ref[...]
    lam = (jnp.exp(jnp.sum(lp[0:1] * lp[1:2], axis=1, keepdims=True))
           - jnp.exp(jnp.sum(lp[2:3] * lp[3:4], axis=1, keepdims=True)) + LAMBDA_INIT)

    def finish(i, carry):
        rows = pl.ds(pl.multiple_of(i * T, T), T)
        for h in range(DIFF_HEADS):
            c = i * nc + 2 * h
            o = acc_scr[c] * (1.0 / l_scr[c]) - lam * (acc_scr[c + 1] * (1.0 / l_scr[c + 1]))
            o = o * lax.rsqrt(jnp.mean(o * o, axis=0, keepdims=True) + RMS_EPS)
            o_ref[rows, LANE * h:LANE * (h + 1)] = ((o.T * g_ref[...]) * (1.0 - LAMBDA_INIT)).astype(BF16)
        return carry

    lax.fori_loop(0, nq, finish, 0)


def _diff(qdf, kdf3, vdft, lam, norm_g, nb, seq):
    n = qdf.shape[0]
    nq = seq // T
    it, jt = _tile_schedule(nq)
    grid_spec = pltpu.PrefetchScalarGridSpec(
        num_scalar_prefetch=2,
        grid=(nb,),
        in_specs=[pl.BlockSpec((seq, 512), lambda b, it, jt: (b, 0)),
                  pl.BlockSpec((nq, T, 512), lambda b, it, jt: (b, 0, 0)),
                  pl.BlockSpec((nq, 512, T), lambda b, it, jt: (b, 0, 0)),
                  pl.BlockSpec((4, HEAD_DIM), lambda b, it, jt: (0, 0)),
                  pl.BlockSpec((1, LANE), lambda b, it, jt: (0, 0))],
        out_specs=pl.BlockSpec((seq, 512), lambda b, it, jt: (b, 0)),
        scratch_shapes=_flash_scratch(nq, 2 * DIFF_HEADS, 2 * DIFF_HEADS * DIFF_V_DIM))
    return pl.pallas_call(
        _diff_kernel,
        out_shape=jax.ShapeDtypeStruct((n, 512), BF16),
        grid_spec=grid_spec,
        compiler_params=_params(1, VMEM_LIMIT),
        name="diff",
    )(it, jt, qdf, kdf3, vdft, lam, norm_g)


def _layer_norm(z, g, b):
    mu = jnp.mean(z, axis=-1, keepdims=True)
    zc = z - mu
    var = jnp.mean(zc * zc, axis=-1, keepdims=True)
    return zc * lax.rsqrt(var + LN_EPS) * g + b


def _mlp_kernel(x_ref, ocmp_ref, oslc_ref, owin_ref, yb_ref, gm_ref, wa_ref, wb_ref, wo_ref, g1_ref, b1_ref,
                wgu_ref, wd_ref, g2_ref, b2_ref, o_ref, h_scr):
    s = pl.program_id(0)

    @pl.when(s == 0)
    def _():
        h_scr[1] = jnp.zeros(h_scr.shape[1:], F32)

    for u in range(TM // T):
        rs = slice(T * u, T * (u + 1))
        h = h_scr[(s + 1) % 2, rs, :]
        hb = h.astype(BF16)
        gate = _nn(hb, wgu_ref[:, 0:FFN_HIDDEN])
        up = _nn(hb, wgu_ref[:, FFN_HIDDEN:2 * FFN_HIDDEN])
        act = (gate * jax.nn.sigmoid(gate) * up).astype(BF16)
        y = _nn(act, wd_ref[...])
        o_ref[rs, :] = _layer_norm(DEEPNORM_ALPHA * h + y, g2_ref[...], b2_ref[...])

        ya = ocmp_ref[rs, :].astype(F32) + oslc_ref[rs, :].astype(F32) + owin_ref[rs, :].astype(F32)
        ta = _nn(ya.astype(BF16), wa_ref[...])
        tb = _nn(yb_ref[rs, :], wb_ref[...])
        merged = gm_ref[rs, 0:D_MODEL].astype(F32) * ta + gm_ref[rs, D_MODEL:2 * D_MODEL].astype(F32) * tb
        mix = _nn(merged.astype(BF16), wo_ref[...])
        h_scr[s % 2, rs, :] = _layer_norm(DEEPNORM_ALPHA * x_ref[rs, :] + mix, g1_ref[...], b1_ref[...])


def _mlp(x2, ocmp, oslc, owin, yb, gm, wa, wb, wo, g1, b1, wgu, wd, g2, b2):
    n = x2.shape[0]
    nt = n // TM
    row = lambda w: pl.BlockSpec((TM, w), lambda s: (jnp.minimum(s, nt - 1), 0))
    full = lambda a: pl.BlockSpec(a.shape, lambda s: (0,) * a.ndim, pipeline_mode=pl.Buffered(1))
    return pl.pallas_call(
        _mlp_kernel,
        out_shape=jax.ShapeDtypeStruct((n, D_MODEL), F32),
        grid=(nt + 1,),
        in_specs=[row(D_MODEL), row(512), row(512), row(512), row(512), row(2048),
                  full(wa), full(wb), full(wo), full(g1), full(b1), full(wgu), full(wd), full(g2), full(b2)],
        out_specs=pl.BlockSpec((TM, D_MODEL), lambda s: (jnp.maximum(s - 1, 0), 0)),
        scratch_shapes=[pltpu.VMEM((2, TM, D_MODEL), F32)],
        compiler_params=_params(1, VMEM_LIMIT),
        name="mlp",
    )(x2, ocmp, oslc, owin, yb, gm, wa, wb, wo, g1, b1, wgu, wd, g2, b2)


def _rope_tables(seq):
    half = ROPE_DIM // 2
    inv_freq = ROPE_THETA ** (-jnp.arange(half, dtype=F32) * 2.0 / ROPE_DIM)
    ang = jnp.arange(seq, dtype=F32)[:, None] * inv_freq[None, :]
    cos, sin = jnp.cos(ang), jnp.sin(ang)
    ones = jnp.ones((seq, HEAD_DIM - ROPE_DIM), F32)
    zeros8 = jnp.zeros((seq, half), F32)
    zeros48 = jnp.zeros((seq, HEAD_DIM - ROPE_DIM), F32)
    c64 = jnp.concatenate([cos, cos, ones], axis=1)
    sa64 = jnp.concatenate([-sin, zeros8, zeros48], axis=1)
    sb64 = jnp.concatenate([zeros8, sin, zeros48], axis=1)
    rep = lambda a: jnp.concatenate([a, a], axis=1)
    pos_blk = jnp.arange(seq, dtype=jnp.int32)[:, None] // SLC_LEN
    lane = jnp.arange(LANE, dtype=jnp.int32)[None, :]
    onehot = ((lane >= 64) & (lane < 96) & (lane - 64 == pos_blk)).astype(F32)
    return rep(c64), rep(sa64), rep(sb64), onehot


def _prep_compress(cmp_pe, cmp_w1, cmp_b1, cmp_w2):
    half = CMP_LEN // 2
    pet, peb, wt, wb, b1 = [], [], [], [], []
    for kv in range(2):
        pe = cmp_pe[kv]
        tile2 = lambda a: jnp.concatenate([a, a], axis=1).reshape(1, half * 2 * HEAD_DIM)
        pet.append(tile2(pe[:half]))
        peb.append(tile2(pe[half:]))
        w1 = cmp_w1[kv].reshape(CMP_LEN, HEAD_DIM, CMP_HIDDEN)
        z = jnp.zeros((half, HEAD_DIM, CMP_HIDDEN), w1.dtype)

        def spread(wh):
            g0 = jnp.concatenate([wh, z], axis=2)
            g1 = jnp.concatenate([z, wh], axis=2)
            return jnp.stack([g0, g1], axis=1).reshape(half * 2 * HEAD_DIM, 2 * CMP_HIDDEN)

        wt.append(spread(w1[:half]))
        wb.append(spread(w1[half:]))
        b1.append(jnp.concatenate([cmp_b1[kv], cmp_b1[kv]])[None, :])
    w2k, w2v = cmp_w2[0], cmp_w2[1]
    zk = jnp.zeros_like(w2k)
    w2k_g = jnp.stack([
        jnp.concatenate([jnp.concatenate([w2k, zk], axis=1), jnp.zeros((CMP_HIDDEN, LANE), w2k.dtype)], axis=0),
        jnp.concatenate([jnp.zeros((CMP_HIDDEN, LANE), w2k.dtype), jnp.concatenate([w2k, zk], axis=1)], axis=0),
    ])
    zv = jnp.zeros_like(w2v)
    w2v_bd = jnp.concatenate([jnp.concatenate([w2v, zv], axis=1),
                              jnp.concatenate([zv, w2v], axis=1)], axis=0)
    st = lambda xs: jnp.stack(xs)
    return (st(pet), st(peb), st(wt).astype(BF16), st(wb).astype(BF16), st(b1),
            w2k_g.astype(BF16), w2v_bd.astype(BF16))


def kernel(x, w_in, cmp_pe, cmp_w1, cmp_b1, cmp_w2, diff_lambda, diff_norm_g, w_branch_a, w_branch_b,
           w_o, ln1_g, ln1_b, w_gate_up, w_down, ln2_g, ln2_b):
    nb, seq, d = x.shape
    assert d == D_MODEL and seq % T == 0 and seq // SLC_LEN == 32 and seq // CMP_STRIDE == 128
    n = nb * seq
    x2 = x.reshape(n, d)
    cos_t, sa_t, sb_t, oh_t = _rope_tables(seq)
    (qraw, qrot, kcs, vcs, ksa, vst, kwa, vwt, qdf, kdf, vdft, gm, gnt) = _inproj(
        x2, w_in.reshape(w_in.shape[1:]).T, cos_t, sa_t, sb_t, oh_t, seq)

    pet, peb, wt, wb, b1, w2k, w2v = _prep_compress(cmp_pe[0], cmp_w1[0], cmp_b1[0], cmp_w2[0])
    kca, vct = _compress(kcs, vcs, pet, peb, wt, wb, b1, w2k, w2v, nb, seq)

    ocmp, sel = _cmpsel(qraw, kca, vct, gnt, nb, seq)
    nt = n // T
    oslc = _slc(qrot, sel, ksa.reshape(nt, T, 256), vst, gnt, nb, seq)
    owin = _win(qrot, kwa.reshape(nt, T, 256), vwt, gnt, nb, seq)
    yb = _diff(qdf, kdf.reshape(nt, T, 512), vdft, diff_lambda[0], diff_norm_g[0][None, :], nb, seq)

    out = _mlp(x2, ocmp, oslc, owin, yb, gm,
               w_branch_a[0].astype(BF16), w_branch_b[0].astype(BF16), w_o[0].astype(BF16),
               ln1_g[0][None, :], ln1_b[0][None, :],
               w_gate_up[0].astype(BF16), w_down[0].astype(BF16), ln2_g[0][None, :], ln2_b[0][None, :])
    return out.reshape(nb, seq, d)
```

```python
import math

import jax
import jax.numpy as jnp
from jax import lax
from jax.experimental import pallas as pl
from jax.experimental.pallas import tpu as pltpu

D_MODEL = 1024
HEAD_DIM = 64
ROPE_DIM = HEAD_DIM // 4
ROPE_THETA = 500000.0
NSA_HEADS = 8
NSA_GROUPS = 2
NSA_HPG = NSA_HEADS // NSA_GROUPS
CMP_LEN = 32
CMP_STRIDE = 16
CMP_HIDDEN = 256
SLC_LEN = 64
SLC_TOPK = 8
WINDOW = 512
FORCE_BONUS = 1.0e4
DIFF_HEADS = 4
DIFF_V_DIM = 2 * HEAD_DIM
FFN_HIDDEN = ((8 * D_MODEL // 3 + 255) // 256) * 256
DEPTH = 1
DEEPNORM_ALPHA = (2 * DEPTH) ** 0.25
NEG = -1.0e30
LN_EPS = 1e-5
RMS_EPS = 1e-5
LAMBDA_INIT = 0.8 - 0.6 * math.exp(-0.3 * 0)
QK_SCALE = HEAD_DIM ** -0.5
QK_SCALE_LOG2 = QK_SCALE * math.log2(math.e)

NSA_Q = NSA_HEADS * HEAD_DIM
NSA_KV = NSA_GROUPS * HEAD_DIM
DIFF_QK = DIFF_HEADS * 2 * HEAD_DIM
DIFF_V = DIFF_HEADS * DIFF_V_DIM
IN_WIDTHS = (NSA_Q, NSA_KV, NSA_KV, NSA_KV, NSA_KV, NSA_KV, NSA_KV, 3 * NSA_HEADS,
             DIFF_QK, DIFF_QK, DIFF_V, 2 * D_MODEL)

LANE = 128
T = 256
TM = 512
VMEM_LIMIT = 56 * 1024 * 1024
GATE_ROWS = 32

BF16 = jnp.bfloat16
F32 = jnp.float32

_C_Q, _C_KC, _C_VC, _C_KS, _C_VS, _C_KW, _C_VW = 0, 512, 640, 768, 896, 1024, 1152
_C_QD, _C_KD, _C_VD, _C_GM, _C_GN, _C_END = 1280, 1792, 2304, 2816, 4864, 4992


def _nt(a, b):
    return lax.dot_general(a, b, (((1,), (1,)), ((), ())), preferred_element_type=F32)


def _nn(a, b):
    return jnp.dot(a, b, preferred_element_type=F32)


def _params(n_axes, vmem=None):
    return pltpu.CompilerParams(dimension_semantics=("arbitrary",) * n_axes,
                                vmem_limit_bytes=vmem)


def _inproj_kernel(x_ref, w_ref, cos_ref, sa_ref, sb_ref, oh_ref,
                   qraw_ref, qrot_ref, kc_ref, vc_ref, ksa_ref, vst_ref, kwa_ref, vwt_ref,
                   qdf_ref, kdf_ref, vdft_ref, gm_ref, gn_ref, w_scr):
    @pl.when(pl.program_id(0) == 0)
    def _():
        gn0 = _C_QD
        gn1 = gn0 + 3 * NSA_HEADS
        rows = 256

        def regroup(r, carry):
            dst = pl.ds(pl.multiple_of(r * rows, rows), rows)
            src = pl.ds(pl.multiple_of(jnp.where(r * rows < gn0, r * rows, r * rows + gn1 - gn0), 8), rows)
            w_scr[dst, :] = w_ref[src, :].astype(BF16)
            return carry

        lax.fori_loop(0, _C_GN // rows, regroup, 0)
        gates = jnp.concatenate([w_ref[gn0:gn1, :], jnp.zeros((LANE - (gn1 - gn0), D_MODEL), F32)], axis=0)
        w_scr[_C_GN:_C_END, :] = gates.astype(BF16)

    lane = lax.broadcasted_iota(jnp.int32, (T, LANE), 1)
    for u in range(TM // T):
        rs = slice(T * u, T * (u + 1))
        xb = x_ref[rs, :].astype(BF16)
        cos = cos_ref[rs, :]
        sa = sa_ref[rs, :]
        sb = sb_ref[rs, :]

        def mm(c0, n):
            return _nt(xb, w_scr[c0:c0 + n, :])

        def rope(t):
            return t * cos + pltpu.roll(t, LANE - 8, 1) * sa + pltpu.roll(t, 8, 1) * sb

        def per_group(k, extra, out_ref):
            out_ref[rs, 0:LANE] = jnp.where(lane < 64, k, extra).astype(BF16)
            out_ref[rs, LANE:2 * LANE] = jnp.where(lane < 64, pltpu.roll(k, 64, 1), extra).astype(BF16)

        t = mm(_C_Q, 512)
        for j in range(4):
            tj = t[:, LANE * j:LANE * (j + 1)]
            qraw_ref[rs, LANE * j:LANE * (j + 1)] = (tj * QK_SCALE_LOG2).astype(BF16)
            qrot_ref[rs, LANE * j:LANE * (j + 1)] = (rope(tj) * QK_SCALE_LOG2).astype(BF16)

        t = mm(_C_KC, 256)
        kc_ref[rs, :] = t[:, :LANE]
        vc_ref[rs, :] = t[:, LANE:]

        t = mm(_C_KS, 256)
        per_group(rope(t[:, :LANE]), oh_ref[rs, :], ksa_ref)
        vst_ref[u] = t[:, LANE:].T.astype(BF16)

        t = mm(_C_KW, 256)
        per_group(rope(t[:, :LANE]), 0.0, kwa_ref)
        vwt_ref[u] = t[:, LANE:].T.astype(BF16)

        t = mm(_C_QD, 512)
        for j in range(4):
            qdf_ref[rs, LANE * j:LANE * (j + 1)] = (rope(t[:, LANE * j:LANE * (j + 1)]) * QK_SCALE_LOG2).astype(BF16)
        t = mm(_C_KD, 512)
        for j in range(4):
            kdf_ref[rs, LANE * j:LANE * (j + 1)] = rope(t[:, LANE * j:LANE * (j + 1)]).astype(BF16)
        t = mm(_C_VD, 512)
        for j in range(4):
            vdft_ref[u, LANE * j:LANE * (j + 1), :] = t[:, LANE * j:LANE * (j + 1)].T.astype(BF16)

        for j in range(4):
            t = mm(_C_GM + 512 * j, 512)
            gm_ref[rs, 512 * j:512 * (j + 1)] = jax.nn.sigmoid(t).astype(BF16)

        gn_ref[u] = jax.nn.sigmoid(mm(_C_GN, 128)).T[0:GATE_ROWS]


def _inproj(x2, w, cos_t, sa_t, sb_t, oh_t, seq):
    n = x2.shape[0]
    nt = n // T
    spt = seq // TM
    assert seq % TM == 0 and TM % T == 0
    assert w.shape == (sum(IN_WIDTHS), D_MODEL) and sum(IN_WIDTHS[:7]) == _C_QD
    assert sum(IN_WIDTHS) - 3 * NSA_HEADS == _C_GN and _C_QD % 256 == 0 and _C_GN % 256 == 0
    row = lambda w: pl.BlockSpec((TM, w), lambda i: (i, 0))
    whole = lambda a: pl.BlockSpec(a.shape, lambda i: (0, 0), pipeline_mode=pl.Buffered(1))
    tab = pl.BlockSpec((TM, LANE), lambda i: (i % spt, 0))
    tile_t = lambda r: pl.BlockSpec((TM // T, r, T), lambda i: (i, 0, 0))
    out_shape = (
        jax.ShapeDtypeStruct((n, 512), BF16),
        jax.ShapeDtypeStruct((n, 512), BF16),
        jax.ShapeDtypeStruct((n, LANE), F32),
        jax.ShapeDtypeStruct((n, LANE), F32),
        jax.ShapeDtypeStruct((n, 256), BF16),
        jax.ShapeDtypeStruct((nt, LANE, T), BF16),
        jax.ShapeDtypeStruct((n, 256), BF16),
        jax.ShapeDtypeStruct((nt, LANE, T), BF16),
        jax.ShapeDtypeStruct((n, 512), BF16),
        jax.ShapeDtypeStruct((n, 512), BF16),
        jax.ShapeDtypeStruct((nt, 512, T), BF16),
        jax.ShapeDtypeStruct((n, 2048), BF16),
        jax.ShapeDtypeStruct((nt, GATE_ROWS, T), F32),
    )
    out_specs = (row(512), row(512), row(LANE), row(LANE), row(256), tile_t(LANE), row(256),
                 tile_t(LANE), row(512), row(512), tile_t(512), row(2048), tile_t(GATE_ROWS))
    return pl.pallas_call(
        _inproj_kernel,
        out_shape=out_shape,
        grid=(n // TM,),
        in_specs=[row(D_MODEL), whole(w), tab, tab, tab, tab],
        out_specs=out_specs,
        scratch_shapes=[pltpu.VMEM((_C_END, D_MODEL), BF16)],
        compiler_params=_params(1, VMEM_LIMIT),
        name="inproj",
    )(x2, w, cos_t, sa_t, sb_t, oh_t)


def _gelu_tanh(x):
    c = math.sqrt(2.0 / math.pi)
    return x * (0.5 * (1.0 + jnp.tanh(c * (x + 0.044715 * (x * x * x)))))


def _compress_kernel(kf_ref, vf_ref, pet_ref, peb_ref, wt_ref, wb_ref, b1_ref, w2k_ref, w2v_ref,
                     kca_ref, vct_ref):
    nchunk = kf_ref.shape[0] // CMP_STRIDE

    def hidden(x_ref, kv):
        a = jnp.zeros((nchunk, 2 * CMP_HIDDEN), F32)
        b = jnp.zeros((nchunk, 2 * CMP_HIDDEN), F32)
        for p in range(0, CMP_STRIDE, 2):
            x = jnp.concatenate([x_ref[pl.ds(p, nchunk, stride=CMP_STRIDE), :],
                                 x_ref[pl.ds(p + 1, nchunk, stride=CMP_STRIDE), :]], axis=1)
            c0, c1 = LANE * p, LANE * (p + 2)
            a = a + _nn((x + pet_ref[kv, :, c0:c1]).astype(BF16), wt_ref[kv, c0:c1, :])
            b = b + _nn((x + peb_ref[kv, :, c0:c1]).astype(BF16), wb_ref[kv, c0:c1, :])
        h = a + pltpu.roll(b, nchunk - 1, 0) + b1_ref[kv]
        return _gelu_tanh(h).astype(BF16)

    hk = hidden(kf_ref, 0)
    for g in range(2):
        kca_ref[g] = _nn(hk, w2k_ref[g]).astype(BF16)
    hv = hidden(vf_ref, 1)
    vct_ref[...] = _nn(hv, w2v_ref[...]).T.astype(BF16)


def _compress(kf, vf, pet, peb, wt, wb, b1, w2k, w2v, nb, seq):
    full = lambda a: pl.BlockSpec(a.shape, lambda b: (0,) * a.ndim)
    return pl.pallas_call(
        _compress_kernel,
        out_shape=(jax.ShapeDtypeStruct((nb, 2, 128, LANE), BF16),
                   jax.ShapeDtypeStruct((nb, LANE, 128), BF16)),
        grid=(nb,),
        in_specs=[pl.BlockSpec((seq, LANE), lambda b: (b, 0)),
                  pl.BlockSpec((seq, LANE), lambda b: (b, 0)),
                  full(pet), full(peb), full(wt), full(wb), full(b1), full(w2k), full(w2v)],
        out_specs=(pl.BlockSpec((None, 2, 128, LANE), lambda b: (b, 0, 0, 0)),
                   pl.BlockSpec((None, LANE, 128), lambda b: (b, 0, 0))),
        compiler_params=_params(1, VMEM_LIMIT),
        name="compress",
    )(kf, vf, pet, peb, wt, wb, b1, w2k, w2v)


def _cmpsel_kernel(q_ref, kca_ref, vct_ref, gnt_ref, ocmp_ref, sel_ref, s_scr):
    nq = q_ref.shape[0] // T
    lane = lax.broadcasted_iota(jnp.int32, (T, LANE), 1)
    crow = lax.broadcasted_iota(jnp.int32, (128, T), 0)
    tcol0 = lax.broadcasted_iota(jnp.int32, (128, T), 1)
    tlane = lax.broadcasted_iota(jnp.int32, (1, T), 1)

    jj = lax.broadcasted_iota(jnp.int32, (32, 128), 0) * SLC_LEN
    cc = lax.broadcasted_iota(jnp.int32, (32, 128), 1) * CMP_STRIDE
    ov = jnp.maximum(jnp.minimum(cc + CMP_LEN, jj + SLC_LEN) - jnp.maximum(cc, jj), 0)
    ovt = (ov.astype(F32) * (1.0 / CMP_LEN)).astype(BF16)

    jrow = lax.broadcasted_iota(jnp.int32, (32, T), 0)
    jrow8 = lax.broadcasted_iota(jnp.int32, (8, T), 0)
    tblk0 = lax.broadcasted_iota(jnp.int32, (32, T), 1)

    def scores(i, buf):
        rows = pl.ds(pl.multiple_of(i * T, T), T)
        for hp in range(NSA_HEADS // 2):
            g = hp // (NSA_HPG // 2)
            qp = q_ref[rows, LANE * hp:LANE * (hp + 1)].astype(F32)
            qs = (jnp.where(lane < 64, qp, 0.0), jnp.where(lane < 64, pltpu.roll(qp, 64, 1), 0.0))
            for par in range(2):
                s_scr[buf, 2 * hp + par] = _nt(kca_ref[g], qs[par].astype(BF16))

    def attend_select(i, buf):
        t0 = i * T
        rows = pl.ds(pl.multiple_of(t0, T), T)
        cmask = (crow * CMP_STRIDE + (CMP_LEN - 1)) <= tcol0 + t0
        live = jnp.where(tlane + t0 >= (CMP_LEN - 1), 1.0, 0.0)
        tblk = (tblk0 + t0) // SLC_LEN
        valid = jrow <= tblk
        forced = (jrow == 0) | (jrow == tblk) | (jrow == tblk - 1)
        bonus = jnp.where(forced, FORCE_BONUS, 0.0)
        for g in range(NSA_GROUPS):
            vt = vct_ref[64 * g:64 * (g + 1), :]
            psum = jnp.zeros((128, T), F32)
            for hp in range(NSA_HPG // 2):
                outs = []
                for par in range(2):
                    h = NSA_HPG * g + 2 * hp + par
                    s = jnp.where(cmask, s_scr[buf, h], NEG)
                    m = jnp.max(s, axis=0, keepdims=True)
                    e = jnp.exp2(s - m)
                    p = e * (live / jnp.sum(e, axis=0, keepdims=True))
                    psum = psum + p
                    outs.append(_nn(vt, p.astype(BF16)) * gnt_ref[i, 3 * h:3 * h + 1, :])
                ocmp_ref[rows, LANE * (2 * g + hp):LANE * (2 * g + hp + 1)] = (
                    jnp.concatenate(outs, axis=0).T.astype(BF16))

            p_hi = psum.astype(BF16)
            p_lo = (psum - p_hi.astype(F32)).astype(BF16)
            pslc = _nn(ovt, p_hi) + _nn(ovt, p_lo)
            pri = jnp.where(valid, pslc + bonus, -1.0)
            rank = [jnp.zeros((8, T), F32) for _ in range(4)]
            for r in range(32):
                row = pri[r:r + 1, :]
                for a in range(4):
                    pa = pri[8 * a:8 * (a + 1), :]
                    if 8 * a > r:
                        ahead = jnp.where(row >= pa, 1.0, 0.0)
                    elif 8 * a + 7 < r:
                        ahead = jnp.where(row > pa, 1.0, 0.0)
                    else:
                        ahead = jnp.where(jrow8 + 8 * a > r, jnp.where(row >= pa, 1.0, 0.0),
                                          jnp.where(row > pa, 1.0, 0.0))
                    rank[a] = rank[a] + ahead
            rank = jnp.concatenate(rank, axis=0)
            selneg = jnp.where(rank < float(SLC_TOPK), 0.0, NEG)
            pad = jnp.concatenate([jnp.zeros((64, T), F32), selneg, jnp.zeros((32, T), F32)], axis=0)
            sel_ref[rows, LANE * g:LANE * (g + 1)] = pad.T.astype(BF16)

    scores(0, 0)

    def body(u, carry):
        scores(2 * u + 1, 1)
        attend_select(2 * u, 0)
        scores(jnp.minimum(2 * u + 2, nq - 1), 0)
        attend_select(2 * u + 1, 1)
        return carry

    lax.fori_loop(0, nq // 2, body, 0)


def _cmpsel(qraw, kca, vct, gnt, nb, seq):
    n = qraw.shape[0]
    nq = seq // T
    assert nq % 2 == 0
    return pl.pallas_call(
        _cmpsel_kernel,
        out_shape=(jax.ShapeDtypeStruct((n, 512), BF16), jax.ShapeDtypeStruct((n, 256), BF16)),
        grid=(nb,),
        in_specs=[pl.BlockSpec((seq, 512), lambda b: (b, 0)),
                  pl.BlockSpec((None, 2, 128, LANE), lambda b: (b, 0, 0, 0)),
                  pl.BlockSpec((None, LANE, 128), lambda b: (b, 0, 0)),
                  pl.BlockSpec((nq, GATE_ROWS, T), lambda b: (b, 0, 0))],
        out_specs=(pl.BlockSpec((seq, 512), lambda b: (b, 0)),
                   pl.BlockSpec((seq, 256), lambda b: (b, 0))),
        scratch_shapes=[pltpu.VMEM((2, NSA_HEADS, 128, T), F32)],
        compiler_params=_params(1, VMEM_LIMIT),
        name="cmpsel",
    )(qraw, kca, vct, gnt)


def _split_heads(q_ref, extras, qs_scr, base=0):
    lane = lax.broadcasted_iota(jnp.int32, (T, LANE), 1)
    for hp in range(NSA_HEADS // 2):
        extra = extras[hp // (NSA_HPG // 2)]
        qp = q_ref[:, LANE * hp:LANE * (hp + 1)].astype(F32)
        qs_scr[base + 2 * hp] = jnp.where(lane < 64, qp, extra).astype(BF16)
        qs_scr[base + 2 * hp + 1] = jnp.where(lane < 64, pltpu.roll(qp, 64, 1), extra).astype(BF16)


SM_ROWS = 32


RING = 4


def _tile_schedule(nq):
    below = [(i, j) for j in range(nq) for i in range(j + 1, nq)]
    assert len(below) == (RING - 1) * nq + RING
    pairs = []
    for i in range(nq):
        pairs += [(i, i)] + below[(RING - 1) * i:(RING - 1) * (i + 1)]
    pairs += below[(RING - 1) * nq:]
    return (jnp.asarray([p[0] for p in pairs], jnp.int32), jnp.asarray([p[1] for p in pairs], jnp.int32))


def _flash_scratch(nq, nc, acc_rows):
    return [pltpu.VMEM((nq * nc, T, LANE), BF16),
            pltpu.VMEM((RING, nc, T, T), F32),
            pltpu.VMEM((RING, nc, 1, T), F32),
            pltpu.VMEM((nq * nc, 1, T), F32),
            pltpu.VMEM((nq * nc, 1, T), F32),
            pltpu.VMEM((nq * acc_rows // LANE, LANE, T), F32)]


def _causal_flash(it_ref, jt_ref, nq, k_ref, vt_ref, qs_scr, s_scr, mx_scr, m_scr, l_scr, acc_scr, chains, acc_of):
    nc = len(chains)
    nsets = it_ref.shape[0]
    assert nsets % RING == 0 and nsets >= 2 * RING
    m_scr[...] = jnp.full(m_scr.shape, NEG, F32)
    l_scr[...] = jnp.zeros(l_scr.shape, F32)
    acc_scr[...] = jnp.zeros(acc_scr.shape, F32)
    krow = lax.broadcasted_iota(jnp.int32, (T, T), 0)
    qcol = lax.broadcasted_iota(jnp.int32, (T, T), 1)
    ones = jnp.ones((16, T), BF16)

    def scores(n, buf):
        i, j = it_ref[n], jt_ref[n]
        if buf == 0:
            mask = krow <= qcol + jnp.where(i == j, 0, T)
        for c, (kl, _, _) in enumerate(chains):
            s = _nt(k_ref[j, :, kl:kl + LANE], qs_scr[i * nc + c])
            if buf == 0:
                s = jnp.where(mask, s, NEG)
            s_scr[buf, c] = s
            mx_scr[buf, c] = jnp.max(s, axis=0, keepdims=True)

    def softmax_pv(n, buf):
        i, j = it_ref[n], jt_ref[n]
        for c, (_, vr, dv) in enumerate(chains):
            m_prev = m_scr[i * nc + c]
            m_new = jnp.maximum(m_prev, mx_scr[buf, c])
            alpha = jnp.exp2(m_prev - m_new)
            parts = [jnp.exp2(s_scr[buf, c, r:r + SM_ROWS, :] - m_new).astype(BF16)
                     for r in range(0, T, SM_ROWS)]
            vt = jnp.concatenate([vt_ref[j, vr:vr + dv, :], ones], axis=0)
            o = _nn(vt, jnp.concatenate(parts, axis=0))
            l_scr[i * nc + c] = alpha * l_scr[i * nc + c] + o[dv:dv + 1]
            m_scr[i * nc + c] = m_new
            acc = acc_of(i, c)
            acc[...] = acc[...] * alpha + o[0:dv]

    scores(0, 0)

    def body(u, carry):
        n = RING * u
        for r in range(RING):
            scores(n + r + 1, (r + 1) % RING)
            softmax_pv(n + r, r)
        return carry

    lax.fori_loop(0, nsets // RING - 1, body, 0)
    n = nsets - RING
    for r in range(RING):
        if r + 1 < RING:
            scores(n + r + 1, r + 1)
        softmax_pv(n + r, r)


def _slc_kernel(it_ref, jt_ref, q_ref, sel_ref, k_ref, vt_ref, gnt_ref, o_ref,
                qs_scr, s_scr, mx_scr, m_scr, l_scr, acc_scr):
    nq = q_ref.shape[0] // T
    nh = NSA_HEADS

    def prep(i, carry):
        rows = pl.ds(pl.multiple_of(i * T, T), T)
        sel = sel_ref[rows, :].astype(F32)
        _split_heads(q_ref.at[rows, :], [sel[:, 0:LANE], sel[:, LANE:2 * LANE]], qs_scr, i * nh)
        return carry

    lax.fori_loop(0, nq, prep, 0)
    chains = [(LANE * (h // NSA_HPG), 64 * (h // NSA_HPG), 64) for h in range(nh)]
    acc_of = lambda i, h: acc_scr.at[i * (nh // 2) + h // 2, 64 * (h % 2):64 * (h % 2 + 1), :]
    _causal_flash(it_ref, jt_ref, nq, k_ref, vt_ref, qs_scr, s_scr, mx_scr, m_scr, l_scr, acc_scr, chains, acc_of)

    def finish(i, carry):
        rows = pl.ds(pl.multiple_of(i * T, T), T)
        for hp in range(nh // 2):
            scale = [gnt_ref[i, 3 * h + 1:3 * h + 2, :] / l_scr[i * nh + h] for h in (2 * hp, 2 * hp + 1)]
            inv = jnp.concatenate([jnp.broadcast_to(scale[0], (64, T)), jnp.broadcast_to(scale[1], (64, T))],
                                  axis=0)
            o_ref[rows, LANE * hp:LANE * (hp + 1)] = (acc_scr[i * (nh // 2) + hp] * inv).T.astype(BF16)
        return carry

    lax.fori_loop(0, nq, finish, 0)


def _slc(qrot, sel, ksa3, vst, gnt3, nb, seq):
    n = qrot.shape[0]
    nq = seq // T
    it, jt = _tile_schedule(nq)
    grid_spec = pltpu.PrefetchScalarGridSpec(
        num_scalar_prefetch=2,
        grid=(nb,),
        in_specs=[pl.BlockSpec((seq, 512), lambda b, it, jt: (b, 0)),
                  pl.BlockSpec((seq, 256), lambda b, it, jt: (b, 0)),
                  pl.BlockSpec((nq, T, 256), lambda b, it, jt: (b, 0, 0)),
                  pl.BlockSpec((nq, LANE, T), lambda b, it, jt: (b, 0, 0)),
                  pl.BlockSpec((nq, GATE_ROWS, T), lambda b, it, jt: (b, 0, 0))],
        out_specs=pl.BlockSpec((seq, 512), lambda b, it, jt: (b, 0)),
        scratch_shapes=_flash_scratch(nq, NSA_HEADS, NSA_Q))
    return pl.pallas_call(
        _slc_kernel,
        out_shape=jax.ShapeDtypeStruct((n, 512), BF16),
        grid_spec=grid_spec,
        compiler_params=_params(1, VMEM_LIMIT),
        name="slc",
    )(it, jt, qrot, sel, ksa3, vst, gnt3)


def _win_kernel(q_ref, k_ref, vt_ref, gnt_ref, o_ref, qs_scr, s_scr, mx_scr, acc_scr):
    i = pl.program_id(1)
    _split_heads(q_ref, [0.0, 0.0], qs_scr)
    krow = lax.broadcasted_iota(jnp.int32, (T, T), 0)
    qcol = lax.broadcasted_iota(jnp.int32, (T, T), 1)
    span = WINDOW // T
    masks = {0: krow <= qcol, span: (krow > qcol) & (i >= span)}
    ones = jnp.ones((16, T), BF16)

    def scores(hp, buf):
        for par in range(2):
            h = 2 * hp + par
            g = h // NSA_HPG
            mx = None
            for d in range(span + 1):
                j = i - d
                s = _nt(k_ref[jnp.maximum(j, 0), :, LANE * g:LANE * (g + 1)], qs_scr[h])
                if d in masks:
                    s = jnp.where(masks[d], s, NEG)
                else:
                    s = s + jnp.where(j >= 0, 0.0, NEG)
                s_scr[buf, par, d] = s
                md = jnp.max(s, axis=0, keepdims=True)
                mx = md if mx is None else jnp.maximum(mx, md)
            mx_scr[buf, par] = mx

    def softmax_pv(hp, buf):
        for par in range(2):
            g = (2 * hp + par) // NSA_HPG
            m = mx_scr[buf, par]
            acc = jnp.zeros((64 + 16, T), F32)
            for d in range(span + 1):
                parts = [jnp.exp2(s_scr[buf, par, d, r:r + SM_ROWS, :] - m).astype(BF16)
                         for r in range(0, T, SM_ROWS)]
                vt = jnp.concatenate([vt_ref[jnp.maximum(i - d, 0), 64 * g:64 * (g + 1), :], ones], axis=0)
                acc = acc + _nn(vt, jnp.concatenate(parts, axis=0))
            gate = gnt_ref[3 * (2 * hp + par) + 2:3 * (2 * hp + par) + 3, :]
            acc_scr[hp, 64 * par:64 * (par + 1), :] = acc[0:64] * (gate / acc[64:65])

    npairs = NSA_HEADS // 2
    scores(0, 0)
    for hp in range(npairs):
        if hp + 1 < npairs:
            scores(hp + 1, (hp + 1) % 2)
        softmax_pv(hp, hp % 2)
    for hp in range(npairs):
        o_ref[:, LANE * hp:LANE * (hp + 1)] = acc_scr[hp].T.astype(BF16)


def _win(qrot, kwa3, vwt, gnt, nb, seq):
    n = qrot.shape[0]
    nq = seq // T
    return pl.pallas_call(
        _win_kernel,
        out_shape=jax.ShapeDtypeStruct((n, 512), BF16),
        grid=(nb, nq),
        in_specs=[pl.BlockSpec((T, 512), lambda b, i: (b * nq + i, 0)),
                  pl.BlockSpec((nq, T, 256), lambda b, i: (b, 0, 0)),
                  pl.BlockSpec((nq, LANE, T), lambda b, i: (b, 0, 0)),
                  pl.BlockSpec((None, GATE_ROWS, T), lambda b, i: (b * nq + i, 0, 0))],
        out_specs=pl.BlockSpec((T, 512), lambda b, i: (b * nq + i, 0)),
        scratch_shapes=[pltpu.VMEM((NSA_HEADS, T, LANE), BF16),
                        pltpu.VMEM((2, 2, WINDOW // T + 1, T, T), F32),
                        pltpu.VMEM((2, 2, 1, T), F32),
                        pltpu.VMEM((NSA_HEADS // 2, LANE, T), F32)],
        compiler_params=_params(2, VMEM_LIMIT),
        name="win",
    )(qrot, kwa3, vwt, gnt)


def _diff_kernel(it_ref, jt_ref, q_ref, k_ref, vt_ref, lam_ref, g_ref, o_ref,
                 qs_scr, s_scr, mx_scr, m_scr, l_scr, acc_scr):
    nq = q_ref.shape[0] // T
    nc = 2 * DIFF_HEADS
    lane = lax.broadcasted_iota(jnp.int32, (T, LANE), 1)

    def prep(i, carry):
        rows = pl.ds(pl.multiple_of(i * T, T), T)
        for h in range(DIFF_HEADS):
            q = q_ref[rows, LANE * h:LANE * (h + 1)].astype(F32)
            qs_scr[i * nc + 2 * h] = jnp.where(lane < 64, q, 0.0).astype(BF16)
            qs_scr[i * nc + 2 * h + 1] = jnp.where(lane < 64, 0.0, q).astype(BF16)
        return carry

    lax.fori_loop(0, nq, prep, 0)
    chains = [(LANE * (c // 2), LANE * (c // 2), LANE) for c in range(nc)]
    _causal_flash(it_ref, jt_ref, nq, k_ref, vt_ref, qs_scr, s_scr, mx_scr, m_scr, l_scr, acc_scr, chains,
                  lambda i, c: acc_scr.at[i * nc + c])

    lp = lam_ref[...]
    lam = (jnp.exp(jnp.sum(lp[0:1] * lp[1:2], axis=1, keepdims=True))
           - jnp.exp(jnp.sum(lp[2:3] * lp[3:4], axis=1, keepdims=True)) + LAMBDA_INIT)

    def finish(i, carry):
        rows = pl.ds(pl.multiple_of(i * T, T), T)
        for h in range(DIFF_HEADS):
            c = i * nc + 2 * h
            o = acc_scr[c] * (1.0 / l_scr[c]) - lam * (acc_scr[c + 1] * (1.0 / l_scr[c + 1]))
            o = o * lax.rsqrt(jnp.mean(o * o, axis=0, keepdims=True) + RMS_EPS)
            o_ref[rows, LANE * h:LANE * (h + 1)] = ((o.T * g_ref[...]) * (1.0 - LAMBDA_INIT)).astype(BF16)
        return carry

    lax.fori_loop(0, nq, finish, 0)


def _diff(qdf, kdf3, vdft, lam, norm_g, nb, seq):
    n = qdf.shape[0]
    nq = seq // T
    it, jt = _tile_schedule(nq)
    grid_spec = pltpu.PrefetchScalarGridSpec(
        num_scalar_prefetch=2,
        grid=(nb,),
        in_specs=[pl.BlockSpec((seq, 512), lambda b, it, jt: (b, 0)),
                  pl.BlockSpec((nq, T, 512), lambda b, it, jt: (b, 0, 0)),
                  pl.BlockSpec((nq, 512, T), lambda b, it, jt: (b, 0, 0)),
                  pl.BlockSpec((4, HEAD_DIM), lambda b, it, jt: (0, 0)),
                  pl.BlockSpec((1, LANE), lambda b, it, jt: (0, 0))],
        out_specs=pl.BlockSpec((seq, 512), lambda b, it, jt: (b, 0)),
        scratch_shapes=_flash_scratch(nq, 2 * DIFF_HEADS, 2 * DIFF_HEADS * DIFF_V_DIM))
    return pl.pallas_call(
        _diff_kernel,
        out_shape=jax.ShapeDtypeStruct((n, 512), BF16),
        grid_spec=grid_spec,
        compiler_params=_params(1, VMEM_LIMIT),
        name="diff",
    )(it, jt, qdf, kdf3, vdft, lam, norm_g)


def _layer_norm(z, g, b):
    mu = jnp.mean(z, axis=-1, keepdims=True)
    zc = z - mu
    var = jnp.mean(zc * zc, axis=-1, keepdims=True)
    return zc * lax.rsqrt(var + LN_EPS) * g + b


WGU_ROWS = 128
WD_ROWS = FFN_HIDDEN // 8


def _load_cast(src_hbm, dst_scr, stg, sem, rows):
    nchunk = dst_scr.shape[0] // rows

    def cp(k):
        return pltpu.make_async_copy(src_hbm.at[0, k * rows:(k + 1) * rows, :], stg.at[k % 2], sem.at[k % 2])

    cp(0).start()
    cp(1).start()
    for k in range(nchunk):
        cp(k).wait()
        dst_scr[k * rows:(k + 1) * rows, :] = stg[k % 2].astype(BF16)
        if k + 2 < nchunk:
            cp(k + 2).start()


def _mlp_kernel(x_ref, ocmp_ref, oslc_ref, owin_ref, yb_ref, gm_ref, wa_ref, wb_ref, wo_ref, g1_ref, b1_ref,
                wgu_hbm, wd_hbm, g2_ref, b2_ref, o_ref, h_scr, wgu_ref, wd_ref, stg_gu, stg_d, sem_gu, sem_d):
    s = pl.program_id(0)

    @pl.when(s == 0)
    def _():
        h_scr[1] = jnp.zeros(h_scr.shape[1:], F32)
        _load_cast(wgu_hbm, wgu_ref, stg_gu, sem_gu, WGU_ROWS)
        _load_cast(wd_hbm, wd_ref, stg_d, sem_d, WD_ROWS)

    for u in range(TM // T):
        rs = slice(T * u, T * (u + 1))
        h = h_scr[(s + 1) % 2, rs, :]
        hb = h.astype(BF16)
        gate = _nn(hb, wgu_ref[:, 0:FFN_HIDDEN])
        up = _nn(hb, wgu_ref[:, FFN_HIDDEN:2 * FFN_HIDDEN])
        act = (gate * jax.nn.sigmoid(gate) * up).astype(BF16)
        y = _nn(act, wd_ref[...])
        o_ref[rs, :] = _layer_norm(DEEPNORM_ALPHA * h + y, g2_ref[...], b2_ref[...])

        ya = ocmp_ref[rs, :].astype(F32) + oslc_ref[rs, :].astype(F32) + owin_ref[rs, :].astype(F32)
        ta = _nn(ya.astype(BF16), wa_ref[...])
        tb = _nn(yb_ref[rs, :], wb_ref[...])
        merged = gm_ref[rs, 0:D_MODEL].astype(F32) * ta + gm_ref[rs, D_MODEL:2 * D_MODEL].astype(F32) * tb
        mix = _nn(merged.astype(BF16), wo_ref[...])
        h_scr[s % 2, rs, :] = _layer_norm(DEEPNORM_ALPHA * x_ref[rs, :] + mix, g1_ref[...], b1_ref[...])


def _mlp(x2, ocmp, oslc, owin, yb, gm, wa, wb, wo, g1, b1, wgu, wd, g2, b2):
    n = x2.shape[0]
    nt = n // TM
    row = lambda w: pl.BlockSpec((TM, w), lambda s: (jnp.minimum(s, nt - 1), 0))
    full = lambda a: pl.BlockSpec(a.shape, lambda s: (0,) * a.ndim, pipeline_mode=pl.Buffered(1))
    hbm = pl.BlockSpec(memory_space=pl.ANY)
    assert wgu.shape == (1, D_MODEL, 2 * FFN_HIDDEN) and wd.shape == (1, FFN_HIDDEN, D_MODEL)
    return pl.pallas_call(
        _mlp_kernel,
        out_shape=jax.ShapeDtypeStruct((n, D_MODEL), F32),
        grid=(nt + 1,),
        in_specs=[row(D_MODEL), row(512), row(512), row(512), row(512), row(2048),
                  full(wa), full(wb), full(wo), full(g1), full(b1), hbm, hbm, full(g2), full(b2)],
        out_specs=pl.BlockSpec((TM, D_MODEL), lambda s: (jnp.maximum(s - 1, 0), 0)),
        scratch_shapes=[pltpu.VMEM((2, TM, D_MODEL), F32),
                        pltpu.VMEM((D_MODEL, 2 * FFN_HIDDEN), BF16),
                        pltpu.VMEM((FFN_HIDDEN, D_MODEL), BF16),
                        pltpu.VMEM((2, WGU_ROWS, 2 * FFN_HIDDEN), F32),
                        pltpu.VMEM((2, WD_ROWS, D_MODEL), F32),
                        pltpu.SemaphoreType.DMA((2,)),
                        pltpu.SemaphoreType.DMA((2,))],
        compiler_params=_params(1, VMEM_LIMIT),
        name="mlp",
    )(x2, ocmp, oslc, owin, yb, gm, wa, wb, wo, g1, b1, wgu, wd, g2, b2)


def _rope_tables(seq):
    half = ROPE_DIM // 2
    inv_freq = ROPE_THETA ** (-jnp.arange(half, dtype=F32) * 2.0 / ROPE_DIM)
    ang = jnp.arange(seq, dtype=F32)[:, None] * inv_freq[None, :]
    cos, sin = jnp.cos(ang), jnp.sin(ang)
    ones = jnp.ones((seq, HEAD_DIM - ROPE_DIM), F32)
    zeros8 = jnp.zeros((seq, half), F32)
    zeros48 = jnp.zeros((seq, HEAD_DIM - ROPE_DIM), F32)
    c64 = jnp.concatenate([cos, cos, ones], axis=1)
    sa64 = jnp.concatenate([-sin, zeros8, zeros48], axis=1)
    sb64 = jnp.concatenate([zeros8, sin, zeros48], axis=1)
    rep = lambda a: jnp.concatenate([a, a], axis=1)
    pos_blk = jnp.arange(seq, dtype=jnp.int32)[:, None] // SLC_LEN
    lane = jnp.arange(LANE, dtype=jnp.int32)[None, :]
    onehot = ((lane >= 64) & (lane < 96) & (lane - 64 == pos_blk)).astype(F32)
    return rep(c64), rep(sa64), rep(sb64), onehot


def _prep_compress(cmp_pe, cmp_w1, cmp_b1, cmp_w2):
    half = CMP_LEN // 2
    pet, peb, wt, wb, b1 = [], [], [], [], []
    for kv in range(2):
        pe = cmp_pe[kv]
        tile2 = lambda a: jnp.concatenate([a, a], axis=1).reshape(1, half * 2 * HEAD_DIM)
        pet.append(tile2(pe[:half]))
        peb.append(tile2(pe[half:]))
        w1 = cmp_w1[kv].reshape(CMP_LEN, HEAD_DIM, CMP_HIDDEN)
        z = jnp.zeros((half, HEAD_DIM, CMP_HIDDEN), w1.dtype)

        def spread(wh):
            g0 = jnp.concatenate([wh, z], axis=2)
            g1 = jnp.concatenate([z, wh], axis=2)
            return jnp.stack([g0, g1], axis=1).reshape(half * 2 * HEAD_DIM, 2 * CMP_HIDDEN)

        wt.append(spread(w1[:half]))
        wb.append(spread(w1[half:]))
        b1.append(jnp.concatenate([cmp_b1[kv], cmp_b1[kv]])[None, :])
    w2k, w2v = cmp_w2[0], cmp_w2[1]
    zk = jnp.zeros_like(w2k)
    w2k_g = jnp.stack([
        jnp.concatenate([jnp.concatenate([w2k, zk], axis=1), jnp.zeros((CMP_HIDDEN, LANE), w2k.dtype)], axis=0),
        jnp.concatenate([jnp.zeros((CMP_HIDDEN, LANE), w2k.dtype), jnp.concatenate([w2k, zk], axis=1)], axis=0),
    ])
    zv = jnp.zeros_like(w2v)
    w2v_bd = jnp.concatenate([jnp.concatenate([w2v, zv], axis=1),
                              jnp.concatenate([zv, w2v], axis=1)], axis=0)
    st = lambda xs: jnp.stack(xs)
    return (st(pet), st(peb), st(wt).astype(BF16), st(wb).astype(BF16), st(b1),
            w2k_g.astype(BF16), w2v_bd.astype(BF16))


def kernel(x, w_in, cmp_pe, cmp_w1, cmp_b1, cmp_w2, diff_lambda, diff_norm_g, w_branch_a, w_branch_b,
           w_o, ln1_g, ln1_b, w_gate_up, w_down, ln2_g, ln2_b):
    nb, seq, d = x.shape
    assert d == D_MODEL and seq % T == 0 and seq // SLC_LEN == 32 and seq // CMP_STRIDE == 128
    n = nb * seq
    x2 = x.reshape(n, d)
    cos_t, sa_t, sb_t, oh_t = _rope_tables(seq)
    (qraw, qrot, kcs, vcs, ksa, vst, kwa, vwt, qdf, kdf, vdft, gm, gnt) = _inproj(
        x2, w_in.reshape(w_in.shape[1:]).T, cos_t, sa_t, sb_t, oh_t, seq)

    pet, peb, wt, wb, b1, w2k, w2v = _prep_compress(cmp_pe[0], cmp_w1[0], cmp_b1[0], cmp_w2[0])
    kca, vct = _compress(kcs, vcs, pet, peb, wt, wb, b1, w2k, w2v, nb, seq)

    ocmp, sel = _cmpsel(qraw, kca, vct, gnt, nb, seq)
    nt = n // T
    oslc = _slc(qrot, sel, ksa.reshape(nt, T, 256), vst, gnt, nb, seq)
    owin = _win(qrot, kwa.reshape(nt, T, 256), vwt, gnt, nb, seq)
    yb = _diff(qdf, kdf.reshape(nt, T, 512), vdft, diff_lambda[0], diff_norm_g[0][None, :], nb, seq)

    out = _mlp(x2, ocmp, oslc, owin, yb, gm,
               w_branch_a[0].astype(BF16), w_branch_b[0].astype(BF16), w_o[0].astype(BF16),
               ln1_g[0][None, :], ln1_b[0][None, :],
               w_gate_up, w_down, ln2_g[0][None, :], ln2_b[0][None, :])
    return out.reshape(nb, seq, d)
```
